```python
import math
import jax, jax.numpy as jnp
from jax import lax
import numpy as np

D_MODEL = 2048
BATCH = 4
SEQ = 2048
DEPTH = 4
DEC_BATCH = 8
DEC_SEQ = 1
PAST_LEN = 16384
PAGE_SIZE = 128

N_BRANCH = 4
BRANCH_W = D_MODEL // 4
GLA_HEADS = 4
GLA_DV = BRANCH_W // GLA_HEADS
GLA_DK = GLA_DV // 2
GLA_RANK = 16
GLA_TAU = 16.0
NSA_HEADS = 4
NSA_KV = 2
NSA_R = NSA_HEADS // NSA_KV
NSA_DH = BRANCH_W // NSA_HEADS
CMP_BLOCK = 32
SEL_BLOCK = 64
SEL_TOPK = 16
WINDOW = 512
SEL_QCHUNK = 16
WIN_QBLOCK = 128
FORCE_SCORE = 1e4
NEG = -1e30
SSD_HEADS = 8
SSD_P = BRANCH_W // SSD_HEADS
SSD_GROUPS = 2
SSD_N = 128
SSD_CONV = 4
SSD_CONV_CH = BRANCH_W + 2 * SSD_GROUPS * SSD_N
ML_HEADS = 4
ML_DH = BRANCH_W // ML_HEADS
CHUNK = 64
D_FF = ((8 * D_MODEL + 3 * 256 - 1) // (3 * 256)) * 256

IN_SPLITS = (
    ('gla_q', GLA_HEADS * GLA_DK), ('gla_k', GLA_HEADS * GLA_DK), ('gla_v', BRANCH_W),
    ('gla_r', BRANCH_W), ('gla_a', GLA_RANK),
    ('nsa_q', BRANCH_W), ('nsa_kv', 6 * NSA_KV * NSA_DH), ('nsa_g', 3 * NSA_HEADS),
    ('ssd_z', BRANCH_W), ('ssd_xbc', SSD_CONV_CH), ('ssd_dt', SSD_HEADS),
    ('ml_q', BRANCH_W), ('ml_k', BRANCH_W), ('ml_v', BRANCH_W), ('ml_o', BRANCH_W),
    ('ml_i', ML_HEADS), ('ml_f', ML_HEADS),
    ('merge', N_BRANCH * D_MODEL),
)
D_IN = sum(w for _, w in IN_SPLITS)

kernel_name = 'hybrid_gla_nsa_ssd_mlstm_step'


def rmsnorm(x, g, eps=1e-6):
    xf = x.astype(jnp.float32)
    y = xf * lax.rsqrt(jnp.mean(xf * xf, axis=-1, keepdims=True) + eps)
    return (y * g.astype(jnp.float32)).astype(x.dtype)


def split_cols(z):
    parts = {}
    off = 0
    for name, width in IN_SPLITS:
        parts[name] = z[..., off:off + width]
        off += width
    return parts


def alibi_slopes():
    h = jnp.arange(1, NSA_HEADS + 1, dtype=jnp.float32)
    return (2.0 ** (-8.0 * h / NSA_HEADS)).reshape(NSA_KV, NSA_R)


def gated_linear_scan(q, k, v, logg, s0):
    b_, t_, h_, dk = q.shape
    dv = v.shape[-1]
    c = math.gcd(t_, CHUNK)
    n = t_ // c
    causal = jnp.tril(jnp.ones((c, c), bool))

    def to_chunks(a):
        return a.astype(jnp.float32).reshape(b_, n, c, *a.shape[2:]).swapaxes(0, 1)

    def step(s, inp):
        qc, kc, vc, gc = inp
        b = jnp.cumsum(gc, axis=1)
        diff = b[:, :, None] - b[:, None, :]
        decay = jnp.exp(jnp.where(causal[None, :, :, None, None], diff, -jnp.inf))
        a = jnp.einsum('bthd,bshd,btshd->bhts', qc, kc, decay)
        o = jnp.einsum('bhts,bshv->bthv', a, vc) + jnp.einsum('bthd,bhdv->bthv', qc * jnp.exp(b), s)
        bl = b[:, -1]
        s = jnp.exp(bl)[..., None] * s + jnp.einsum('bshd,bshv->bhdv', kc * jnp.exp(bl[:, None] - b), vc)
        return s, o

    s, o = lax.scan(step, s0.astype(jnp.float32), (to_chunks(q), to_chunks(k), to_chunks(v), to_chunks(logg)))
    return o.swapaxes(0, 1).reshape(b_, t_, h_, dv), s


def mlstm_scan(q, k, v, ipre, logf, c0, n0, m0):
    b_, t_, h_, d = q.shape
    c = math.gcd(t_, CHUNK)
    n = t_ // c
    causal = jnp.tril(jnp.ones((c, c), bool))

    def to_chunks(a):
        return a.astype(jnp.float32).reshape(b_, n, c, *a.shape[2:]).swapaxes(0, 1)

    def step(carry, inp):
        cs, ns, ms = carry
        qc, kc, vc, ic, fc = inp
        b = jnp.cumsum(fc, axis=1)
        logd = b[:, :, None, :] - b[:, None, :, :] + ic[:, None, :, :]
        logd = jnp.where(causal[None, :, :, None], logd, -jnp.inf)
        g = b + ms[:, None, :]
        m = jnp.maximum(g, jnp.max(logd, axis=2))
        dm = jnp.exp(logd - m[:, :, None, :])
        wi = jnp.exp(g - m)
        qk = jnp.einsum('bthd,bshd->btsh', qc, kc) * dm
        num = jnp.einsum('btsh,bshv->bthv', qk, vc) + wi[..., None] * jnp.einsum('bthd,bhdv->bthv', qc, cs)
        den = jnp.sum(qk, axis=2) + wi * jnp.einsum('bthd,bhd->bth', qc, ns)
        h = num / jnp.maximum(jnp.abs(den), jnp.exp(-m))[..., None]
        m_new = m[:, -1]
        ws = jnp.exp(b[:, -1:] - b + ic - m_new[:, None])
        wc = jnp.exp(b[:, -1] + ms - m_new)
        cs = wc[..., None, None] * cs + jnp.einsum('bsh,bshd,bshv->bhdv', ws, kc, vc)
        ns = wc[..., None] * ns + jnp.einsum('bsh,bshd->bhd', ws, kc)
        return (cs, ns, m_new), h

    carry0 = (c0.astype(jnp.float32), n0.astype(jnp.float32), m0.astype(jnp.float32))
    (cs, ns, ms), h = lax.scan(step, carry0, tuple(to_chunks(a) for a in (q, k, v, ipre, logf)))
    return h.swapaxes(0, 1).reshape(b_, t_, h_, d), cs, ns, ms


def nsa_compressed(qg, k, v, q_pos, slopes):
    b_, l_ = k.shape[:2]
    nc = l_ // CMP_BLOCK
    kc = k[:, :nc * CMP_BLOCK].reshape(b_, nc, CMP_BLOCK, NSA_KV, NSA_DH).mean(axis=2)
    vc = v[:, :nc * CMP_BLOCK].reshape(b_, nc, CMP_BLOCK, NSA_KV, NSA_DH).mean(axis=2)
    blk = jnp.arange(nc)
    valid = (blk * CMP_BLOCK + CMP_BLOCK - 1)[None, :] <= q_pos[:, None]
    dist = q_pos[:, None].astype(jnp.float32) - (blk.astype(jnp.float32) * CMP_BLOCK + (CMP_BLOCK - 1) / 2.0)[None, :]
    s = jnp.einsum('btgrd,bngd->bgrtn', qg, kc) - slopes[None, :, :, None, None] * dist[None, None, None]
    s = jnp.where(valid, s, NEG)
    pr = jax.nn.softmax(s, axis=-1) * valid
    return jnp.einsum('bgrtn,bngd->btgrd', pr, vc), pr


def nsa_selected(qg, k, v, p_cmp, q_pos, slopes):
    b_, l_ = k.shape[:2]
    t_ = qg.shape[1]
    ns = -(-l_ // SEL_BLOCK)
    pad = ((0, 0), (0, ns * SEL_BLOCK - l_), (0, 0), (0, 0))
    kb = jnp.pad(k, pad).reshape(b_, ns, SEL_BLOCK, NSA_KV, NSA_DH).transpose(0, 3, 1, 2, 4)
    vb = jnp.pad(v, pad).reshape(b_, ns, SEL_BLOCK, NSA_KV, NSA_DH).transpose(0, 3, 1, 2, 4)
    ratio = SEL_BLOCK // CMP_BLOCK
    imp = jnp.sum(p_cmp, axis=2)
    nc = imp.shape[-1]
    imp = jnp.pad(imp, ((0, 0), (0, 0), (0, 0), (0, ns * ratio - nc))).reshape(b_, NSA_KV, t_, ns, ratio).sum(-1)
    blk = jnp.arange(ns)
    valid = (blk * SEL_BLOCK)[None, :] <= q_pos[:, None]
    forced = (blk[None, :] == 0) | (blk[None, :] == (q_pos // SEL_BLOCK)[:, None])
    score = jnp.where(valid, jnp.where(forced, FORCE_SCORE, imp), -1.0)
    top_s, idx = lax.top_k(score, min(SEL_TOPK, ns))
    kk = idx.shape[-1]
    ok = top_s >= 0.0
    qc_ = math.gcd(t_, SEL_QCHUNK)
    nq = t_ // qc_
    bi = jnp.arange(b_)[:, None, None, None]
    gi = jnp.arange(NSA_KV)[None, :, None, None]
    offs = jnp.arange(SEL_BLOCK)

    def chunk(args):
        qc, ic, okc, pc = args
        kg = kb[bi, gi, ic]
        vg = vb[bi, gi, ic]
        kpos = ic[..., None] * SEL_BLOCK + offs
        dist = pc[None, None, :, None, None] - kpos
        msk = okc[..., None] & (dist >= 0)
        s = jnp.einsum('bqgrd,bgqkjd->bgrqkj', qc, kg) - slopes[None, :, :, None, None, None] * dist[:, :, None].astype(jnp.float32)
        s = jnp.where(msk[:, :, None], s, NEG).reshape(b_, NSA_KV, NSA_R, qc_, kk * SEL_BLOCK)
        pr = jax.nn.softmax(s, axis=-1).reshape(b_, NSA_KV, NSA_R, qc_, kk, SEL_BLOCK)
        return jnp.einsum('bgrqkj,bgqkjd->bqgrd', pr, vg)

    xs = (qg.reshape(b_, nq, qc_, NSA_KV, NSA_R, NSA_DH).swapaxes(0, 1),
          idx.reshape(b_, NSA_KV, nq, qc_, kk).transpose(2, 0, 1, 3, 4),
          ok.reshape(b_, NSA_KV, nq, qc_, kk).transpose(2, 0, 1, 3, 4),
          q_pos.reshape(nq, qc_))
    out = lax.map(chunk, xs)
    return out.swapaxes(0, 1).reshape(b_, t_, NSA_KV, NSA_R, NSA_DH)


def nsa_window(qg, k, v, past_len, slopes):
    b_, t_ = qg.shape[:2]
    wb = k.shape[1] - t_
    pad = ((0, 0), (WINDOW - wb, 0), (0, 0), (0, 0))
    k = jnp.pad(k, pad)
    v = jnp.pad(v, pad)
    qb = math.gcd(t_, WIN_QBLOCK)
    nqb = t_ // qb
    idx = jnp.arange(nqb)[:, None] * qb + jnp.arange(qb + WINDOW)[None, :]
    kb = k[:, idx]
    vb = v[:, idx]
    kpos = past_len - WINDOW + idx
    qpos = past_len + jnp.arange(t_).reshape(nqb, qb)
    dist = qpos[:, :, None] - kpos[:, None, :]
    valid = (kpos[:, None, :] >= 0) & (dist >= 0) & (dist < WINDOW)
    qr = qg.reshape(b_, nqb, qb, NSA_KV, NSA_R, NSA_DH)
    s = jnp.einsum('bnqgrd,bnkgd->bngrqk', qr, kb) - slopes[None, None, :, :, None, None] * dist[None, :, None, None].astype(jnp.float32)
    s = jnp.where(valid[None, :, None, None], s, NEG)
    pr = jax.nn.softmax(s, axis=-1)
    return jnp.einsum('bngrqk,bnkgd->bnqgrd', pr, vb).reshape(b_, t_, NSA_KV, NSA_R, NSA_DH)


def nsa_mixer(nq, nkv, g_logits, past_kv, win_buf, past_len):
    b_, t_ = nq.shape[:2]
    f32 = jnp.float32
    qg = nq.astype(f32).reshape(b_, t_, NSA_KV, NSA_R, NSA_DH) * NSA_DH ** -0.5
    rows = nkv[:, :, :4]
    kv_all = jnp.concatenate([past_kv.astype(f32), rows.astype(f32)], axis=1)
    q_pos = past_len + jnp.arange(t_)
    slopes = alibi_slopes()
    o_cmp, p_cmp = nsa_compressed(qg, kv_all[:, :, 0], kv_all[:, :, 1], q_pos, slopes)
    o_slc = nsa_selected(qg, kv_all[:, :, 2], kv_all[:, :, 3], p_cmp, q_pos, slopes)
    win_all = jnp.concatenate([win_buf.astype(f32), nkv[:, :, 4:].astype(f32)], axis=1)
    o_win = nsa_window(qg, win_all[:, :, 0], win_all[:, :, 1], past_len, slopes)
    g = jax.nn.sigmoid(g_logits.astype(f32)).reshape(b_, t_, 3, NSA_KV, NSA_R)[..., None]
    o = g[:, :, 0] * o_cmp + g[:, :, 1] * o_slc + g[:, :, 2] * o_win
    keep = min(WINDOW, win_all.shape[1])
    new_win = win_all[:, win_all.shape[1] - keep:].astype(nkv.dtype)
    return o.reshape(b_, t_, BRANCH_W).astype(nq.dtype), rows, new_win


def mixer(h, p, st, past_len, past_kv):
    b_, t_, d_ = h.shape
    f32 = jnp.float32
    s = split_cols(h @ p['w_in'])
    gq = s['gla_q'].reshape(b_, t_, GLA_HEADS, GLA_DK) * GLA_DK ** -0.5
    gk = s['gla_k'].reshape(b_, t_, GLA_HEADS, GLA_DK)
    gv = s['gla_v'].reshape(b_, t_, GLA_HEADS, GLA_DV)
    ga = s['gla_a'] @ p['gla_w_a'] + p['gla_b_a']
    glog = (jax.nn.log_sigmoid(ga.astype(f32)) / GLA_TAU).reshape(b_, t_, GLA_HEADS, GLA_DK)
    go, gla_state = gated_linear_scan(gq, gk, gv, glog, st['gla'])
    go = rmsnorm(go, p['gla_norm_g'].reshape(GLA_HEADS, GLA_DV)).reshape(b_, t_, BRANCH_W)
    o_gla = go.astype(h.dtype) * jax.nn.silu(s['gla_r'])
    nq = s['nsa_q'].reshape(b_, t_, NSA_HEADS, NSA_DH)
    nkv = s['nsa_kv'].reshape(b_, t_, 6, NSA_KV, NSA_DH)
    o_nsa, nsa_rows, nsa_win = nsa_mixer(nq, nkv, s['nsa_g'], past_kv, st['win'], past_len)
    xbc = s['ssd_xbc']
    xp = jnp.concatenate([st['conv'].astype(xbc.dtype), xbc], axis=1)
    conv = p['ssd_conv_b']
    for j in range(SSD_CONV):
        conv = conv + xp[:, j:j + t_] * p['ssd_conv_w'][j]
    conv_state = xp[:, t_:]
    xbc_c = jax.nn.silu(conv)
    hpg = SSD_HEADS // SSD_GROUPS
    sx = xbc_c[..., :BRANCH_W].reshape(b_, t_, SSD_HEADS, SSD_P)
    sb = jnp.repeat(xbc_c[..., BRANCH_W:BRANCH_W + SSD_GROUPS * SSD_N].reshape(b_, t_, SSD_GROUPS, SSD_N), hpg, axis=2)
    sc = jnp.repeat(xbc_c[..., BRANCH_W + SSD_GROUPS * SSD_N:].reshape(b_, t_, SSD_GROUPS, SSD_N), hpg, axis=2)
    dt = jax.nn.softplus(s['ssd_dt'].astype(f32) + p['ssd_dt_bias'])
    a = -jnp.exp(p['ssd_a_log'].astype(f32))
    logg = jnp.broadcast_to((dt * a)[..., None], (b_, t_, SSD_HEADS, SSD_N))
    sy, ssd_state = gated_linear_scan(sc, sb, sx.astype(f32) * dt[..., None], logg, st['ssd'])
    sy = sy + p['ssd_d'][:, None] * sx
    o_ssd = rmsnorm(sy.reshape(b_, t_, BRANCH_W) * jax.nn.silu(s['ssd_z'].astype(f32)), p['ssd_norm_g']).astype(h.dtype)
    mq = s['ml_q'].reshape(b_, t_, ML_HEADS, ML_DH)
    mk = s['ml_k'].reshape(b_, t_, ML_HEADS, ML_DH) * ML_DH ** -0.5
    mv = s['ml_v'].reshape(b_, t_, ML_HEADS, ML_DH)
    ipre = s['ml_i'].astype(f32) + p['ml_b_i']
    logf = jax.nn.log_sigmoid(s['ml_f'].astype(f32) + p['ml_b_f'])
    mh, ml_c, ml_n, ml_m = mlstm_scan(mq, mk, mv, ipre, logf, st['ml_c'], st['ml_n'], st['ml_m'])
    mh = rmsnorm(mh, p['ml_norm_g'].reshape(ML_HEADS, ML_DH)).reshape(b_, t_, BRANCH_W)
    o_ml = (mh * jax.nn.sigmoid(s['ml_o'].astype(f32))).astype(h.dtype)
    branches = (o_gla, o_nsa, o_ssd, o_ml)
    gates = jax.nn.sigmoid(s['merge'].reshape(b_, t_, N_BRANCH, d_))
    merged = gates[:, :, 0] * (branches[0] @ p['w_branch'][0])
    for i in range(1, N_BRANCH):
        merged = merged + gates[:, :, i] * (branches[i] @ p['w_branch'][i])
    new_st = {'rows': nsa_rows, 'win': nsa_win, 'gla': gla_state, 'ssd': ssd_state, 'conv': conv_state,
              'ml_c': ml_c, 'ml_n': ml_n, 'ml_m': ml_m}
    return merged @ p['w_out'], new_st


def decoder_layer(x, c, p, st, past_len, past_kv):
    mod = jax.nn.silu(c) @ p['ada_w'] + p['ada_b']
    sh1, sc1, gt1, sh2, sc2, gt2 = jnp.split(mod[:, None, :], 6, axis=-1)
    h = rmsnorm(x, p['norm_mix_g']) * (1 + sc1) + sh1
    mix, new_st = mixer(h, p, st, past_len, past_kv)
    x = x + gt1 * mix
    h = rmsnorm(x, p['norm_ffn_g']) * (1 + sc2) + sh2
    ffn = (jax.nn.silu(h @ p['ffn_w_gate']) * (h @ p['ffn_w_up'])) @ p['ffn_w_down']
    return x + gt2 * ffn, new_st


def setup_inputs(seed: int = 0) -> dict:
    key = jax.random.key(seed)
    ks = iter(jax.random.split(key, 48))

    def nrm(shape, scale):
        return jax.random.normal(next(ks), shape, jnp.float32) * scale

    n_pages = PAST_LEN // PAGE_SIZE
    used = DEC_BATCH * n_pages
    n_pool = used + max(1, used // 4)
    win_buf = min(WINDOW, PAST_LEN)
    page_table = jax.random.permutation(next(ks), n_pool)[:used].reshape(DEC_BATCH, n_pages).astype(jnp.int32)
    dt = jnp.exp(jax.random.uniform(next(ks), (DEPTH, SSD_HEADS), jnp.float32, math.log(1e-3), math.log(1e-1)))
    dt_bias = dt + jnp.log(-jnp.expm1(-dt))
    a_log = jnp.log(jax.random.uniform(next(ks), (DEPTH, SSD_HEADS), jnp.float32, 1.0, 16.0))
    ds = D_MODEL ** -0.5
    return {
        'x_prompt': nrm((BATCH, SEQ, D_MODEL), 1.0),
        'x_sample': nrm((DEC_BATCH, DEC_SEQ, D_MODEL), 1.0),
        'cache_nsa_kv': nrm((DEPTH, n_pool, PAGE_SIZE, 4, NSA_KV, NSA_DH), 1.0),
        'cache_nsa_win': nrm((DEPTH, DEC_BATCH, win_buf, 2, NSA_KV, NSA_DH), 1.0),
        'state_gla': nrm((DEPTH, DEC_BATCH, GLA_HEADS, GLA_DK, GLA_DV), 0.3),
        'state_ssd': nrm((DEPTH, DEC_BATCH, SSD_HEADS, SSD_N, SSD_P), 0.3),
        'state_ssd_conv': nrm((DEPTH, DEC_BATCH, SSD_CONV - 1, SSD_CONV_CH), 1.0),
        'state_mlstm_c': nrm((DEPTH, DEC_BATCH, ML_HEADS, ML_DH, ML_DH), 0.3),
        'state_mlstm_n': nrm((DEPTH, DEC_BATCH, ML_HEADS, ML_DH), 0.3),
        'state_mlstm_m': nrm((DEPTH, DEC_BATCH, ML_HEADS), 1.0),
        'page_table': page_table,
        'c_prompt': nrm((BATCH, D_MODEL), 1.0),
        'c_sample': nrm((DEC_BATCH, D_MODEL), 1.0),
        'ada_w': nrm((DEPTH, D_MODEL, 6 * D_MODEL), ds),
        'ada_b': nrm((DEPTH, 6 * D_MODEL), 0.02),
        'norm_mix_g': 1.0 + nrm((DEPTH, D_MODEL), 0.02),
        'norm_ffn_g': 1.0 + nrm((DEPTH, D_MODEL), 0.02),
        'w_in': nrm((DEPTH, D_MODEL, D_IN), ds),
        'gla_w_a': nrm((DEPTH, GLA_RANK, GLA_HEADS * GLA_DK), GLA_RANK ** -0.5),
        'gla_b_a': nrm((DEPTH, GLA_HEADS * GLA_DK), 0.1),
        'gla_norm_g': 1.0 + nrm((DEPTH, BRANCH_W), 0.02),
        'ssd_conv_w': nrm((DEPTH, SSD_CONV, SSD_CONV_CH), 0.5),
        'ssd_conv_b': nrm((DEPTH, SSD_CONV_CH), 0.02),
        'ssd_dt_bias': dt_bias,
        'ssd_a_log': a_log,
        'ssd_d': 1.0 + nrm((DEPTH, SSD_HEADS), 0.1),
        'ssd_norm_g': 1.0 + nrm((DEPTH, BRANCH_W), 0.02),
        'ml_b_i': nrm((DEPTH, ML_HEADS), 0.1),
        'ml_b_f': jnp.linspace(3.0, 6.0, ML_HEADS, dtype=jnp.float32)[None, :] + nrm((DEPTH, ML_HEADS), 0.1),
        'ml_norm_g': 1.0 + nrm((DEPTH, BRANCH_W), 0.02),
        'w_branch': nrm((DEPTH, N_BRANCH, BRANCH_W, D_MODEL), BRANCH_W ** -0.5),
        'w_out': nrm((DEPTH, D_MODEL, D_MODEL), ds),
        'ffn_w_gate': nrm((DEPTH, D_MODEL, D_FF), ds),
        'ffn_w_up': nrm((DEPTH, D_MODEL, D_FF), ds),
        'ffn_w_down': nrm((DEPTH, D_FF, D_MODEL), D_FF ** -0.5),
        'final_norm_g': 1.0 + nrm((D_MODEL,), 0.02),
    }


def reference(x_prompt, x_sample, cache_nsa_kv, cache_nsa_win, state_gla, state_ssd, state_ssd_conv,
              state_mlstm_c, state_mlstm_n, state_mlstm_m, page_table, c_prompt, c_sample,
              ada_w, ada_b, norm_mix_g, norm_ffn_g, w_in, gla_w_a, gla_b_a, gla_norm_g,
              ssd_conv_w, ssd_conv_b, ssd_dt_bias, ssd_a_log, ssd_d, ssd_norm_g,
              ml_b_i, ml_b_f, ml_norm_g, w_branch, w_out, ffn_w_gate, ffn_w_up, ffn_w_down, final_norm_g):
    f32 = jnp.float32
    bp = x_prompt.shape[0]
    bs = x_sample.shape[0]
    past_len = page_table.shape[1] * PAGE_SIZE
    xp, xs = x_prompt, x_sample
    new_p, new_s = [], []
    for l in range(DEPTH):
        p = {'ada_w': ada_w[l], 'ada_b': ada_b[l], 'norm_mix_g': norm_mix_g[l], 'norm_ffn_g': norm_ffn_g[l],
             'w_in': w_in[l], 'gla_w_a': gla_w_a[l], 'gla_b_a': gla_b_a[l], 'gla_norm_g': gla_norm_g[l],
             'ssd_conv_w': ssd_conv_w[l], 'ssd_conv_b': ssd_conv_b[l], 'ssd_dt_bias': ssd_dt_bias[l],
             'ssd_a_log': ssd_a_log[l], 'ssd_d': ssd_d[l], 'ssd_norm_g': ssd_norm_g[l],
             'ml_b_i': ml_b_i[l], 'ml_b_f': ml_b_f[l], 'ml_norm_g': ml_norm_g[l],
             'w_branch': w_branch[l], 'w_out': w_out[l],
             'ffn_w_gate': ffn_w_gate[l], 'ffn_w_up': ffn_w_up[l], 'ffn_w_down': ffn_w_down[l]}
        st_p = {'gla': jnp.zeros((bp, GLA_HEADS, GLA_DK, GLA_DV), f32),
                'ssd': jnp.zeros((bp, SSD_HEADS, SSD_N, SSD_P), f32),
                'conv': jnp.zeros((bp, SSD_CONV - 1, SSD_CONV_CH), x_prompt.dtype),
                'ml_c': jnp.zeros((bp, ML_HEADS, ML_DH, ML_DH), f32),
                'ml_n': jnp.zeros((bp, ML_HEADS, ML_DH), f32),
                'ml_m': jnp.zeros((bp, ML_HEADS), f32),
                'win': jnp.zeros((bp, 0, 2, NSA_KV, NSA_DH), x_prompt.dtype)}
        empty_past = jnp.zeros((bp, 0, 4, NSA_KV, NSA_DH), x_prompt.dtype)
        xp, sp_l = decoder_layer(xp, c_prompt, p, st_p, 0, empty_past)
        new_p.append(sp_l)
        past_kv = cache_nsa_kv[l][page_table].reshape(bs, past_len, 4, NSA_KV, NSA_DH)
        st_s = {'gla': state_gla[l], 'ssd': state_ssd[l], 'conv': state_ssd_conv[l],
                'ml_c': state_mlstm_c[l], 'ml_n': state_mlstm_n[l], 'ml_m': state_mlstm_m[l],
                'win': cache_nsa_win[l]}
        xs, ss_l = decoder_layer(xs, c_sample, p, st_s, past_len, past_kv)
        new_s.append(ss_l)
    keys = ('rows', 'win', 'gla', 'ssd', 'conv', 'ml_c', 'ml_n', 'ml_m')
    sp = {k: jnp.stack([d[k] for d in new_p]) for k in keys}
    ss = {k: jnp.stack([d[k] for d in new_s]) for k in keys}
    y_prompt = rmsnorm(xp, final_norm_g)
    y_sample = rmsnorm(xs, final_norm_g)
    return (y_prompt, y_sample, sp['rows'], ss['rows'], sp['win'], ss['win'], sp['gla'], ss['gla'],
            sp['ssd'], ss['ssd'], sp['conv'], ss['conv'], sp['ml_c'], ss['ml_c'], sp['ml_n'], ss['ml_n'],
            sp['ml_m'], ss['ml_m'])
```

```python
import functools
import math

import jax
import jax.numpy as jnp
from jax import lax
from jax.experimental import pallas as pl
from jax.experimental.pallas import tpu as pltpu

F32 = jnp.float32
BF16 = jnp.bfloat16

PAGE_SIZE = 128
GLA_HEADS = 4
GLA_RANK = 16
GLA_TAU = 16.0
NSA_HEADS = 4
NSA_KV = 2
NSA_R = NSA_HEADS // NSA_KV
CMP_BLOCK = 32
SEL_BLOCK = 64
SEL_TOPK = 16
WINDOW = 512
FORCE_SCORE = 1e4
NEG = -1e30
SSD_HEADS = 8
SSD_GROUPS = 2
SSD_N = 128
SSD_CONV = 4
ML_HEADS = 4
CHUNK = 64
EPS = 1e-6

LANE = 128
SUBLANE = 8
VMEM_LIMIT = 56 * 1024 * 1024

SROWS = SUBLANE


def _cparams(sem):
    return pltpu.CompilerParams(dimension_semantics=sem, vmem_limit_bytes=VMEM_LIMIT)


def _layout(d_model):
    bw = d_model // 4
    gdk = (bw // GLA_HEADS) // 2
    conv_ch = bw + 2 * SSD_GROUPS * SSD_N
    splits = (('gla_q', GLA_HEADS * gdk), ('gla_k', GLA_HEADS * gdk), ('gla_v', bw), ('gla_r', bw),
              ('gla_a', GLA_RANK), ('nsa_q', bw), ('nsa_kv', 6 * NSA_KV * (bw // NSA_HEADS)),
              ('nsa_g', 3 * NSA_HEADS), ('ssd_z', bw), ('ssd_xbc', conv_ch), ('ssd_dt', SSD_HEADS),
              ('ml_q', bw), ('ml_k', bw), ('ml_v', bw), ('ml_o', bw), ('ml_i', ML_HEADS), ('ml_f', ML_HEADS),
              ('merge', 4 * d_model))
    src = {}
    off = 0
    for name, w in splits:
        src[name] = (off, w)
        off += w
    return src, off


_Z_ORDER = ('gla_q', 'gla_k', 'gla_v', 'gla_r', 'nsa_kv', 'ml_q', 'ml_k', 'ml_v', 'ml_o', 'ssd_xbc',
            'nsa_q', 'ssd_z', 'merge')
_Z_SMALL = ('gla_a', 'nsa_g', 'ssd_dt', 'ml_i', 'ml_f')


def _packed_offsets(d_model):
    src, d_in = _layout(d_model)
    dst = {}
    off = 0
    for name in _Z_ORDER:
        dst[name] = off
        off += src[name][1]
    small_off = off
    lane = 0
    for name in _Z_SMALL:
        dst[name] = lane
        lane += src[name][1]
    assert lane <= LANE
    return src, dst, small_off, small_off + LANE, d_in


def _pack_w_in(w_in):
    depth, d_model, _ = w_in.shape
    src, dst, small_off, n_packed, d_in = _packed_offsets(d_model)
    assert w_in.shape[2] == d_in
    parts = [w_in[:, :, src[n][0]:src[n][0] + src[n][1]] for n in _Z_ORDER]
    small = [w_in[:, :, src[n][0]:src[n][0] + src[n][1]] for n in _Z_SMALL]
    used = sum(src[n][1] for n in _Z_SMALL)
    small.append(jnp.zeros((depth, d_model, LANE - used), w_in.dtype))
    return jnp.concatenate(parts + small, axis=2).astype(BF16)


def _sigmoid(x):
    return 1.0 / (1.0 + jnp.exp(-x))


def _silu(x):
    return x * _sigmoid(x)


def _log_sigmoid(x):
    return jnp.minimum(x, 0.0) - jnp.log1p(jnp.exp(-jnp.abs(x)))


def _softplus(x):
    return jnp.maximum(x, 0.0) + jnp.log1p(jnp.exp(-jnp.abs(x)))


def _dot(a, b):
    return jnp.dot(a, b, preferred_element_type=F32)


def _dot_nt(a, b):
    return lax.dot_general(a, b, (((1,), (1,)), ((), ())), preferred_element_type=F32)


def _dot_tn(a, b):
    return lax.dot_general(a, b, (((0,), (0,)), ((), ())), preferred_element_type=F32)


def _bf(x):
    return x.astype(BF16)


def _split3(x):
    h = x.astype(BF16)
    r = x - h.astype(F32)
    m = r.astype(BF16)
    lo = (r - m.astype(F32)).astype(BF16)
    return h, m, lo


def _split2(x):
    h = x.astype(BF16)
    return h, (x - h.astype(F32)).astype(BF16)


def _tri(c, upper=False):
    r = lax.broadcasted_iota(jnp.int32, (c, c), 0)
    s = lax.broadcasted_iota(jnp.int32, (c, c), 1)
    keep = (r <= s) if upper else (r >= s)
    return keep, jnp.where(keep, 1.0, 0.0).astype(BF16)


def _cumsum_rows(tri_lo, x):
    h, m, lo = _split3(x)
    return _dot(tri_lo, h) + _dot(tri_lo, m) + _dot(tri_lo, lo)


def _cumsum_cols(x, tri_up):
    h, m, lo = _split3(x)
    return _dot(h, tri_up) + _dot(m, tri_up) + _dot(lo, tri_up)


def _rms(x, g):
    return x * lax.rsqrt(jnp.mean(x * x, axis=-1, keepdims=True) + EPS) * g


def _softmax_rows(s):
    m = jnp.max(s, axis=-1, keepdims=True)
    e = jnp.exp(s - m)
    return e / jnp.sum(e, axis=-1, keepdims=True)


def _alibi_slopes():
    return [2.0 ** (-8.0 * (h + 1) / NSA_HEADS) for h in range(NSA_HEADS)]


def _ada_kernel(c_ref, w_ref, b_ref, o_ref):
    a = _bf(_silu(c_ref[...]))
    o_ref[...] = _dot(a, _bf(w_ref[...])) + b_ref[...]


def _ada(c_all, ada_w, ada_b):
    depth, d, n = ada_w.shape
    rows = c_all.shape[0]
    tn = 1024
    return pl.pallas_call(
        _ada_kernel, grid=(depth, n // tn),
        in_specs=[pl.BlockSpec((rows, d), lambda l, j: (0, 0)),
                  pl.BlockSpec((None, d, tn), lambda l, j: (l, 0, j)),
                  pl.BlockSpec((None, 1, tn), lambda l, j: (l, 0, j))],
        out_specs=pl.BlockSpec((None, rows, tn), lambda l, j: (l, 0, j)),
        out_shape=jax.ShapeDtypeStruct((depth, rows, n), F32),
        compiler_params=_cparams(("arbitrary", "arbitrary")), name="ada",
    )(c_all, ada_w, ada_b.reshape(depth, 1, n))


def _norm_mod_kernel(x_ref, g_ref, sc_ref, sh_ref, o_ref):
    y = _rms(x_ref[...], g_ref[...])
    o_ref[...] = (y * (1.0 + sc_ref[...]) + sh_ref[...]).astype(o_ref.dtype)


def _norm_kernel(x_ref, g_ref, o_ref):
    o_ref[...] = _rms(x_ref[...], g_ref[...]).astype(o_ref.dtype)


def _norm_mod_prompt(x, g, modp, l, j_shift, j_scale, seq):
    m, d = x.shape
    tm = 512
    per = seq // tm
    return pl.pallas_call(
        _norm_mod_kernel, grid=(m // tm,),
        in_specs=[pl.BlockSpec((tm, d), lambda i: (i, 0)),
                  pl.BlockSpec((None, 1, d), lambda i: (l, 0, 0)),
                  pl.BlockSpec((None, None, 1, d), lambda i: (l, i // per, 0, j_scale)),
                  pl.BlockSpec((None, None, 1, d), lambda i: (l, i // per, 0, j_shift))],
        out_specs=pl.BlockSpec((tm, d), lambda i: (i, 0)),
        out_shape=jax.ShapeDtypeStruct((m, d), BF16),
        compiler_params=_cparams(("arbitrary",)), name="norm_mod_prompt",
    )(x, g, modp, modp)


def _norm_mod_sample(x, g, mods, l, j_shift, j_scale):
    m, d = x.shape
    return pl.pallas_call(
        _norm_mod_kernel, grid=(1,),
        in_specs=[pl.BlockSpec((m, d), lambda i: (0, 0)),
                  pl.BlockSpec((None, 1, d), lambda i: (l, 0, 0)),
                  pl.BlockSpec((None, m, d), lambda i: (l, 0, j_scale)),
                  pl.BlockSpec((None, m, d), lambda i: (l, 0, j_shift))],
        out_specs=pl.BlockSpec((m, d), lambda i: (0, 0)),
        out_shape=jax.ShapeDtypeStruct((m, d), BF16),
        compiler_params=_cparams(("arbitrary",)), name="norm_mod_sample",
    )(x, g, mods, mods)


def _final_norm(x, g):
    m, d = x.shape
    tm = min(m, 512)
    return pl.pallas_call(
        _norm_kernel, grid=(m // tm,),
        in_specs=[pl.BlockSpec((tm, d), lambda i: (i, 0)), pl.BlockSpec((1, d), lambda i: (0, 0))],
        out_specs=pl.BlockSpec((tm, d), lambda i: (i, 0)),
        out_shape=jax.ShapeDtypeStruct((m, d), F32),
        compiler_params=_cparams(("arbitrary",)), name="final_norm",
    )(x, g.reshape(1, d))


def _mm_plain_kernel(a_ref, as_ref, w_ref, o_ref, os_ref):
    w = w_ref[...]

    @pl.when(pl.program_id(1) == 0)
    def _():
        os_ref[...] = _dot(as_ref[...], w)

    o_ref[...] = _dot(a_ref[...], w)


def _mm_plain(a, a_s, w_packed, l, tm, tn):
    m, k = a.shape
    ms = a_s.shape[0]
    n = w_packed.shape[2]
    return pl.pallas_call(
        _mm_plain_kernel, grid=(n // tn, m // tm),
        in_specs=[pl.BlockSpec((tm, k), lambda j, i: (i, 0)),
                  pl.BlockSpec((ms, k), lambda j, i: (0, 0)),
                  pl.BlockSpec((None, k, tn), lambda j, i: (l, 0, j))],
        out_specs=[pl.BlockSpec((tm, tn), lambda j, i: (i, j)),
                   pl.BlockSpec((ms, tn), lambda j, i: (0, j))],
        out_shape=[jax.ShapeDtypeStruct((m, n), F32), jax.ShapeDtypeStruct((ms, n), F32)],
        compiler_params=_cparams(("arbitrary", "arbitrary")), name="proj_in",
    )(a, a_s, w_packed)


def _mm_resid_kernel(a_ref, as_ref, w_ref, x_ref, xs_ref, gt_ref, gts_ref, o_ref, os_ref, wb_ref):
    @pl.when(pl.program_id(1) == 0)
    def _():
        wb_ref[...] = _bf(w_ref[...])
        os_ref[...] = xs_ref[...] + gts_ref[...] * _dot(as_ref[...], wb_ref[...])

    o_ref[...] = x_ref[...] + gt_ref[...] * _dot(a_ref[...], wb_ref[...])


def _mm_resid(a, a_s, w, x, x_s, modp, mods, l, j_gate, seq, tm, tn, name):
    m, k = a.shape
    ms = a_s.shape[0]
    n = w.shape[2]
    per = seq // tm
    nj = n // tn
    return pl.pallas_call(
        _mm_resid_kernel, grid=(nj, m // tm),
        in_specs=[pl.BlockSpec((tm, k), lambda j, i: (i, 0)),
                  pl.BlockSpec((ms, k), lambda j, i: (0, 0)),
                  pl.BlockSpec((None, k, tn), lambda j, i: (l, 0, j)),
                  pl.BlockSpec((tm, tn), lambda j, i: (i, j)),
                  pl.BlockSpec((ms, tn), lambda j, i: (0, j)),
                  pl.BlockSpec((None, None, 1, tn), lambda j, i: (l, i // per, 0, j_gate * nj + j)),
                  pl.BlockSpec((None, ms, tn), lambda j, i: (l, 0, j_gate * nj + j))],
        out_specs=[pl.BlockSpec((tm, tn), lambda j, i: (i, j)),
                   pl.BlockSpec((ms, tn), lambda j, i: (0, j))],
        out_shape=[jax.ShapeDtypeStruct((m, n), F32), jax.ShapeDtypeStruct((ms, n), F32)],
        scratch_shapes=[pltpu.VMEM((k, tn), BF16)],
        compiler_params=_cparams(("arbitrary", "arbitrary")), name=name,
    )(a, a_s, w, x, x_s, modp, mods)


def _mm_swiglu_kernel(a_ref, as_ref, wg_ref, wu_ref, o_ref, os_ref, wgb_ref, wub_ref):
    @pl.when(pl.program_id(1) == 0)
    def _():
        wgb_ref[...] = _bf(wg_ref[...])
        wub_ref[...] = _bf(wu_ref[...])
        a_s = as_ref[...]
        os_ref[...] = (_silu(_dot(a_s, wgb_ref[...])) * _dot(a_s, wub_ref[...])).astype(os_ref.dtype)

    a = a_ref[...]
    o_ref[...] = (_silu(_dot(a, wgb_ref[...])) * _dot(a, wub_ref[...])).astype(o_ref.dtype)


def _mm_swiglu(a, a_s, wg, wu, l, tm, tn):
    m, k = a.shape
    ms = a_s.shape[0]
    n = wg.shape[2]
    return pl.pallas_call(
        _mm_swiglu_kernel, grid=(n // tn, m // tm),
        in_specs=[pl.BlockSpec((tm, k), lambda j, i: (i, 0)),
                  pl.BlockSpec((ms, k), lambda j, i: (0, 0)),
                  pl.BlockSpec((None, k, tn), lambda j, i: (l, 0, j)),
                  pl.BlockSpec((None, k, tn), lambda j, i: (l, 0, j))],
        out_specs=[pl.BlockSpec((tm, tn), lambda j, i: (i, j)),
                   pl.BlockSpec((ms, tn), lambda j, i: (0, j))],
        out_shape=[jax.ShapeDtypeStruct((m, n), BF16), jax.ShapeDtypeStruct((ms, n), BF16)],
        scratch_shapes=[pltpu.VMEM((k, tn), BF16), pltpu.VMEM((k, tn), BF16)],
        compiler_params=_cparams(("arbitrary", "arbitrary")), name="ffn_gate_up",
    )(a, a_s, wg, wu)


def _merge_kernel(*refs):
    o_p = refs[0:4]
    o_s = refs[4:8]
    g_p = refs[8:12]
    g_s = refs[12:16]
    w_ref, out_ref, outs_ref, wb_ref = refs[16:20]

    @pl.when(pl.program_id(1) == 0)
    def _():
        wb_ref[...] = _bf(w_ref[...])
        acc = None
        for b in range(4):
            t = _sigmoid(g_s[b][...]) * _dot(_bf(o_s[b][...]), wb_ref[b])
            acc = t if acc is None else acc + t
        outs_ref[...] = acc.astype(outs_ref.dtype)

    acc = None
    for b in range(4):
        t = _sigmoid(g_p[b][...]) * _dot(_bf(o_p[b][...]), wb_ref[b])
        acc = t if acc is None else acc + t
    out_ref[...] = acc.astype(out_ref.dtype)


def _merge(o_p, o_s, z, z_s, w_branch, l, merge_off, tm, tn):
    m = z.shape[0]
    ms = z_s.shape[0]
    bw = w_branch.shape[2]
    d = w_branch.shape[3]
    in_specs = ([pl.BlockSpec((tm, bw), lambda j, i: (i, 0)) for _ in range(4)]
                + [pl.BlockSpec((ms, bw), lambda j, i: (0, 0)) for _ in range(4)]
                + [pl.BlockSpec((tm, tn), lambda j, i, b=b: (i, (merge_off + b * d) // tn + j)) for b in range(4)]
                + [pl.BlockSpec((ms, tn), lambda j, i, b=b: (0, (merge_off + b * d) // tn + j)) for b in range(4)]
                + [pl.BlockSpec((None, 4, bw, tn), lambda j, i: (l, 0, 0, j))])
    return pl.pallas_call(
        _merge_kernel, grid=(d // tn, m // tm),
        in_specs=in_specs,
        out_specs=[pl.BlockSpec((tm, tn), lambda j, i: (i, j)), pl.BlockSpec((ms, tn), lambda j, i: (0, j))],
        out_shape=[jax.ShapeDtypeStruct((m, d), BF16), jax.ShapeDtypeStruct((ms, d), BF16)],
        scratch_shapes=[pltpu.VMEM((4, bw, tn), BF16)],
        compiler_params=_cparams(("arbitrary", "arbitrary")), name="merge",
    )(*o_p, *o_s, z, z, z, z, z_s, z_s, z_s, z_s, w_branch)


def _live_rows(c_rows, n_live):
    return lax.broadcasted_iota(jnp.int32, (c_rows, 1), 0) < n_live


def _gla_kernel(z_ref, small_ref, wa_ref, ba_ref, gn_ref, s0_ref, o_ref, sout_ref, st_scr, *, C, L, H, DK, DV):
    c = pl.program_id(1)

    @pl.when(c == 0)
    def _():
        for h in range(H):
            st_scr[h] = s0_ref[h].T

    z = z_ref[...]
    q = z[:, 0:H * DK] * (DK ** -0.5)
    k = z[:, H * DK:2 * H * DK]
    v = z[:, 2 * H * DK:2 * H * DK + H * DV]
    r = z[:, 2 * H * DK + H * DV:]
    ga = _dot(_bf(small_ref[...]), _bf(wa_ref[...])) + ba_ref[...]
    glog = _log_sigmoid(ga) * (1.0 / GLA_TAU)
    if L < C:
        live = _live_rows(C, L)
        glog = jnp.where(live, glog, 0.0)
        k = jnp.where(live, k, 0.0)
    causal, tri_lo = _tri(C)
    b = _cumsum_rows(tri_lo, glog)
    bl = b[C - 1:C, :]
    qe = _bf(q * jnp.exp(b))
    ke = _bf(k * jnp.exp(-b))
    kd = _bf(k * jnp.exp(bl - b))
    ebl = jnp.exp(bl)
    gn = gn_ref[...]
    outs = []
    for h in range(H):
        ks = slice(h * DK, (h + 1) * DK)
        vs = slice(h * DV, (h + 1) * DV)
        vh = _bf(v[:, vs])
        a = jnp.where(causal, _dot_nt(qe[:, ks], ke[:, ks]), 0.0)
        st = st_scr[h]
        o = _dot(_bf(a), vh) + _dot_nt(qe[:, ks], _bf(st))
        st_scr[h] = st * ebl[:, ks] + _dot_tn(vh, kd[:, ks])
        outs.append(_rms(o, gn[:, vs]))
    o_ref[...] = (jnp.concatenate(outs, axis=1) * _silu(r)).astype(o_ref.dtype)

    @pl.when(c == pl.num_programs(1) - 1)
    def _():
        for h in range(H):
            sout_ref[h] = st_scr[h].T


def _gla_call(z, small_idx, wa_pad, ba, gn, s0, l, nb, seq, C, L, out_dtype):
    H, DK, DV = s0.shape[1:]
    nch = seq // C
    width = 2 * H * DK + 2 * H * DV
    kern = functools.partial(_gla_kernel, C=C, L=L, H=H, DK=DK, DV=DV)
    return pl.pallas_call(
        kern, grid=(nb, nch),
        in_specs=[pl.BlockSpec((C, width), lambda b, c: (b * nch + c, 0)),
                  pl.BlockSpec((C, LANE), lambda b, c: (b * nch + c, small_idx)),
                  pl.BlockSpec((None, LANE, H * DK), lambda b, c: (l, 0, 0)),
                  pl.BlockSpec((None, 1, H * DK), lambda b, c: (l, 0, 0)),
                  pl.BlockSpec((None, 1, H * DV), lambda b, c: (l, 0, 0)),
                  pl.BlockSpec((None, H, DK, DV), lambda b, c: (b, 0, 0, 0))],
        out_specs=[pl.BlockSpec((C, H * DV), lambda b, c: (b * nch + c, 0)),
                   pl.BlockSpec((None, H, DK, DV), lambda b, c: (b, 0, 0, 0))],
        out_shape=[jax.ShapeDtypeStruct((nb * seq, H * DV), out_dtype),
                   jax.ShapeDtypeStruct((nb, H, DK, DV), F32)],
        scratch_shapes=[pltpu.VMEM((H, DV, DK), F32)],
        compiler_params=_cparams(("arbitrary", "arbitrary")), name="gla",
    )(z, z, wa_pad, ba, gn, s0)


def _ssd_kernel(xbc_ref, zg_ref, small_ref, cw_ref, cb_ref, dtb_ref, alog_ref, dvec_ref, gn_ref, s0_ref, conv0_ref,
                o_ref, sout_ref, convout_ref, s_scr, ext_scr, *, C, L, H, G, N, P, dt_lane):
    c = pl.program_id(1)
    kh = SSD_CONV - 1
    base = SUBLANE - kh

    @pl.when(c == 0)
    def _():
        s_scr[...] = s0_ref[...]
        ext_scr[base:SUBLANE, :] = conv0_ref[...]

    ext_scr[SUBLANE:SUBLANE + C, :] = xbc_ref[...]
    cw = cw_ref[...]
    conv = cb_ref[...]
    for j in range(SSD_CONV):
        conv = conv + ext_scr[pl.ds(base + j, C), :] * cw[j:j + 1, :]
    hist = ext_scr[pl.ds(base + L, kh), :]
    ext_scr[base:SUBLANE, :] = hist
    xc = _silu(conv)
    bw = H * P
    sx = xc[:, 0:bw]
    bm = xc[:, bw:bw + G * N]
    cm = xc[:, bw + G * N:]
    dt = _softplus(small_ref[...] + dtb_ref[...])
    logg = dt * (-jnp.exp(alog_ref[...]))
    if L < C:
        live = _live_rows(C, L)
        logg = jnp.where(live, logg, 0.0)
        bm = jnp.where(live, bm, 0.0)
    causal, tri_lo = _tri(C)
    _, tri_up = _tri(C, upper=True)
    b = _cumsum_rows(tri_lo, logg)
    b_t = _cumsum_cols(logg.T, tri_up)
    hpg = H // G
    ys = []
    for g in range(G):
        cg = cm[:, g * N:(g + 1) * N]
        bg = bm[:, g * N:(g + 1) * N]
        gmat = _dot_nt(_bf(cg), _bf(bg))
        for hh in range(hpg):
            h = g * hpg + hh
            ln = dt_lane + h
            bc = b[:, ln:ln + 1]
            br = b_t[ln:ln + 1, :]
            dec = jnp.exp(jnp.where(causal, bc - br, -jnp.inf))
            xdt = _bf(sx[:, h * P:(h + 1) * P] * dt[:, ln:ln + 1])
            st = s_scr[h]
            y = _dot(_bf(gmat * dec), xdt) + _dot(_bf(cg * jnp.exp(bc)), _bf(st))
            bl = bc[C - 1:C, :]
            s_scr[h] = jnp.exp(bl) * st + _dot_tn(_bf(bg * jnp.exp(bl - bc)), xdt)
            ys.append(y)
    y_all = jnp.concatenate(ys, axis=1) + dvec_ref[...] * sx
    o_ref[...] = _rms(y_all * _silu(zg_ref[...]), gn_ref[...]).astype(o_ref.dtype)

    @pl.when(c == pl.num_programs(1) - 1)
    def _():
        sout_ref[...] = s_scr[...]
        convout_ref[...] = hist


def _ssd_call(z, xbc_idx, zg_idx, small_idx, cw, cb, dtb, alog, dvec, gn, s0, conv0, l, nb, seq, C, L, dt_lane,
              out_dtype):
    H, N, P = s0.shape[1:]
    G = SSD_GROUPS
    ch = conv0.shape[2]
    bw = H * P
    nch = seq // C
    kern = functools.partial(_ssd_kernel, C=C, L=L, H=H, G=G, N=N, P=P, dt_lane=dt_lane)
    vec = lambda w: pl.BlockSpec((None, 1, w), lambda b, c: (l, 0, 0))
    return pl.pallas_call(
        kern, grid=(nb, nch),
        in_specs=[pl.BlockSpec((C, ch), lambda b, c: (b * nch + c, xbc_idx)),
                  pl.BlockSpec((C, bw), lambda b, c: (b * nch + c, zg_idx)),
                  pl.BlockSpec((C, LANE), lambda b, c: (b * nch + c, small_idx)),
                  pl.BlockSpec((None, SSD_CONV, ch), lambda b, c: (l, 0, 0)),
                  vec(ch), vec(LANE), vec(LANE), vec(bw), vec(bw),
                  pl.BlockSpec((None, H, N, P), lambda b, c: (b, 0, 0, 0)),
                  pl.BlockSpec((None, SSD_CONV - 1, ch), lambda b, c: (b, 0, 0))],
        out_specs=[pl.BlockSpec((C, bw), lambda b, c: (b * nch + c, 0)),
                   pl.BlockSpec((None, H, N, P), lambda b, c: (b, 0, 0, 0)),
                   pl.BlockSpec((None, SSD_CONV - 1, ch), lambda b, c: (b, 0, 0))],
        out_shape=[jax.ShapeDtypeStruct((nb * seq, bw), out_dtype),
                   jax.ShapeDtypeStruct((nb, H, N, P), F32),
                   jax.ShapeDtypeStruct((nb, SSD_CONV - 1, ch), F32)],
        scratch_shapes=[pltpu.VMEM((H, N, P), F32), pltpu.VMEM((SUBLANE + C, ch), F32)],
        compiler_params=_cparams(("arbitrary", "arbitrary")), name="ssd",
    )(z, z, z, cw, cb, dtb, alog, dvec, gn, s0, conv0)


def _mlstm_kernel(qk_ref, vo_ref, small_ref, bif_ref, gn_ref, c0_ref, nm0_ref, o_ref, cout_ref, nmout_ref,
                  c_scr, nm_scr, *, C, L, H, DH, i_lane, f_lane):
    c = pl.program_id(1)

    @pl.when(c == 0)
    def _():
        c_scr[...] = c0_ref[...]
        nm_scr[...] = nm0_ref[...]

    qk = qk_ref[...]
    vo = vo_ref[...]
    q = qk[:, 0:H * DH]
    k = qk[:, H * DH:] * (DH ** -0.5)
    v = vo[:, 0:H * DH]
    og = vo[:, H * DH:]
    pre = small_ref[...] + bif_ref[...]
    logf = _log_sigmoid(pre)
    ipre = pre
    if L < C:
        live = _live_rows(C, L)
        logf = jnp.where(live, logf, 0.0)
        ipre = jnp.where(live, pre, NEG)
    causal, tri_lo = _tri(C)
    _, tri_up = _tri(C, upper=True)
    b = _cumsum_rows(tri_lo, logf)
    b_t = _cumsum_cols(logf.T, tri_up)
    i_t = ipre.T
    nm = nm_scr[...]
    lane = lax.broadcasted_iota(jnp.int32, (1, LANE), 1)
    m_row = nm[H:H + 1, :]
    gn = gn_ref[...]
    outs = []
    for h in range(H):
        sl = slice(h * DH, (h + 1) * DH)
        bc = b[:, f_lane + h:f_lane + h + 1]
        br = b_t[f_lane + h:f_lane + h + 1, :]
        ic = ipre[:, i_lane + h:i_lane + h + 1]
        ir = i_t[i_lane + h:i_lane + h + 1, :]
        m_prev = nm[H:H + 1, h:h + 1]
        logd = jnp.where(causal, bc - br + ir, -jnp.inf)
        gg = bc + m_prev
        m = jnp.maximum(gg, jnp.max(logd, axis=1, keepdims=True))
        dm = jnp.exp(logd - m)
        wi = jnp.exp(gg - m)
        qf = q[:, sl]
        qh = _bf(qf)
        kh = k[:, sl]
        vh = _bf(v[:, sl])
        qkm = _dot_nt(qh, _bf(kh)) * dm
        cst = c_scr[h]
        nrow = nm[h:h + 1, :]
        num = _dot(_bf(qkm), vh) + wi * _dot(qh, _bf(cst))
        den = jnp.sum(qkm, axis=1, keepdims=True) + wi * jnp.sum(qf * nrow, axis=1, keepdims=True)
        hid = num / jnp.maximum(jnp.abs(den), jnp.exp(-m))
        m_new = m[C - 1:C, :]
        bl = bc[C - 1:C, :]
        ws = jnp.exp(bl - bc + ic - m_new)
        wc = jnp.exp(bl + m_prev - m_new)
        wk = ws * kh
        c_scr[h] = wc * cst + _dot_tn(_bf(wk), vh)
        nm_scr[h:h + 1, :] = wc * nrow + jnp.sum(wk, axis=0, keepdims=True)
        m_row = jnp.where(lane == h, m_new, m_row)
        outs.append(_rms(hid, gn[:, sl]))
    nm_scr[H:H + 1, :] = m_row
    o_ref[...] = (jnp.concatenate(outs, axis=1) * _sigmoid(og)).astype(o_ref.dtype)

    @pl.when(c == pl.num_programs(1) - 1)
    def _():
        cout_ref[...] = c_scr[...]
        nmout_ref[...] = nm_scr[...]


def _mlstm_call(z, qk_idx, vo_idx, small_idx, bif, gn, c0, nm0, l, nb, seq, C, L, i_lane, f_lane, out_dtype):
    H, DH = c0.shape[1:3]
    nch = seq // C
    kern = functools.partial(_mlstm_kernel, C=C, L=L, H=H, DH=DH, i_lane=i_lane, f_lane=f_lane)
    return pl.pallas_call(
        kern, grid=(nb, nch),
        in_specs=[pl.BlockSpec((C, 2 * H * DH), lambda b, c: (b * nch + c, qk_idx)),
                  pl.BlockSpec((C, 2 * H * DH), lambda b, c: (b * nch + c, vo_idx)),
                  pl.BlockSpec((C, LANE), lambda b, c: (b * nch + c, small_idx)),
                  pl.BlockSpec((None, 1, LANE), lambda b, c: (l, 0, 0)),
                  pl.BlockSpec((None, 1, H * DH), lambda b, c: (l, 0, 0)),
                  pl.BlockSpec((None, H, DH, DH), lambda b, c: (b, 0, 0, 0)),
                  pl.BlockSpec((None, SUBLANE, LANE), lambda b, c: (b, 0, 0))],
        out_specs=[pl.BlockSpec((C, H * DH), lambda b, c: (b * nch + c, 0)),
                   pl.BlockSpec((None, H, DH, DH), lambda b, c: (b, 0, 0, 0)),
                   pl.BlockSpec((None, SUBLANE, LANE), lambda b, c: (b, 0, 0))],
        out_shape=[jax.ShapeDtypeStruct((nb * seq, H * DH), out_dtype),
                   jax.ShapeDtypeStruct((nb, H, DH, DH), F32),
                   jax.ShapeDtypeStruct((nb, SUBLANE, LANE), F32)],
        scratch_shapes=[pltpu.VMEM((H, DH, DH), F32), pltpu.VMEM((SUBLANE, LANE), F32)],
        compiler_params=_cparams(("arbitrary", "arbitrary")), name="mlstm",
    )(z, z, z, bif, gn, c0, nm0)


def _rank_select(score, n_blocks, keep):
    lane = lax.broadcasted_iota(jnp.int32, score.shape, 1)
    cnt = jnp.zeros(score.shape, jnp.int32)
    for j in range(n_blocks):
        cj = score[:, j:j + 1]
        ahead = jnp.where(cj > score, 1, jnp.where(cj == score, jnp.where(lane > j, 1, 0), 0))
        cnt = cnt + ahead
    return cnt < keep


def _nsa_prompt_kernel(q_ref, kv_ref, small_ref, o_ref, kvb_ref, cmp_ref, *, T, TQ, DH, g_lane):
    qi = pl.program_id(1)
    G, R = NSA_KV, NSA_R
    gw = G * DH
    ncb = T // CMP_BLOCK
    nsb = T // SEL_BLOCK
    ratio = SEL_BLOCK // CMP_BLOCK
    slopes = _alibi_slopes()

    @pl.when(qi == 0)
    def _():
        kvb_ref[...] = _bf(kv_ref[...])
        hi, lo = _split2(kv_ref[:, 0:2 * gw])
        jj = lax.broadcasted_iota(jnp.int32, (ncb, T), 0) * CMP_BLOCK
        tt = lax.broadcasted_iota(jnp.int32, (ncb, T), 1)
        pm = jnp.where((tt >= jj) & (tt < jj + CMP_BLOCK), 1.0 / CMP_BLOCK, 0.0).astype(BF16)
        cmp_ref[...] = _bf(_dot(pm, hi) + _dot(pm, lo))

    q = q_ref[...]
    gate = _sigmoid(small_ref[...])
    pos = qi * TQ + lax.broadcasted_iota(jnp.int32, (TQ, 1), 0)
    posf = pos.astype(F32)
    cb = lax.broadcasted_iota(jnp.int32, (1, ncb), 1)
    cvalid = (cb * CMP_BLOCK + (CMP_BLOCK - 1)) <= pos
    cdist = posf - (cb.astype(F32) * CMP_BLOCK + (CMP_BLOCK - 1) / 2.0)
    sb = lax.broadcasted_iota(jnp.int32, (1, nsb), 1)
    svalid = sb * SEL_BLOCK <= pos
    forced = (sb == 0) | ((sb * SEL_BLOCK <= pos) & (pos < sb * SEL_BLOCK + SEL_BLOCK))
    pi = lax.broadcasted_iota(jnp.int32, (ncb, nsb), 0)
    pj = lax.broadcasted_iota(jnp.int32, (ncb, nsb), 1) * ratio
    pool = jnp.where((pi >= pj) & (pi < pj + ratio), 1.0, 0.0).astype(BF16)
    pos2 = jnp.concatenate([pos] * R, axis=0)
    row2 = lax.broadcasted_iota(jnp.int32, (R * TQ, 1), 0)
    sbcol = lax.broadcasted_iota(jnp.int32, (nsb, 1), 0) * SEL_BLOCK

    outs = [None] * (G * R)
    for g in range(G):
        kc = cmp_ref[:, g * DH:(g + 1) * DH]
        vc = cmp_ref[:, gw + g * DH:gw + (g + 1) * DH]
        qs = []
        o_cmp = []
        imp = jnp.zeros((TQ, nsb), F32)
        for r in range(R):
            hd = g * R + r
            qh = _bf(q[:, hd * DH:(hd + 1) * DH] * (DH ** -0.5))
            qs.append(qh)
            s = jnp.where(cvalid, _dot_nt(qh, kc) - slopes[hd] * cdist, NEG)
            p = jnp.where(cvalid, _softmax_rows(s), 0.0)
            o_cmp.append(_dot(_bf(p), vc))
            ph, plo = _split2(p)
            imp = imp + _dot(ph, pool) + _dot(plo, pool)
        score = jnp.where(svalid, jnp.where(forced, FORCE_SCORE, imp), -1.0)
        sel = _rank_select(score, nsb, SEL_TOPK) & (score >= 0.0)
        sel2 = jnp.concatenate([jnp.where(sel, 1.0, 0.0).astype(BF16)] * R, axis=0)
        q2 = jnp.concatenate(qs, axis=0)
        slope2 = slopes[g * R]
        for r in range(1, R):
            slope2 = jnp.where(row2 >= r * TQ, slopes[g * R + r], slope2)

        def attend(kcol, vcol, lo_tile, use_sel):
            def body(kt, carry):
                m, l, acc = carry
                k0 = pl.multiple_of(kt * TQ, TQ)
                ks = kvb_ref[pl.ds(k0, TQ), kcol:kcol + DH]
                vs = kvb_ref[pl.ds(k0, TQ), vcol:vcol + DH]
                kpos = k0 + lax.broadcasted_iota(jnp.int32, (1, TQ), 1)
                dist = pos2 - kpos
                if use_sel:
                    expand = jnp.where((kpos >= sbcol) & (kpos < sbcol + SEL_BLOCK), 1.0, 0.0).astype(BF16)
                    msk = (_dot(sel2, expand) > 0.5) & (dist >= 0)
                else:
                    msk = (dist >= 0) & (dist < WINDOW)
                s = jnp.where(msk, _dot_nt(q2, ks) - slope2 * dist.astype(F32), NEG)
                mn = jnp.maximum(m, jnp.max(s, axis=1, keepdims=True))
                p = jnp.where(msk, jnp.exp(s - mn), 0.0)
                alpha = jnp.exp(m - mn)
                return (mn, alpha * l + jnp.sum(p, axis=1, keepdims=True), alpha * acc + _dot(_bf(p), vs))

            init = (jnp.full((R * TQ, 1), NEG, F32), jnp.zeros((R * TQ, 1), F32), jnp.zeros((R * TQ, DH), F32))
            _, l, acc = lax.fori_loop(lo_tile, qi + 1, body, init)
            return acc / l

        o_slc = attend(2 * gw + g * DH, 3 * gw + g * DH, 0, True)
        o_win = attend(4 * gw + g * DH, 5 * gw + g * DH, jnp.maximum(qi - WINDOW // TQ, 0), False)
        for r in range(R):
            hd = g * R + r
            gc = gate[:, g_lane + hd:g_lane + hd + 1]
            gs = gate[:, g_lane + NSA_HEADS + hd:g_lane + NSA_HEADS + hd + 1]
            gwn = gate[:, g_lane + 2 * NSA_HEADS + hd:g_lane + 2 * NSA_HEADS + hd + 1]
            outs[hd] = gc * o_cmp[r] + gs * o_slc[r * TQ:(r + 1) * TQ] + gwn * o_win[r * TQ:(r + 1) * TQ]
    o_ref[...] = jnp.concatenate(outs, axis=1).astype(o_ref.dtype)


def _nsa_prompt_call(z, q_idx, kv_idx, small_idx, nb, seq, g_lane, tq=256):
    dh = LANE
    kvw = 6 * NSA_KV * dh
    qw = NSA_HEADS * dh
    nq = seq // tq
    assert WINDOW % tq == 0 and seq % tq == 0
    kern = functools.partial(_nsa_prompt_kernel, T=seq, TQ=tq, DH=dh, g_lane=g_lane)
    return pl.pallas_call(
        kern, grid=(nb, nq),
        in_specs=[pl.BlockSpec((tq, qw), lambda b, i: (b * nq + i, q_idx)),
                  pl.BlockSpec((seq, kvw), lambda b, i: (b, kv_idx)),
                  pl.BlockSpec((tq, LANE), lambda b, i: (b * nq + i, small_idx))],
        out_specs=pl.BlockSpec((tq, qw), lambda b, i: (b * nq + i, 0)),
        out_shape=jax.ShapeDtypeStruct((nb * seq, qw), BF16),
        scratch_shapes=[pltpu.VMEM((seq, kvw), BF16), pltpu.VMEM((seq // CMP_BLOCK, 2 * NSA_KV * dh), BF16)],
        compiler_params=_cparams(("arbitrary", "arbitrary")), name="nsa_prompt",
    )(z, z, z)


def _nsa_means_kernel(pt_ref, *refs, PP, DH):
    ins = refs[:PP]
    o_ref = refs[PP]
    nper = PAGE_SIZE // CMP_BLOCK
    cols = []
    for cg in range(2 * NSA_KV):
        parts = [jnp.sum(ins[i][:, cg // NSA_KV, cg % NSA_KV, :].reshape(nper, CMP_BLOCK, DH), axis=1)
                 * (1.0 / CMP_BLOCK) for i in range(PP)]
        cols.append(jnp.concatenate(parts, axis=0))
    o_ref[...] = jnp.concatenate(cols, axis=1)


def _nsa_means_call(cache, page_table, l, pp=8):
    nb, n_pages = page_table.shape
    dh = cache.shape[5]
    nper = PAGE_SIZE // CMP_BLOCK
    assert cache.shape[2] == PAGE_SIZE and cache.shape[4] == NSA_KV and n_pages % pp == 0
    in_specs = [pl.BlockSpec((None, None, PAGE_SIZE, 2, NSA_KV, dh),
                             lambda b, s, pt, i=i: (l, pt[b, s * pp + i], 0, 0, 0, 0)) for i in range(pp)]
    gs = pltpu.PrefetchScalarGridSpec(
        num_scalar_prefetch=1, grid=(nb, n_pages // pp), in_specs=in_specs,
        out_specs=pl.BlockSpec((None, nper * pp, 2 * NSA_KV * dh), lambda b, s, pt: (b, s, 0)))
    return pl.pallas_call(
        functools.partial(_nsa_means_kernel, PP=pp, DH=dh), grid_spec=gs,
        out_shape=jax.ShapeDtypeStruct((nb, n_pages * nper, 2 * NSA_KV * dh), F32),
        compiler_params=_cparams(("arbitrary", "arbitrary")), name="nsa_cmp_means",
    )(page_table, *([cache] * pp))


def _nsa_score_kernel(q_ref, kvc_ref, ocmp_ref, ids_ref, *, DH, qpos):
    G, R = NSA_KV, NSA_R
    gw = G * DH
    ncb = kvc_ref.shape[0]
    ratio = SEL_BLOCK // CMP_BLOCK
    nsb = ncb // ratio
    slopes = _alibi_slopes()
    q = q_ref[...]
    kvc = _bf(kvc_ref[...])
    rows = q.shape[0]
    cb = lax.broadcasted_iota(jnp.int32, (1, ncb), 1)
    cvalid = (cb * CMP_BLOCK + (CMP_BLOCK - 1)) <= qpos
    cdist = float(qpos) - (cb.astype(F32) * CMP_BLOCK + (CMP_BLOCK - 1) / 2.0)
    pi = lax.broadcasted_iota(jnp.int32, (ncb, nsb), 0)
    pj = lax.broadcasted_iota(jnp.int32, (ncb, nsb), 1) * ratio
    pool = jnp.where((pi >= pj) & (pi < pj + ratio), 1.0, 0.0).astype(BF16)
    lane = lax.broadcasted_iota(jnp.int32, (rows, nsb), 1).astype(F32)
    lane_out = lax.broadcasted_iota(jnp.int32, (rows, LANE), 1)
    o_cmp = []
    id_rows = []
    for g in range(G):
        kc = kvc[:, g * DH:(g + 1) * DH]
        vc = kvc[:, gw + g * DH:gw + (g + 1) * DH]
        imp = jnp.zeros((rows, nsb), F32)
        for r in range(R):
            hd = g * R + r
            qh = _bf(q[:, hd * DH:(hd + 1) * DH] * (DH ** -0.5))
            s = jnp.where(cvalid, _dot_nt(qh, kc) - slopes[hd] * cdist, NEG)
            p = jnp.where(cvalid, _softmax_rows(s), 0.0)
            o_cmp.append(_dot(_bf(p), vc))
            ph, plo = _split2(p)
            imp = imp + _dot(ph, pool) + _dot(plo, pool)
        val = jnp.where(lane == 0.0, -1.0, imp)
        ids = jnp.zeros((rows, LANE), F32)
        for it in range(SEL_TOPK - 2):
            mx = jnp.max(val, axis=1, keepdims=True)
            ix = jnp.min(jnp.where(val == mx, lane, float(nsb)), axis=1, keepdims=True)
            ids = jnp.where(lane_out == it + 1, ix, ids)
            val = jnp.where(lane == ix, -2.0, val)
        id_rows.append(ids[0:1, :])
    ocmp_ref[...] = jnp.concatenate(o_cmp, axis=1)
    pad = jnp.zeros((ids_ref.shape[0] - G, LANE), F32)
    ids_ref[...] = jnp.concatenate(id_rows + [pad], axis=0).astype(jnp.int32)


def _nsa_score_call(z_s, q_idx, kvc, qpos):
    nb, ncb, w = kvc.shape
    dh = w // (2 * NSA_KV)
    qw = NSA_HEADS * dh
    return pl.pallas_call(
        functools.partial(_nsa_score_kernel, DH=dh, qpos=qpos), grid=(nb,),
        in_specs=[pl.BlockSpec((SROWS, qw), lambda b: (b, q_idx)),
                  pl.BlockSpec((None, ncb, w), lambda b: (b, 0, 0))],
        out_specs=[pl.BlockSpec((SROWS, qw), lambda b: (b, 0)),
                   pl.BlockSpec((None, SUBLANE, LANE), lambda b: (b, 0, 0))],
        out_shape=[jax.ShapeDtypeStruct((nb * SROWS, qw), F32),
                   jax.ShapeDtypeStruct((nb, SUBLANE, LANE), jnp.int32)],
        compiler_params=_cparams(("arbitrary",)), name="nsa_cmp_score",
    )(z_s, kvc)


def _nsa_gather_kernel(sel_ref, q_ref, new_ref, small_ref, ocmp_ref, win_ref, *rest,
                       DH, NSEL, qpos, g_lane):
    kv = rest[:NSA_KV * NSEL]
    o_ref = rest[NSA_KV * NSEL]
    b = pl.program_id(0)
    G, R = NSA_KV, NSA_R
    gw = G * DH
    slopes = _alibi_slopes()
    q = q_ref[...]
    new = new_ref[...]
    gate = _sigmoid(small_ref[...])
    ocmp = ocmp_ref[...]
    wlen = win_ref.shape[0]
    jw = lax.broadcasted_iota(jnp.int32, (1, wlen), 1)
    distw = wlen - jw
    validw = (distw < WINDOW) & (qpos - distw >= 0)
    off = lax.broadcasted_iota(jnp.int32, (1, SEL_BLOCK), 1)
    outs = []
    for g in range(G):
        ksn = _bf(new[:, 2 * gw + g * DH:2 * gw + (g + 1) * DH]).astype(F32)
        vsn = _bf(new[:, 3 * gw + g * DH:3 * gw + (g + 1) * DH]).astype(F32)
        kwn = _bf(new[:, 4 * gw + g * DH:4 * gw + (g + 1) * DH]).astype(F32)
        vwn = _bf(new[:, 5 * gw + g * DH:5 * gw + (g + 1) * DH]).astype(F32)
        kw = _bf(win_ref[:, g * DH:(g + 1) * DH])
        vw = _bf(win_ref[:, gw + g * DH:gw + (g + 1) * DH])
        kblk = [_bf(kv[g * NSEL + i][:, 0, g, :]) for i in range(NSEL)]
        vblk = [_bf(kv[g * NSEL + i][:, 1, g, :]) for i in range(NSEL)]
        nslot = NSA_KV * SEL_TOPK
        dists = [qpos - (sel_ref[b, 2 * nslot + g * SEL_TOPK + i] * SEL_BLOCK + off) for i in range(NSEL)]
        for r in range(R):
            hd = g * R + r
            slope = slopes[hd]
            qh = _bf(q[:, hd * DH:(hd + 1) * DH] * (DH ** -0.5))
            qf = qh.astype(F32)
            s_list = [jnp.where(dists[i] >= 0, _dot_nt(qh, kblk[i]) - slope * dists[i].astype(F32), NEG)
                      for i in range(NSEL)]
            s_new = jnp.sum(qf * ksn, axis=1, keepdims=True)
            mx = s_new
            for s in s_list:
                mx = jnp.maximum(mx, jnp.max(s, axis=1, keepdims=True))
            p_new = jnp.exp(s_new - mx)
            den = p_new
            acc = p_new * vsn
            for i, s in enumerate(s_list):
                p = jnp.exp(s - mx)
                den = den + jnp.sum(p, axis=1, keepdims=True)
                acc = acc + _dot(_bf(p), vblk[i])
            o_slc = acc / den
            s = jnp.where(validw, _dot_nt(qh, kw) - slope * distw.astype(F32), NEG)
            s_new = jnp.sum(qf * kwn, axis=1, keepdims=True)
            mx = jnp.maximum(s_new, jnp.max(s, axis=1, keepdims=True))
            p = jnp.exp(s - mx)
            p_new = jnp.exp(s_new - mx)
            o_win = (_dot(_bf(p), vw) + p_new * vwn) / (jnp.sum(p, axis=1, keepdims=True) + p_new)
            gc, gs, gwn = [gate[:, g_lane + c * NSA_HEADS + hd:g_lane + c * NSA_HEADS + hd + 1] for c in range(3)]
            outs.append(gc * ocmp[:, hd * DH:(hd + 1) * DH] + gs * o_slc + gwn * o_win)
    o_ref[...] = jnp.concatenate(outs, axis=1)


def _nsa_gather_call(z_s, q_idx, kv_idx, small_idx, ocmp, win, cache, page_table, ids, l, qpos, g_lane):
    nb = page_table.shape[0]
    dh = cache.shape[5]
    G = NSA_KV
    qw = NSA_HEADS * dh
    nsel = SEL_TOPK - 1
    bpp = PAGE_SIZE // SEL_BLOCK
    wlen, ww = win.shape[2:]
    nslot = G * SEL_TOPK
    sel = jnp.concatenate([jnp.take_along_axis(page_table, ids // bpp, axis=1), ids % bpp, ids], axis=1)
    in_specs = [pl.BlockSpec((SROWS, qw), lambda b, s: (b, q_idx)),
                pl.BlockSpec((SROWS, 6 * G * dh), lambda b, s: (b, kv_idx)),
                pl.BlockSpec((SROWS, LANE), lambda b, s: (b, small_idx)),
                pl.BlockSpec((SROWS, qw), lambda b, s: (b, 0)),
                pl.BlockSpec((None, None, wlen, ww), lambda b, s: (l, b, 0, 0))]
    for g in range(G):
        for i in range(nsel):
            in_specs.append(pl.BlockSpec(
                (None, None, SEL_BLOCK, 2, G, dh),
                lambda b, s, j=g * SEL_TOPK + i: (l, s[b, j], s[b, nslot + j], 1, 0, 0)))
    gs = pltpu.PrefetchScalarGridSpec(
        num_scalar_prefetch=1, grid=(nb,), in_specs=in_specs,
        out_specs=pl.BlockSpec((SROWS, qw), lambda b, s: (b, 0)))
    kern = functools.partial(_nsa_gather_kernel, DH=dh, NSEL=nsel, qpos=qpos, g_lane=g_lane)
    return pl.pallas_call(
        kern, grid_spec=gs,
        out_shape=jax.ShapeDtypeStruct((nb * SROWS, qw), F32),
        compiler_params=_cparams(("arbitrary",)), name="nsa_sel_win",
    )(sel, z_s, z_s, z_s, ocmp, win, *([cache] * (G * nsel)))


def _lane_vec(depth, lane, values):
    n = values.shape[1]
    return jnp.zeros((depth, 1, LANE), F32).at[:, 0, lane:lane + n].set(values.astype(F32))


def kernel(x_prompt, x_sample, cache_nsa_kv, cache_nsa_win, state_gla, state_ssd, state_ssd_conv, state_mlstm_c, state_mlstm_n, state_mlstm_m, page_table, c_prompt, c_sample, ada_w, ada_b, norm_mix_g, norm_ffn_g, w_in, gla_w_a, gla_b_a, gla_norm_g, ssd_conv_w, ssd_conv_b, ssd_dt_bias, ssd_a_log, ssd_d, ssd_norm_g, ml_b_i, ml_b_f, ml_norm_g, w_branch, w_out, ffn_w_gate, ffn_w_up, ffn_w_down, final_norm_g):
    nbp, seq, d = x_prompt.shape
    nbs, dec_seq, _ = x_sample.shape
    assert dec_seq == 1 and seq % CHUNK == 0
    depth = w_in.shape[0]
    past_len = page_table.shape[1] * PAGE_SIZE
    bw = d // 4
    dh = bw // NSA_HEADS
    gw = NSA_KV * dh
    src, dst, small_off, n_packed, _ = _packed_offsets(d)
    small_idx = small_off // LANE
    kv_off = dst['nsa_kv']
    q_idx = dst['nsa_q'] // bw
    kv_idx = kv_off // (6 * gw)
    xbc_idx = dst['ssd_xbc'] // src['ssd_xbc'][1]
    zg_idx = dst['ssd_z'] // bw
    mlqk_idx = dst['ml_q'] // (2 * bw)
    mlvo_idx = dst['ml_v'] // (2 * bw)
    for name, width in (('gla_q', 3 * bw), ('nsa_q', bw), ('nsa_kv', 6 * gw), ('ssd_xbc', src['ssd_xbc'][1]),
                        ('ssd_z', bw), ('ml_q', 2 * bw), ('ml_v', 2 * bw), ('merge', 512)):
        assert dst[name] % width == 0
    assert dst['gla_q'] == 0
    g_lane, dt_lane, i_lane, f_lane = dst['nsa_g'], dst['ssd_dt'], dst['ml_i'], dst['ml_f']

    crow = 2 * SUBLANE
    c_all = jnp.concatenate([c_prompt, c_sample, jnp.zeros((crow - nbp - nbs, d), F32)], axis=0)
    mod = _ada(c_all, ada_w, ada_b)
    modp = mod[:, :nbp].reshape(depth, nbp, 1, 6 * d)
    mods = jnp.repeat(mod[:, nbp:nbp + nbs], SROWS, axis=1)

    w_in_p = _pack_w_in(w_in)
    r3 = lambda a: a.reshape(depth, 1, a.shape[-1])
    wa_pad = jnp.zeros((depth, LANE, gla_w_a.shape[2]), F32).at[:, dst['gla_a']:dst['gla_a'] + GLA_RANK].set(gla_w_a)
    dtb = _lane_vec(depth, dt_lane, ssd_dt_bias)
    alog = _lane_vec(depth, dt_lane, ssd_a_log)
    dvec = jnp.repeat(ssd_d, bw // SSD_HEADS, axis=1).reshape(depth, 1, bw)
    bif = _lane_vec(depth, i_lane, ml_b_i) + _lane_vec(depth, f_lane, ml_b_f)
    win4 = cache_nsa_win.reshape(depth, nbs, cache_nsa_win.shape[2], 2 * gw)

    xp = x_prompt.reshape(nbp * seq, d)
    xs = jnp.zeros((nbs, SROWS, d), F32).at[:, 0].set(x_sample[:, 0]).reshape(nbs * SROWS, d)

    zeros_like_p = lambda a: jnp.zeros((nbp,) + a.shape[2:], F32)
    gla0, ssd0, conv0, mlc0 = (zeros_like_p(a) for a in (state_gla, state_ssd, state_ssd_conv, state_mlstm_c))
    nm0 = jnp.zeros((nbp, SUBLANE, LANE), F32)
    nm_s = (jnp.zeros((depth, nbs, SUBLANE, LANE), F32).at[:, :, :ML_HEADS, :].set(state_mlstm_n)
            .at[:, :, ML_HEADS, :ML_HEADS].set(state_mlstm_m))

    out = {k: [] for k in ('rows_p', 'rows_s', 'win_p', 'win_s', 'gla_p', 'gla_s', 'ssd_p', 'ssd_s', 'conv_p',
                           'conv_s', 'mlc_p', 'mlc_s', 'mln_p', 'mln_s', 'mlm_p', 'mlm_s')}
    for l in range(depth):
        hp = _norm_mod_prompt(xp, r3(norm_mix_g), modp, l, 0, 1, seq)
        hs = _norm_mod_sample(xs, r3(norm_mix_g), mods, l, 0, 1)
        zp, zs = _mm_plain(hp, hs, w_in_p, l, 1024, n_packed // 11)

        o_gla_p, gla_p = _gla_call(zp, small_idx, wa_pad, r3(gla_b_a), r3(gla_norm_g), gla0, l, nbp, seq, CHUNK,
                                   CHUNK, BF16)
        o_gla_s, gla_s = _gla_call(zs, small_idx, wa_pad, r3(gla_b_a), r3(gla_norm_g), state_gla[l], l, nbs, SROWS,
                                   SROWS, 1, F32)
        ssd_args = (ssd_conv_w, r3(ssd_conv_b), dtb, alog, dvec, r3(ssd_norm_g))
        o_ssd_p, ssd_p, conv_p = _ssd_call(zp, xbc_idx, zg_idx, small_idx, *ssd_args, ssd0, conv0, l, nbp, seq,
                                           CHUNK, CHUNK, dt_lane, BF16)
        o_ssd_s, ssd_s, conv_s = _ssd_call(zs, xbc_idx, zg_idx, small_idx, *ssd_args, state_ssd[l],
                                           state_ssd_conv[l], l, nbs, SROWS, SROWS, 1, dt_lane, F32)
        o_ml_p, mlc_p, mlnm_p = _mlstm_call(zp, mlqk_idx, mlvo_idx, small_idx, bif, r3(ml_norm_g), mlc0, nm0, l,
                                            nbp, seq, CHUNK, CHUNK, i_lane, f_lane, BF16)
        o_ml_s, mlc_s, mlnm_s = _mlstm_call(zs, mlqk_idx, mlvo_idx, small_idx, bif, r3(ml_norm_g), state_mlstm_c[l],
                                            nm_s[l], l, nbs, SROWS, SROWS, 1, i_lane, f_lane, F32)
        o_nsa_p = _nsa_prompt_call(zp, q_idx, kv_idx, small_idx, nbp, seq, g_lane)
        kvc = _nsa_means_call(cache_nsa_kv, page_table, l)
        ocmp, ids = _nsa_score_call(zs, q_idx, kvc, past_len)
        ids2 = ids[:, :NSA_KV, :SEL_TOPK].reshape(nbs, NSA_KV * SEL_TOPK)
        o_nsa_s = _nsa_gather_call(zs, q_idx, kv_idx, small_idx, ocmp, win4, cache_nsa_kv, page_table, ids2, l,
                                   past_len, g_lane)

        mp, ms = _merge((o_gla_p, o_nsa_p, o_ssd_p, o_ml_p), (o_gla_s, o_nsa_s, o_ssd_s, o_ml_s), zp, zs, w_branch,
                        l, dst['merge'], 1024, 512)
        xp, xs = _mm_resid(mp, ms, w_out, xp, xs, modp, mods, l, 2, seq, 1024, 512, "proj_out")
        hp = _norm_mod_prompt(xp, r3(norm_ffn_g), modp, l, 3, 4, seq)
        hs = _norm_mod_sample(xs, r3(norm_ffn_g), mods, l, 3, 4)
        ap, a_s = _mm_swiglu(hp, hs, ffn_w_gate, ffn_w_up, l, 1024, 512)
        xp, xs = _mm_resid(ap, a_s, ffn_w_down, xp, xs, modp, mods, l, 5, seq, 512, 512, "ffn_down")

        zp3 = zp.reshape(nbp, seq, n_packed)
        zs3 = zs.reshape(nbs, SROWS, n_packed)[:, :1]
        out['rows_p'].append(zp3[:, :, kv_off:kv_off + 4 * gw].reshape(nbp, seq, 4, NSA_KV, dh))
        out['rows_s'].append(zs3[:, :, kv_off:kv_off + 4 * gw].reshape(nbs, 1, 4, NSA_KV, dh))
        keep = min(WINDOW, seq)
        out['win_p'].append(zp3[:, seq - keep:, kv_off + 4 * gw:kv_off + 6 * gw].reshape(nbp, keep, 2, NSA_KV, dh))
        win_all = jnp.concatenate([cache_nsa_win[l], zs3[:, :, kv_off + 4 * gw:kv_off + 6 * gw]
                                   .reshape(nbs, 1, 2, NSA_KV, dh)], axis=1)
        out['win_s'].append(win_all[:, win_all.shape[1] - min(WINDOW, win_all.shape[1]):])
        for key, val in (('gla_p', gla_p), ('gla_s', gla_s), ('ssd_p', ssd_p), ('ssd_s', ssd_s), ('conv_p', conv_p),
                         ('conv_s', conv_s), ('mlc_p', mlc_p), ('mlc_s', mlc_s)):
            out[key].append(val)
        for tag, nm in (('p', mlnm_p), ('s', mlnm_s)):
            out['mln_' + tag].append(nm[:, :ML_HEADS, :])
            out['mlm_' + tag].append(nm[:, ML_HEADS, :ML_HEADS])

    y_prompt = _final_norm(xp, final_norm_g).reshape(nbp, seq, d)
    y_sample = _final_norm(xs, final_norm_g).reshape(nbs, SROWS, d)[:, :1]
    st = {k: jnp.stack(v) for k, v in out.items()}
    return (y_prompt, y_sample, st['rows_p'], st['rows_s'], st['win_p'], st['win_s'], st['gla_p'], st['gla_s'],
            st['ssd_p'], st['ssd_s'], st['conv_p'], st['conv_s'], st['mlc_p'], st['mlc_s'], st['mln_p'], st['mln_s'],
            st['mlm_p'], st['mlm_s'])
```

```python
import functools
import math

import jax
import jax.numpy as jnp
from jax import lax
from jax.experimental import pallas as pl
from jax.experimental.pallas import tpu as pltpu

F32 = jnp.float32
BF16 = jnp.bfloat16

PAGE_SIZE = 128
GLA_HEADS = 4
GLA_RANK = 16
GLA_TAU = 16.0
NSA_HEADS = 4
NSA_KV = 2
NSA_R = NSA_HEADS // NSA_KV
CMP_BLOCK = 32
SEL_BLOCK = 64
SEL_TOPK = 16
WINDOW = 512
FORCE_SCORE = 1e4
NEG = -1e30
SSD_HEADS = 8
SSD_GROUPS = 2
SSD_N = 128
SSD_CONV = 4
ML_HEADS = 4
CHUNK = 128
EPS = 1e-6

LANE = 128
SUBLANE = 8
VMEM_LIMIT = 56 * 1024 * 1024

SROWS = SUBLANE


def _cparams(sem):
    return pltpu.CompilerParams(dimension_semantics=sem, vmem_limit_bytes=VMEM_LIMIT)


def _layout(d_model):
    bw = d_model // 4
    gdk = (bw // GLA_HEADS) // 2
    conv_ch = bw + 2 * SSD_GROUPS * SSD_N
    splits = (('gla_q', GLA_HEADS * gdk), ('gla_k', GLA_HEADS * gdk), ('gla_v', bw), ('gla_r', bw),
              ('gla_a', GLA_RANK), ('nsa_q', bw), ('nsa_kv', 6 * NSA_KV * (bw // NSA_HEADS)),
              ('nsa_g', 3 * NSA_HEADS), ('ssd_z', bw), ('ssd_xbc', conv_ch), ('ssd_dt', SSD_HEADS),
              ('ml_q', bw), ('ml_k', bw), ('ml_v', bw), ('ml_o', bw), ('ml_i', ML_HEADS), ('ml_f', ML_HEADS),
              ('merge', 4 * d_model))
    src = {}
    off = 0
    for name, w in splits:
        src[name] = (off, w)
        off += w
    return src, off


_Z_ORDER = ('gla_q', 'gla_k', 'gla_v', 'gla_r', 'nsa_kv', 'ml_q', 'ml_k', 'ml_v', 'ml_o', 'ssd_xbc',
            'nsa_q', 'ssd_z', 'merge')
_Z_SMALL = ('gla_a', 'nsa_g', 'ssd_dt', 'ml_i', 'ml_f')


Z_TILE = 512


def _packed_offsets(d_model):
    src, d_in = _layout(d_model)
    dst = {}
    off = 0
    for name in _Z_ORDER:
        dst[name] = off
        off += src[name][1]
    for name in _Z_SMALL:
        dst[name] = src[name][0] % LANE
    return src, dst, off, d_in


def _w_in_tile_table(d_model):
    src, dst, n_packed, d_in = _packed_offsets(d_model)
    main, ext, case, shifts = [], [], [], []
    for t in range(n_packed // Z_TILE):
        col = t * Z_TILE
        name = max((n for n in _Z_ORDER if dst[n] <= col), key=lambda n: dst[n])
        s0 = src[name][0] + (col - dst[name])
        shift = s0 % LANE
        a = s0 - shift
        assert a % Z_TILE == 0
        last = col + Z_TILE - 1
        name2 = max((n for n in _Z_ORDER if dst[n] <= last), key=lambda n: dst[n])
        assert src[name2][0] + (last - dst[name2]) == s0 + Z_TILE - 1
        if shift not in shifts:
            shifts.append(shift)
        main.append(a // Z_TILE)
        ext.append(min((a + Z_TILE) // LANE, (d_in - 1) // LANE))
        case.append(shifts.index(shift))
    small_blocks = []
    for name in _Z_SMALL:
        blk = src[name][0] // LANE
        assert (src[name][0] + src[name][1] - 1) // LANE == blk
        if blk not in small_blocks:
            small_blocks.append(blk)
    return (main, ext, case), tuple(shifts), tuple(small_blocks)


def _sigmoid(x):
    return 1.0 / (1.0 + jnp.exp(-x))


def _silu(x):
    return x * _sigmoid(x)


def _log_sigmoid(x):
    return jnp.minimum(x, 0.0) - jnp.log1p(jnp.exp(-jnp.abs(x)))


def _softplus(x):
    return jnp.maximum(x, 0.0) + jnp.log1p(jnp.exp(-jnp.abs(x)))


def _dot(a, b):
    return jnp.dot(a, b, preferred_element_type=F32)


def _dot_nt(a, b):
    return lax.dot_general(a, b, (((1,), (1,)), ((), ())), preferred_element_type=F32)


def _dot_tn(a, b):
    return lax.dot_general(a, b, (((0,), (0,)), ((), ())), preferred_element_type=F32)


def _bf(x):
    return x.astype(BF16)


def _split2(x):
    h = x.astype(BF16)
    return h, (x - h.astype(F32)).astype(BF16)


def _causal(c):
    r = lax.broadcasted_iota(jnp.int32, (c, c), 0)
    s = lax.broadcasted_iota(jnp.int32, (c, c), 1)
    return r >= s


def _cumsum_rows(x):
    n = x.shape[0]
    row = lax.broadcasted_iota(jnp.int32, (n, 1), 0)
    step = 1
    while step < n:
        x = x + jnp.where(row >= step, pltpu.roll(x, step, 0), 0.0)
        step *= 2
    return x


def _rms(x, g):
    return x * lax.rsqrt(jnp.mean(x * x, axis=-1, keepdims=True) + EPS) * g


def _softmax_rows(s):
    m = jnp.max(s, axis=-1, keepdims=True)
    e = jnp.exp(s - m)
    return e / jnp.sum(e, axis=-1, keepdims=True)


def _alibi_slopes():
    return [2.0 ** (-8.0 * (h + 1) / NSA_HEADS) for h in range(NSA_HEADS)]


def _ada_kernel(c_ref, w_ref, b_ref, o_ref):
    a = _bf(_silu(c_ref[...]))
    o_ref[...] = _dot(a, _bf(w_ref[...])) + b_ref[...]


def _ada(c_all, ada_w, ada_b):
    depth, d, n = ada_w.shape
    rows = c_all.shape[0]
    tn = 1024
    return pl.pallas_call(
        _ada_kernel, grid=(depth, n // tn),
        in_specs=[pl.BlockSpec((rows, d), lambda l, j: (0, 0)),
                  pl.BlockSpec((None, d, tn), lambda l, j: (l, 0, j)),
                  pl.BlockSpec((None, 1, tn), lambda l, j: (l, 0, j))],
        out_specs=pl.BlockSpec((None, rows, tn), lambda l, j: (l, 0, j)),
        out_shape=jax.ShapeDtypeStruct((depth, rows, n), F32),
        compiler_params=_cparams(("arbitrary", "arbitrary")), name="ada",
    )(c_all, ada_w, ada_b.reshape(depth, 1, n))


def _norm_mod_kernel(x_ref, g_ref, sc_ref, sh_ref, o_ref):
    y = _rms(x_ref[...], g_ref[...])
    o_ref[...] = (y * (1.0 + sc_ref[...]) + sh_ref[...]).astype(o_ref.dtype)


def _norm_kernel(x_ref, g_ref, o_ref):
    o_ref[...] = _rms(x_ref[...], g_ref[...]).astype(o_ref.dtype)


def _norm_mod_prompt(x, g, modp, l, j_shift, j_scale, seq):
    m, d = x.shape
    tm = 512
    per = seq // tm
    return pl.pallas_call(
        _norm_mod_kernel, grid=(m // tm,),
        in_specs=[pl.BlockSpec((tm, d), lambda i: (i, 0)),
                  pl.BlockSpec((None, 1, d), lambda i: (l, 0, 0)),
                  pl.BlockSpec((None, None, 1, d), lambda i: (l, i // per, 0, j_scale)),
                  pl.BlockSpec((None, None, 1, d), lambda i: (l, i // per, 0, j_shift))],
        out_specs=pl.BlockSpec((tm, d), lambda i: (i, 0)),
        out_shape=jax.ShapeDtypeStruct((m, d), BF16),
        compiler_params=_cparams(("arbitrary",)), name="norm_mod_prompt",
    )(x, g, modp, modp)


def _norm_mod_sample(x, g, mods, l, j_shift, j_scale):
    m, d = x.shape
    return pl.pallas_call(
        _norm_mod_kernel, grid=(1,),
        in_specs=[pl.BlockSpec((m, d), lambda i: (0, 0)),
                  pl.BlockSpec((None, 1, d), lambda i: (l, 0, 0)),
                  pl.BlockSpec((None, m, d), lambda i: (l, 0, j_scale)),
                  pl.BlockSpec((None, m, d), lambda i: (l, 0, j_shift))],
        out_specs=pl.BlockSpec((m, d), lambda i: (0, 0)),
        out_shape=jax.ShapeDtypeStruct((m, d), BF16),
        compiler_params=_cparams(("arbitrary",)), name="norm_mod_sample",
    )(x, g, mods, mods)


def _final_norm(x, g):
    m, d = x.shape
    tm = min(m, 512)
    return pl.pallas_call(
        _norm_kernel, grid=(m // tm,),
        in_specs=[pl.BlockSpec((tm, d), lambda i: (i, 0)), pl.BlockSpec((1, d), lambda i: (0, 0))],
        out_specs=pl.BlockSpec((tm, d), lambda i: (i, 0)),
        out_shape=jax.ShapeDtypeStruct((m, d), F32),
        compiler_params=_cparams(("arbitrary",)), name="final_norm",
    )(x, g.reshape(1, d))


def _proj_in_kernel(tab_ref, a_ref, as_ref, wm_ref, we_ref, o_ref, os_ref, wb_ref, *, shifts):
    j = pl.program_id(0)

    @pl.when(pl.program_id(1) == 0)
    def _():
        for case, s in enumerate(shifts):
            @pl.when(tab_ref[2, j] == case)
            def _():
                if s == 0:
                    wb_ref[...] = _bf(wm_ref[...])
                else:
                    wb_ref[...] = _bf(jnp.concatenate([wm_ref[:, s:], we_ref[:, :s]], axis=1))
        os_ref[...] = _dot(as_ref[...], wb_ref[...])

    o_ref[...] = _dot(a_ref[...], wb_ref[...])


def _proj_in(a, a_s, w_in, l, tm):
    m, k = a.shape
    ms = a_s.shape[0]
    table, shifts, _ = _w_in_tile_table(k)
    ntile = len(table[0])
    gs = pltpu.PrefetchScalarGridSpec(
        num_scalar_prefetch=1, grid=(ntile, m // tm),
        in_specs=[pl.BlockSpec((tm, k), lambda j, i, t: (i, 0)),
                  pl.BlockSpec((ms, k), lambda j, i, t: (0, 0)),
                  pl.BlockSpec((None, k, Z_TILE), lambda j, i, t: (l, 0, t[0, j])),
                  pl.BlockSpec((None, k, LANE), lambda j, i, t: (l, 0, t[1, j]))],
        out_specs=[pl.BlockSpec((tm, Z_TILE), lambda j, i, t: (i, j)),
                   pl.BlockSpec((ms, Z_TILE), lambda j, i, t: (0, j))],
        scratch_shapes=[pltpu.VMEM((k, Z_TILE), BF16)])
    return pl.pallas_call(
        functools.partial(_proj_in_kernel, shifts=shifts), grid_spec=gs,
        out_shape=[jax.ShapeDtypeStruct((m, ntile * Z_TILE), F32), jax.ShapeDtypeStruct((ms, ntile * Z_TILE), F32)],
        compiler_params=_cparams(("arbitrary", "arbitrary")), name="proj_in",
    )(jnp.asarray(table, jnp.int32), a, a_s, w_in, w_in)


def _proj_small_kernel(a_ref, as_ref, *refs, lanes):
    w_refs = refs[:len(lanes)]
    o_ref, os_ref, wb_ref = refs[len(lanes):]

    @pl.when(pl.program_id(0) == 0)
    def _():
        lane = lax.broadcasted_iota(jnp.int32, (1, LANE), 1)
        w = jnp.zeros(wb_ref.shape, F32)
        for w_ref, spans in zip(w_refs, lanes):
            keep = None
            for lo, hi in spans:
                m = (lane >= lo) & (lane < hi)
                keep = m if keep is None else keep | m
            w = jnp.where(keep, w_ref[...], w)
        wb_ref[...] = _bf(w)
        os_ref[...] = _dot(as_ref[...], wb_ref[...])

    o_ref[...] = _dot(a_ref[...], wb_ref[...])


def _proj_small(a, a_s, w_in, l, tm):
    m, k = a.shape
    ms = a_s.shape[0]
    src, dst, _, _ = _packed_offsets(k)
    _, _, blocks = _w_in_tile_table(k)
    lanes = [[(dst[n], dst[n] + src[n][1]) for n in _Z_SMALL if src[n][0] // LANE == blk] for blk in blocks]
    in_specs = ([pl.BlockSpec((tm, k), lambda i: (i, 0)), pl.BlockSpec((ms, k), lambda i: (0, 0))]
                + [pl.BlockSpec((None, k, LANE), lambda i, blk=blk: (l, 0, blk)) for blk in blocks])
    return pl.pallas_call(
        functools.partial(_proj_small_kernel, lanes=lanes), grid=(m // tm,),
        in_specs=in_specs,
        out_specs=[pl.BlockSpec((tm, LANE), lambda i: (i, 0)), pl.BlockSpec((ms, LANE), lambda i: (0, 0))],
        out_shape=[jax.ShapeDtypeStruct((m, LANE), F32), jax.ShapeDtypeStruct((ms, LANE), F32)],
        scratch_shapes=[pltpu.VMEM((k, LANE), BF16)],
        compiler_params=_cparams(("arbitrary",)), name="proj_small",
    )(a, a_s, *([w_in] * len(blocks)))


def _mm_resid_kernel(a_ref, as_ref, w_ref, x_ref, xs_ref, gt_ref, gts_ref, o_ref, os_ref, wb_ref):
    @pl.when(pl.program_id(1) == 0)
    def _():
        wb_ref[...] = _bf(w_ref[...])
        os_ref[...] = xs_ref[...] + gts_ref[...] * _dot(as_ref[...], wb_ref[...])

    o_ref[...] = x_ref[...] + gt_ref[...] * _dot(a_ref[...], wb_ref[...])


def _mm_resid(a, a_s, w, x, x_s, modp, mods, l, j_gate, seq, tm, tn, name):
    m, k = a.shape
    ms = a_s.shape[0]
    n = w.shape[2]
    per = seq // tm
    nj = n // tn
    return pl.pallas_call(
        _mm_resid_kernel, grid=(nj, m // tm),
        in_specs=[pl.BlockSpec((tm, k), lambda j, i: (i, 0)),
                  pl.BlockSpec((ms, k), lambda j, i: (0, 0)),
                  pl.BlockSpec((None, k, tn), lambda j, i: (l, 0, j)),
                  pl.BlockSpec((tm, tn), lambda j, i: (i, j)),
                  pl.BlockSpec((ms, tn), lambda j, i: (0, j)),
                  pl.BlockSpec((None, None, 1, tn), lambda j, i: (l, i // per, 0, j_gate * nj + j)),
                  pl.BlockSpec((None, ms, tn), lambda j, i: (l, 0, j_gate * nj + j))],
        out_specs=[pl.BlockSpec((tm, tn), lambda j, i: (i, j)),
                   pl.BlockSpec((ms, tn), lambda j, i: (0, j))],
        out_shape=[jax.ShapeDtypeStruct((m, n), F32), jax.ShapeDtypeStruct((ms, n), F32)],
        scratch_shapes=[pltpu.VMEM((k, tn), BF16)],
        compiler_params=_cparams(("arbitrary", "arbitrary")), name=name,
    )(a, a_s, w, x, x_s, modp, mods)


def _mm_swiglu_kernel(a_ref, as_ref, wg_ref, wu_ref, o_ref, os_ref, wgb_ref, wub_ref):
    @pl.when(pl.program_id(1) == 0)
    def _():
        wgb_ref[...] = _bf(wg_ref[...])
        wub_ref[...] = _bf(wu_ref[...])
        a_s = as_ref[...]
        os_ref[...] = (_silu(_dot(a_s, wgb_ref[...])) * _dot(a_s, wub_ref[...])).astype(os_ref.dtype)

    a = a_ref[...]
    o_ref[...] = (_silu(_dot(a, wgb_ref[...])) * _dot(a, wub_ref[...])).astype(o_ref.dtype)


def _mm_swiglu(a, a_s, wg, wu, l, tm, tn):
    m, k = a.shape
    ms = a_s.shape[0]
    n = wg.shape[2]
    return pl.pallas_call(
        _mm_swiglu_kernel, grid=(n // tn, m // tm),
        in_specs=[pl.BlockSpec((tm, k), lambda j, i: (i, 0)),
                  pl.BlockSpec((ms, k), lambda j, i: (0, 0)),
                  pl.BlockSpec((None, k, tn), lambda j, i: (l, 0, j)),
                  pl.BlockSpec((None, k, tn), lambda j, i: (l, 0, j))],
        out_specs=[pl.BlockSpec((tm, tn), lambda j, i: (i, j)),
                   pl.BlockSpec((ms, tn), lambda j, i: (0, j))],
        out_shape=[jax.ShapeDtypeStruct((m, n), BF16), jax.ShapeDtypeStruct((ms, n), BF16)],
        scratch_shapes=[pltpu.VMEM((k, tn), BF16), pltpu.VMEM((k, tn), BF16)],
        compiler_params=_cparams(("arbitrary", "arbitrary")), name="ffn_gate_up",
    )(a, a_s, wg, wu)


def _merge_kernel(*refs):
    o_p = refs[0:4]
    o_s = refs[4:8]
    g_p = refs[8:12]
    g_s = refs[12:16]
    w_ref, out_ref, outs_ref, wb_ref = refs[16:20]

    @pl.when(pl.program_id(1) == 0)
    def _():
        wb_ref[...] = _bf(w_ref[...])
        acc = None
        for b in range(4):
            t = _sigmoid(g_s[b][...]) * _dot(_bf(o_s[b][...]), wb_ref[b])
            acc = t if acc is None else acc + t
        outs_ref[...] = acc.astype(outs_ref.dtype)

    acc = None
    for b in range(4):
        t = _sigmoid(g_p[b][...]) * _dot(_bf(o_p[b][...]), wb_ref[b])
        acc = t if acc is None else acc + t
    out_ref[...] = acc.astype(out_ref.dtype)


def _merge(o_p, o_s, z, z_s, w_branch, l, merge_off, tm, tn):
    m = z.shape[0]
    ms = z_s.shape[0]
    bw = w_branch.shape[2]
    d = w_branch.shape[3]
    in_specs = ([pl.BlockSpec((tm, bw), lambda j, i: (i, 0)) for _ in range(4)]
                + [pl.BlockSpec((ms, bw), lambda j, i: (0, 0)) for _ in range(4)]
                + [pl.BlockSpec((tm, tn), lambda j, i, b=b: (i, (merge_off + b * d) // tn + j)) for b in range(4)]
                + [pl.BlockSpec((ms, tn), lambda j, i, b=b: (0, (merge_off + b * d) // tn + j)) for b in range(4)]
                + [pl.BlockSpec((None, 4, bw, tn), lambda j, i: (l, 0, 0, j))])
    return pl.pallas_call(
        _merge_kernel, grid=(d // tn, m // tm),
        in_specs=in_specs,
        out_specs=[pl.BlockSpec((tm, tn), lambda j, i: (i, j)), pl.BlockSpec((ms, tn), lambda j, i: (0, j))],
        out_shape=[jax.ShapeDtypeStruct((m, d), BF16), jax.ShapeDtypeStruct((ms, d), BF16)],
        scratch_shapes=[pltpu.VMEM((4, bw, tn), BF16)],
        compiler_params=_cparams(("arbitrary", "arbitrary")), name="merge",
    )(*o_p, *o_s, z, z, z, z, z_s, z_s, z_s, z_s, w_branch)


def _live_rows(c_rows, n_live):
    return lax.broadcasted_iota(jnp.int32, (c_rows, 1), 0) < n_live


def _gla_kernel(z_ref, small_ref, wa_ref, ba_ref, gn_ref, s0_ref, o_ref, sout_ref, st_scr, *, NB, C, L, H, DK, DV):
    c = pl.program_id(0)

    @pl.when(c == 0)
    def _():
        st_scr[...] = s0_ref[...]

    causal = _causal(C)
    wa = _bf(wa_ref[...])
    ba = ba_ref[...]
    gn = gn_ref[...]
    mid = C // 2

    def one_sequence(bi):
        z = z_ref[bi]
        q = z[:, 0:H * DK] * (DK ** -0.5)
        k = z[:, H * DK:2 * H * DK]
        v = z[:, 2 * H * DK:2 * H * DK + H * DV]
        r = z[:, 2 * H * DK + H * DV:]
        ga = _dot(_bf(small_ref[bi]), wa) + ba
        glog = _log_sigmoid(ga) * (1.0 / GLA_TAU)
        if L < C:
            live = _live_rows(C, L)
            glog = jnp.where(live, glog, 0.0)
            k = jnp.where(live, k, 0.0)
        b = _cumsum_rows(glog)
        bl = b[C - 1:C, :]
        bm = b[mid - 1:mid, :]
        qi = _bf(q * jnp.exp(b - bm))
        ki = _bf(k * jnp.exp(bm - b))
        qs = q * jnp.exp(b)
        kd = _bf(k * jnp.exp(bl - b))
        ebl_col = jnp.exp(b.T[:, C - 1:C])
        outs = []
        for h in range(H):
            ks = slice(h * DK, (h + 1) * DK)
            vs = slice(h * DV, (h + 1) * DV)
            vh = v[:, vs]
            a = jnp.where(causal, _dot_nt(qi[:, ks], ki[:, ks]), 0.0)
            st = st_scr[bi, h]
            o = _dot(_bf(jnp.concatenate([a, qs[:, ks]], axis=1)), _bf(jnp.concatenate([vh, st], axis=0)))
            st_scr[bi, h] = ebl_col[ks, :] * st + _dot_tn(kd[:, ks], _bf(vh))
            outs.append(_rms(o, gn[:, vs]))
        o_ref[bi] = (jnp.concatenate(outs, axis=1) * _silu(r)).astype(o_ref.dtype)

    for bi in range(NB):
        one_sequence(bi)

    @pl.when(c == pl.num_programs(0) - 1)
    def _():
        sout_ref[...] = st_scr[...]


def _gla_call(z, zsm, wa_pad, ba, gn, s0, l, nb, seq, C, L, out_dtype):
    H, DK, DV = s0.shape[1:]
    nch = seq // C
    width = 2 * H * DK + 2 * H * DV
    z3 = z.reshape(nb, seq, z.shape[1])
    zsm3 = zsm.reshape(nb, seq, LANE)
    kern = functools.partial(_gla_kernel, NB=nb, C=C, L=L, H=H, DK=DK, DV=DV)
    o, s = pl.pallas_call(
        kern, grid=(nch,),
        in_specs=[pl.BlockSpec((nb, C, width), lambda c: (0, c, 0)),
                  pl.BlockSpec((nb, C, LANE), lambda c: (0, c, 0)),
                  pl.BlockSpec((None, LANE, H * DK), lambda c: (l, 0, 0)),
                  pl.BlockSpec((None, 1, H * DK), lambda c: (l, 0, 0)),
                  pl.BlockSpec((None, 1, H * DV), lambda c: (l, 0, 0)),
                  pl.BlockSpec((nb, H, DK, DV), lambda c: (0, 0, 0, 0))],
        out_specs=[pl.BlockSpec((nb, C, H * DV), lambda c: (0, c, 0)),
                   pl.BlockSpec((nb, H, DK, DV), lambda c: (0, 0, 0, 0))],
        out_shape=[jax.ShapeDtypeStruct((nb, seq, H * DV), out_dtype),
                   jax.ShapeDtypeStruct((nb, H, DK, DV), F32)],
        scratch_shapes=[pltpu.VMEM((nb, H, DK, DV), F32)],
        compiler_params=_cparams(("arbitrary",)), name="gla",
    )(z3, zsm3, wa_pad, ba, gn, s0)
    return o.reshape(nb * seq, H * DV), s


def _ssd_kernel(xbc_ref, zg_ref, small_ref, cw_ref, cb_ref, dtb_ref, alog_ref, dvec_ref, gn_ref, s0_ref, conv0_ref,
                o_ref, sout_ref, convout_ref, s_scr, ext_scr, *, NB, C, L, H, G, N, P, dt_lane):
    c = pl.program_id(0)
    kh = SSD_CONV - 1
    base = SUBLANE - kh

    @pl.when(c == 0)
    def _():
        s_scr[...] = s0_ref[...]
        ext_scr[:, base:SUBLANE, :] = conv0_ref[...]

    cw = cw_ref[...]
    cb = cb_ref[...]
    dtb = dtb_ref[...]
    neg_a = -jnp.exp(alog_ref[...])
    dvec = dvec_ref[...]
    gn = gn_ref[...]
    causal = _causal(C)
    bw = H * P
    hpg = H // G

    def one_sequence(bi):
        ext_scr[bi, SUBLANE:SUBLANE + C, :] = xbc_ref[bi]
        conv = cb
        for j in range(SSD_CONV):
            conv = conv + ext_scr[bi, pl.ds(base + j, C), :] * cw[j:j + 1, :]
        hist = ext_scr[bi, pl.ds(base + L, kh), :]
        ext_scr[bi, base:SUBLANE, :] = hist
        xc = _silu(conv)
        sx = xc[:, 0:bw]
        bm = xc[:, bw:bw + G * N]
        cm = xc[:, bw + G * N:]
        dt = _softplus(small_ref[bi] + dtb)
        logg = dt * neg_a
        if L < C:
            live = _live_rows(C, L)
            logg = jnp.where(live, logg, 0.0)
            bm = jnp.where(live, bm, 0.0)
        b = _cumsum_rows(logg)
        b_t = b.T
        ys = []
        for g in range(G):
            cg = cm[:, g * N:(g + 1) * N]
            bg = bm[:, g * N:(g + 1) * N]
            gmat = _dot_nt(_bf(cg), _bf(bg))
            for hh in range(hpg):
                h = g * hpg + hh
                ln = dt_lane + h
                bc = b[:, ln:ln + 1]
                br = b_t[ln:ln + 1, :]
                dec = jnp.exp(jnp.where(causal, bc - br, -jnp.inf))
                xdt = sx[:, h * P:(h + 1) * P] * dt[:, ln:ln + 1]
                st = s_scr[bi, h]
                y = _dot(_bf(jnp.concatenate([gmat * dec, cg * jnp.exp(bc)], axis=1)),
                         _bf(jnp.concatenate([xdt, st], axis=0)))
                bl = bc[C - 1:C, :]
                s_scr[bi, h] = jnp.exp(bl) * st + _dot_tn(_bf(bg * jnp.exp(bl - bc)), _bf(xdt))
                ys.append(y)
        y_all = jnp.concatenate(ys, axis=1) + dvec * sx
        o_ref[bi] = _rms(y_all * _silu(zg_ref[bi]), gn).astype(o_ref.dtype)

    for bi in range(NB):
        one_sequence(bi)

    @pl.when(c == pl.num_programs(0) - 1)
    def _():
        sout_ref[...] = s_scr[...]
        convout_ref[...] = ext_scr[:, base:SUBLANE, :]


def _ssd_call(z, zsm, xbc_idx, zg_idx, cw, cb, dtb, alog, dvec, gn, s0, conv0, l, nb, seq, C, L, dt_lane,
              out_dtype):
    H, N, P = s0.shape[1:]
    G = SSD_GROUPS
    ch = conv0.shape[2]
    bw = H * P
    nch = seq // C
    z3 = z.reshape(nb, seq, z.shape[1])
    zsm3 = zsm.reshape(nb, seq, LANE)
    kern = functools.partial(_ssd_kernel, NB=nb, C=C, L=L, H=H, G=G, N=N, P=P, dt_lane=dt_lane)
    vec = lambda w: pl.BlockSpec((None, 1, w), lambda c: (l, 0, 0))
    o, s, cv = pl.pallas_call(
        kern, grid=(nch,),
        in_specs=[pl.BlockSpec((nb, C, ch), lambda c: (0, c, xbc_idx)),
                  pl.BlockSpec((nb, C, bw), lambda c: (0, c, zg_idx)),
                  pl.BlockSpec((nb, C, LANE), lambda c: (0, c, 0)),
                  pl.BlockSpec((None, SSD_CONV, ch), lambda c: (l, 0, 0)),
                  vec(ch), vec(LANE), vec(LANE), vec(bw), vec(bw),
                  pl.BlockSpec((nb, H, N, P), lambda c: (0, 0, 0, 0)),
                  pl.BlockSpec((nb, SSD_CONV - 1, ch), lambda c: (0, 0, 0))],
        out_specs=[pl.BlockSpec((nb, C, bw), lambda c: (0, c, 0)),
                   pl.BlockSpec((nb, H, N, P), lambda c: (0, 0, 0, 0)),
                   pl.BlockSpec((nb, SSD_CONV - 1, ch), lambda c: (0, 0, 0))],
        out_shape=[jax.ShapeDtypeStruct((nb, seq, bw), out_dtype),
                   jax.ShapeDtypeStruct((nb, H, N, P), F32),
                   jax.ShapeDtypeStruct((nb, SSD_CONV - 1, ch), F32)],
        scratch_shapes=[pltpu.VMEM((nb, H, N, P), F32), pltpu.VMEM((nb, SUBLANE + C, ch), F32)],
        compiler_params=_cparams(("arbitrary",)), name="ssd",
    )(z3, z3, zsm3, cw, cb, dtb, alog, dvec, gn, s0, conv0)
    return o.reshape(nb * seq, bw), s, cv


def _mlstm_kernel(qk_ref, vo_ref, small_ref, bif_ref, gn_ref, c0_ref, nm0_ref, o_ref, cout_ref, nmout_ref,
                  c_scr, nm_scr, *, NB, C, L, H, DH, i_lane, f_lane):
    c = pl.program_id(0)

    @pl.when(c == 0)
    def _():
        c_scr[...] = c0_ref[...]
        nm_scr[...] = nm0_ref[...]

    causal = _causal(C)
    lane = lax.broadcasted_iota(jnp.int32, (1, LANE), 1)
    bif = bif_ref[...]
    gn = gn_ref[...]

    def one_sequence(bi):
        qk = qk_ref[bi]
        vo = vo_ref[bi]
        q = qk[:, 0:H * DH]
        k = qk[:, H * DH:] * (DH ** -0.5)
        v = vo[:, 0:H * DH]
        og = vo[:, H * DH:]
        pre = small_ref[bi] + bif
        logf = _log_sigmoid(pre)
        ipre = pre
        if L < C:
            live = _live_rows(C, L)
            logf = jnp.where(live, logf, 0.0)
            ipre = jnp.where(live, pre, NEG)
        b = _cumsum_rows(logf)
        b_t = b.T
        i_t = ipre.T
        nm = nm_scr[bi]
        m_row = nm[H:H + 1, :]
        outs = []
        for h in range(H):
            sl = slice(h * DH, (h + 1) * DH)
            bc = b[:, f_lane + h:f_lane + h + 1]
            br = b_t[f_lane + h:f_lane + h + 1, :]
            ic = ipre[:, i_lane + h:i_lane + h + 1]
            ir = i_t[i_lane + h:i_lane + h + 1, :]
            m_prev = nm[H:H + 1, h:h + 1]
            logd = jnp.where(causal, bc - br + ir, -jnp.inf)
            gg = bc + m_prev
            m = jnp.maximum(gg, jnp.max(logd, axis=1, keepdims=True))
            dm = jnp.exp(logd - m)
            wi = jnp.exp(gg - m)
            qf = q[:, sl]
            kh = k[:, sl]
            vf = v[:, sl]
            vh = _bf(vf)
            qkm = _dot_nt(_bf(qf), _bf(kh)) * dm
            cst = c_scr[bi, h]
            nrow = nm[h:h + 1, :]
            num = _dot(_bf(jnp.concatenate([qkm, wi * qf], axis=1)), _bf(jnp.concatenate([vf, cst], axis=0)))
            den = jnp.sum(qkm, axis=1, keepdims=True) + wi * jnp.sum(qf * nrow, axis=1, keepdims=True)
            hid = num / jnp.maximum(jnp.abs(den), jnp.exp(-m))
            m_new = m[C - 1:C, :]
            bl = bc[C - 1:C, :]
            ws = jnp.exp(bl - bc + ic - m_new)
            wc = jnp.exp(bl + m_prev - m_new)
            wk = ws * kh
            c_scr[bi, h] = wc * cst + _dot_tn(_bf(wk), vh)
            nm_scr[bi, h:h + 1, :] = wc * nrow + jnp.sum(wk, axis=0, keepdims=True)
            m_row = jnp.where(lane == h, m_new, m_row)
            outs.append(_rms(hid, gn[:, sl]))
        nm_scr[bi, H:H + 1, :] = m_row
        o_ref[bi] = (jnp.concatenate(outs, axis=1) * _sigmoid(og)).astype(o_ref.dtype)

    for bi in range(NB):
        one_sequence(bi)

    @pl.when(c == pl.num_programs(0) - 1)
    def _():
        cout_ref[...] = c_scr[...]
        nmout_ref[...] = nm_scr[...]


def _mlstm_call(z, zsm, qk_idx, vo_idx, bif, gn, c0, nm0, l, nb, seq, C, L, i_lane, f_lane, out_dtype):
    H, DH = c0.shape[1:3]
    nch = seq // C
    z3 = z.reshape(nb, seq, z.shape[1])
    zsm3 = zsm.reshape(nb, seq, LANE)
    kern = functools.partial(_mlstm_kernel, NB=nb, C=C, L=L, H=H, DH=DH, i_lane=i_lane, f_lane=f_lane)
    o, cc, nm = pl.pallas_call(
        kern, grid=(nch,),
        in_specs=[pl.BlockSpec((nb, C, 2 * H * DH), lambda c: (0, c, qk_idx)),
                  pl.BlockSpec((nb, C, 2 * H * DH), lambda c: (0, c, vo_idx)),
                  pl.BlockSpec((nb, C, LANE), lambda c: (0, c, 0)),
                  pl.BlockSpec((None, 1, LANE), lambda c: (l, 0, 0)),
                  pl.BlockSpec((None, 1, H * DH), lambda c: (l, 0, 0)),
                  pl.BlockSpec((nb, H, DH, DH), lambda c: (0, 0, 0, 0)),
                  pl.BlockSpec((nb, SUBLANE, LANE), lambda c: (0, 0, 0))],
        out_specs=[pl.BlockSpec((nb, C, H * DH), lambda c: (0, c, 0)),
                   pl.BlockSpec((nb, H, DH, DH), lambda c: (0, 0, 0, 0)),
                   pl.BlockSpec((nb, SUBLANE, LANE), lambda c: (0, 0, 0))],
        out_shape=[jax.ShapeDtypeStruct((nb, seq, H * DH), out_dtype),
                   jax.ShapeDtypeStruct((nb, H, DH, DH), F32),
                   jax.ShapeDtypeStruct((nb, SUBLANE, LANE), F32)],
        scratch_shapes=[pltpu.VMEM((nb, H, DH, DH), F32), pltpu.VMEM((nb, SUBLANE, LANE), F32)],
        compiler_params=_cparams(("arbitrary",)), name="mlstm",
    )(z3, z3, zsm3, bif, gn, c0, nm0)
    return o.reshape(nb * seq, H * DH), cc, nm


def _rank_select(score, n_blocks, keep):
    lane = lax.broadcasted_iota(jnp.int32, score.shape, 1)
    cnt = jnp.zeros(score.shape, jnp.int32)
    for j in range(n_blocks):
        cj = score[:, j:j + 1]
        ahead = jnp.where(cj > score, 1, jnp.where(cj == score, jnp.where(lane > j, 1, 0), 0))
        cnt = cnt + ahead
    return cnt < keep


def _nsa_prompt_kernel(q_ref, kv_ref, small_ref, o_ref, kvb_ref, cmp_ref, *, T, TQ, DH, g_lane):
    qi = pl.program_id(1)
    G, R = NSA_KV, NSA_R
    gw = G * DH
    ncb = T // CMP_BLOCK
    nsb = T // SEL_BLOCK
    ratio = SEL_BLOCK // CMP_BLOCK
    slopes = _alibi_slopes()

    @pl.when(qi == 0)
    def _():
        kvb_ref[...] = _bf(kv_ref[...])
        hi, lo = _split2(kv_ref[:, 0:2 * gw])
        jj = lax.broadcasted_iota(jnp.int32, (ncb, T), 0) * CMP_BLOCK
        tt = lax.broadcasted_iota(jnp.int32, (ncb, T), 1)
        pm = jnp.where((tt >= jj) & (tt < jj + CMP_BLOCK), 1.0 / CMP_BLOCK, 0.0).astype(BF16)
        cmp_ref[...] = _bf(_dot(pm, hi) + _dot(pm, lo))

    q = q_ref[...]
    gate = _sigmoid(small_ref[...])
    posl = qi * TQ + lax.broadcasted_iota(jnp.int32, (1, TQ), 1)
    cbs = lax.broadcasted_iota(jnp.int32, (ncb, 1), 0)
    cvalid = (cbs * CMP_BLOCK + (CMP_BLOCK - 1)) <= posl
    cdist = posl.astype(F32) - (cbs.astype(F32) * CMP_BLOCK + (CMP_BLOCK - 1) / 2.0)
    sbs = lax.broadcasted_iota(jnp.int32, (nsb, 1), 0)
    svalid = sbs * SEL_BLOCK <= posl
    forced = (sbs == 0) | ((sbs * SEL_BLOCK <= posl) & (posl < sbs * SEL_BLOCK + SEL_BLOCK))
    pi = lax.broadcasted_iota(jnp.int32, (nsb, ncb), 0) * ratio
    pj = lax.broadcasted_iota(jnp.int32, (nsb, ncb), 1)
    pool = jnp.where((pj >= pi) & (pj < pi + ratio), 1.0, 0.0).astype(BF16)
    brow = lax.broadcasted_iota(jnp.int32, (nsb, TQ), 0)
    row2 = lax.broadcasted_iota(jnp.int32, (R * TQ, 1), 0)
    rr = lax.broadcasted_iota(jnp.int32, (TQ, TQ), 0)
    cc = lax.broadcasted_iota(jnp.int32, (TQ, TQ), 1)
    lower = rr >= cc
    lower2 = jnp.concatenate([lower] * R, axis=0)
    col = lax.broadcasted_iota(jnp.int32, (1, TQ), 1)
    nfar = WINDOW // TQ

    outs = [None] * (G * R)
    for g in range(G):
        kc = cmp_ref[:, g * DH:(g + 1) * DH]
        vc = cmp_ref[:, gw + g * DH:gw + (g + 1) * DH]
        qs = []
        o_cmp = []
        imp = jnp.zeros((nsb, TQ), F32)
        for r in range(R):
            hd = g * R + r
            qh = _bf(q[:, hd * DH:(hd + 1) * DH] * (DH ** -0.5))
            qs.append(qh)
            s = jnp.where(cvalid, _dot_nt(kc, qh) - slopes[hd] * cdist, NEG)
            e = jnp.exp(s - jnp.max(s, axis=0, keepdims=True))
            p = jnp.where(cvalid, e / jnp.sum(e, axis=0, keepdims=True), 0.0)
            o_cmp.append(_dot_tn(_bf(p), vc))
            ph, plo = _split2(p)
            imp = imp + _dot(pool, ph) + _dot(pool, plo)
        score = jnp.where(svalid, jnp.where(forced, FORCE_SCORE, imp), -1.0)
        ahead = jnp.zeros((nsb, TQ), jnp.int32)
        for j in range(nsb):
            sj = score[j:j + 1, :]
            ahead = ahead + jnp.where(sj > score, 1, jnp.where(sj == score, jnp.where(brow > j, 1, 0), 0))
        sel = _bf(jnp.where((ahead < SEL_TOPK) & (score >= 0.0), 1.0, 0.0))
        q2 = jnp.concatenate(qs, axis=0)
        slope2 = slopes[g * R]
        for r in range(1, R):
            slope2 = jnp.where(row2 >= r * TQ, slopes[g * R + r], slope2)

        def tile(carry, kt, kcol, vcol, mask):
            m, l, acc = carry
            k0 = pl.multiple_of(kt * TQ, TQ)
            ks = kvb_ref[pl.ds(k0, TQ), kcol:kcol + DH]
            vs = kvb_ref[pl.ds(k0, TQ), vcol:vcol + DH]
            s = _dot_nt(q2, ks) + slope2 * (k0 + col).astype(F32)
            if mask is not None:
                s = jnp.where(mask, s, NEG)
            mn = jnp.maximum(m, jnp.max(s, axis=1, keepdims=True))
            p = jnp.exp(s - mn)
            alpha = jnp.exp(m - mn)
            return (mn, alpha * l + jnp.sum(p, axis=1, keepdims=True), alpha * acc + _dot(_bf(p), vs))

        def sel_mask(kt):
            kpos = kt * TQ + col
            expand = _bf(jnp.where((kpos >= sbs * SEL_BLOCK) & (kpos < sbs * SEL_BLOCK + SEL_BLOCK), 1.0, 0.0))
            return _dot_tn(sel, expand) > 0.5

        init = (jnp.full((R * TQ, 1), NEG, F32), jnp.zeros((R * TQ, 1), F32), jnp.zeros((R * TQ, DH), F32))
        kcol, vcol = 2 * gw + g * DH, 3 * gw + g * DH
        carry = tile(init, qi, kcol, vcol, jnp.concatenate([sel_mask(qi) & lower] * R, axis=0))
        carry = lax.fori_loop(
            0, qi, lambda kt, cr: tile(cr, kt, kcol, vcol, jnp.concatenate([sel_mask(kt)] * R, axis=0)), carry)
        o_slc = carry[2] / carry[1]
        kcol, vcol = 4 * gw + g * DH, 5 * gw + g * DH
        carry = tile(init, qi, kcol, vcol, lower2)
        for back in range(1, nfar + 1):
            kt = jnp.maximum(qi - back, 0)
            live = qi >= back
            mask = jnp.logical_and(jnp.logical_not(lower2), live) if back == nfar else live
            carry = tile(carry, kt, kcol, vcol, mask)
        o_win = carry[2] / carry[1]
        for r in range(R):
            hd = g * R + r
            gc = gate[:, g_lane + hd:g_lane + hd + 1]
            gs = gate[:, g_lane + NSA_HEADS + hd:g_lane + NSA_HEADS + hd + 1]
            gwn = gate[:, g_lane + 2 * NSA_HEADS + hd:g_lane + 2 * NSA_HEADS + hd + 1]
            outs[hd] = gc * o_cmp[r] + gs * o_slc[r * TQ:(r + 1) * TQ] + gwn * o_win[r * TQ:(r + 1) * TQ]
    o_ref[...] = jnp.concatenate(outs, axis=1).astype(o_ref.dtype)


def _nsa_prompt_call(z, zsm, q_idx, kv_idx, nb, seq, g_lane, tq=256):
    dh = LANE
    kvw = 6 * NSA_KV * dh
    qw = NSA_HEADS * dh
    nq = seq // tq
    assert WINDOW % tq == 0 and seq % tq == 0
    kern = functools.partial(_nsa_prompt_kernel, T=seq, TQ=tq, DH=dh, g_lane=g_lane)
    return pl.pallas_call(
        kern, grid=(nb, nq),
        in_specs=[pl.BlockSpec((tq, qw), lambda b, i: (b * nq + i, q_idx)),
                  pl.BlockSpec((seq, kvw), lambda b, i: (b, kv_idx)),
                  pl.BlockSpec((tq, LANE), lambda b, i: (b * nq + i, 0))],
        out_specs=pl.BlockSpec((tq, qw), lambda b, i: (b * nq + i, 0)),
        out_shape=jax.ShapeDtypeStruct((nb * seq, qw), BF16),
        scratch_shapes=[pltpu.VMEM((seq, kvw), BF16), pltpu.VMEM((seq // CMP_BLOCK, 2 * NSA_KV * dh), BF16)],
        compiler_params=_cparams(("arbitrary", "arbitrary")), name="nsa_prompt",
    )(z, z, zsm)


def _nsa_means_kernel(pt_ref, *refs, PP, DH):
    ins = refs[:PP]
    o_ref = refs[PP]
    nper = PAGE_SIZE // CMP_BLOCK
    for i in range(PP):
        x = ins[i][...].reshape(nper, CMP_BLOCK, 2, NSA_KV, DH)
        o_ref[i * nper:(i + 1) * nper] = jnp.sum(x, axis=1) * (1.0 / CMP_BLOCK)


def _nsa_means_call(cache, page_table, l, pp=8):
    nb, n_pages = page_table.shape
    dh = cache.shape[5]
    nper = PAGE_SIZE // CMP_BLOCK
    assert cache.shape[2] == PAGE_SIZE and cache.shape[4] == NSA_KV and n_pages % pp == 0
    in_specs = [pl.BlockSpec((None, None, PAGE_SIZE, 2, NSA_KV, dh),
                             lambda b, s, pt, i=i: (l, pt[b, s * pp + i], 0, 0, 0, 0)) for i in range(pp)]
    gs = pltpu.PrefetchScalarGridSpec(
        num_scalar_prefetch=1, grid=(nb, n_pages // pp), in_specs=in_specs,
        out_specs=pl.BlockSpec((None, nper * pp, 2, NSA_KV, dh), lambda b, s, pt: (b, s, 0, 0, 0)))
    means = pl.pallas_call(
        functools.partial(_nsa_means_kernel, PP=pp, DH=dh), grid_spec=gs,
        out_shape=jax.ShapeDtypeStruct((nb, n_pages * nper, 2, NSA_KV, dh), F32),
        compiler_params=_cparams(("arbitrary", "arbitrary")), name="nsa_cmp_means",
    )(page_table, *([cache] * pp))
    return means.reshape(nb, n_pages * nper, 2 * NSA_KV * dh)


def _nsa_score_kernel(q_ref, kvc_ref, ocmp_ref, ids_ref, *, DH, qpos):
    G, R = NSA_KV, NSA_R
    gw = G * DH
    ncb = kvc_ref.shape[0]
    ratio = SEL_BLOCK // CMP_BLOCK
    nsb = ncb // ratio
    slopes = _alibi_slopes()
    q = q_ref[...]
    kvc = _bf(kvc_ref[...])
    rows = q.shape[0]
    cb = lax.broadcasted_iota(jnp.int32, (1, ncb), 1)
    cvalid = (cb * CMP_BLOCK + (CMP_BLOCK - 1)) <= qpos
    cdist = float(qpos) - (cb.astype(F32) * CMP_BLOCK + (CMP_BLOCK - 1) / 2.0)
    pi = lax.broadcasted_iota(jnp.int32, (ncb, nsb), 0)
    pj = lax.broadcasted_iota(jnp.int32, (ncb, nsb), 1) * ratio
    pool = jnp.where((pi >= pj) & (pi < pj + ratio), 1.0, 0.0).astype(BF16)
    lane = lax.broadcasted_iota(jnp.int32, (rows, nsb), 1).astype(F32)
    lane_out = lax.broadcasted_iota(jnp.int32, (rows, LANE), 1)
    o_cmp = []
    id_rows = []
    for g in range(G):
        kc = kvc[:, g * DH:(g + 1) * DH]
        vc = kvc[:, gw + g * DH:gw + (g + 1) * DH]
        imp = jnp.zeros((rows, nsb), F32)
        for r in range(R):
            hd = g * R + r
            qh = _bf(q[:, hd * DH:(hd + 1) * DH] * (DH ** -0.5))
            s = jnp.where(cvalid, _dot_nt(qh, kc) - slopes[hd] * cdist, NEG)
            p = jnp.where(cvalid, _softmax_rows(s), 0.0)
            o_cmp.append(_dot(_bf(p), vc))
            ph, plo = _split2(p)
            imp = imp + _dot(ph, pool) + _dot(plo, pool)
        val = jnp.where(lane == 0.0, -1.0, imp)
        ids = jnp.zeros((rows, LANE), F32)
        for it in range(SEL_TOPK - 2):
            mx = jnp.max(val, axis=1, keepdims=True)
            ix = jnp.min(jnp.where(val == mx, lane, float(nsb)), axis=1, keepdims=True)
            ids = jnp.where(lane_out == it + 1, ix, ids)
            val = jnp.where(lane == ix, -2.0, val)
        id_rows.append(ids[0:1, :])
    ocmp_ref[...] = jnp.concatenate(o_cmp, axis=1)
    pad = jnp.zeros((ids_ref.shape[0] - G, LANE), F32)
    ids_ref[...] = jnp.concatenate(id_rows + [pad], axis=0).astype(jnp.int32)


def _nsa_score_call(z_s, q_idx, kvc, qpos):
    nb, ncb, w = kvc.shape
    dh = w // (2 * NSA_KV)
    qw = NSA_HEADS * dh
    return pl.pallas_call(
        functools.partial(_nsa_score_kernel, DH=dh, qpos=qpos), grid=(nb,),
        in_specs=[pl.BlockSpec((SROWS, qw), lambda b: (b, q_idx)),
                  pl.BlockSpec((None, ncb, w), lambda b: (b, 0, 0))],
        out_specs=[pl.BlockSpec((SROWS, qw), lambda b: (b, 0)),
                   pl.BlockSpec((None, SUBLANE, LANE), lambda b: (b, 0, 0))],
        out_shape=[jax.ShapeDtypeStruct((nb * SROWS, qw), F32),
                   jax.ShapeDtypeStruct((nb, SUBLANE, LANE), jnp.int32)],
        compiler_params=_cparams(("arbitrary",)), name="nsa_cmp_score",
    )(z_s, kvc)


def _nsa_gather_kernel(sel_ref, q_ref, new_ref, small_ref, ocmp_ref, win_ref, *rest,
                       DH, NSEL, qpos, g_lane):
    kv = rest[:NSA_KV * NSEL]
    o_ref = rest[NSA_KV * NSEL]
    b = pl.program_id(0)
    G, R = NSA_KV, NSA_R
    gw = G * DH
    slopes = _alibi_slopes()
    q = q_ref[...]
    new = new_ref[...]
    gate = _sigmoid(small_ref[...])
    ocmp = ocmp_ref[...]
    wlen = win_ref.shape[0]
    jw = lax.broadcasted_iota(jnp.int32, (1, wlen), 1)
    distw = wlen - jw
    validw = (distw < WINDOW) & (qpos - distw >= 0)
    off = lax.broadcasted_iota(jnp.int32, (1, SEL_BLOCK), 1)
    outs = []
    for g in range(G):
        ksn = _bf(new[:, 2 * gw + g * DH:2 * gw + (g + 1) * DH]).astype(F32)
        vsn = _bf(new[:, 3 * gw + g * DH:3 * gw + (g + 1) * DH]).astype(F32)
        kwn = _bf(new[:, 4 * gw + g * DH:4 * gw + (g + 1) * DH]).astype(F32)
        vwn = _bf(new[:, 5 * gw + g * DH:5 * gw + (g + 1) * DH]).astype(F32)
        kw = _bf(win_ref[:, g * DH:(g + 1) * DH])
        vw = _bf(win_ref[:, gw + g * DH:gw + (g + 1) * DH])
        kblk = [_bf(kv[g * NSEL + i][:, 0, g, :]) for i in range(NSEL)]
        vblk = [_bf(kv[g * NSEL + i][:, 1, g, :]) for i in range(NSEL)]
        nslot = NSA_KV * SEL_TOPK
        dists = [qpos - (sel_ref[b, 2 * nslot + g * SEL_TOPK + i] * SEL_BLOCK + off) for i in range(NSEL)]
        for r in range(R):
            hd = g * R + r
            slope = slopes[hd]
            qh = _bf(q[:, hd * DH:(hd + 1) * DH] * (DH ** -0.5))
            qf = qh.astype(F32)
            s_list = [jnp.where(dists[i] >= 0, _dot_nt(qh, kblk[i]) - slope * dists[i].astype(F32), NEG)
                      for i in range(NSEL)]
            s_new = jnp.sum(qf * ksn, axis=1, keepdims=True)
            mx = s_new
            for s in s_list:
                mx = jnp.maximum(mx, jnp.max(s, axis=1, keepdims=True))
            p_new = jnp.exp(s_new - mx)
            den = p_new
            acc = p_new * vsn
            for i, s in enumerate(s_list):
                p = jnp.exp(s - mx)
                den = den + jnp.sum(p, axis=1, keepdims=True)
                acc = acc + _dot(_bf(p), vblk[i])
            o_slc = acc / den
            s = jnp.where(validw, _dot_nt(qh, kw) - slope * distw.astype(F32), NEG)
            s_new = jnp.sum(qf * kwn, axis=1, keepdims=True)
            mx = jnp.maximum(s_new, jnp.max(s, axis=1, keepdims=True))
            p = jnp.exp(s - mx)
            p_new = jnp.exp(s_new - mx)
            o_win = (_dot(_bf(p), vw) + p_new * vwn) / (jnp.sum(p, axis=1, keepdims=True) + p_new)
            gc, gs, gwn = [gate[:, g_lane + c * NSA_HEADS + hd:g_lane + c * NSA_HEADS + hd + 1] for c in range(3)]
            outs.append(gc * ocmp[:, hd * DH:(hd + 1) * DH] + gs * o_slc + gwn * o_win)
    o_ref[...] = jnp.concatenate(outs, axis=1)


def _nsa_gather_call(z_s, zsm_s, q_idx, kv_idx, ocmp, win, cache, page_table, ids, l, qpos, g_lane):
    nb = page_table.shape[0]
    dh = cache.shape[5]
    G = NSA_KV
    qw = NSA_HEADS * dh
    nsel = SEL_TOPK - 1
    bpp = PAGE_SIZE // SEL_BLOCK
    wlen, ww = win.shape[2:]
    nslot = G * SEL_TOPK
    sel = jnp.concatenate([jnp.take_along_axis(page_table, ids // bpp, axis=1), ids % bpp, ids], axis=1)
    in_specs = [pl.BlockSpec((SROWS, qw), lambda b, s: (b, q_idx)),
                pl.BlockSpec((SROWS, 6 * G * dh), lambda b, s: (b, kv_idx)),
                pl.BlockSpec((SROWS, LANE), lambda b, s: (b, 0)),
                pl.BlockSpec((SROWS, qw), lambda b, s: (b, 0)),
                pl.BlockSpec((None, None, wlen, ww), lambda b, s: (l, b, 0, 0))]
    for g in range(G):
        for i in range(nsel):
            in_specs.append(pl.BlockSpec(
                (None, None, SEL_BLOCK, 2, G, dh),
                lambda b, s, j=g * SEL_TOPK + i: (l, s[b, j], s[b, nslot + j], 1, 0, 0)))
    gs = pltpu.PrefetchScalarGridSpec(
        num_scalar_prefetch=1, grid=(nb,), in_specs=in_specs,
        out_specs=pl.BlockSpec((SROWS, qw), lambda b, s: (b, 0)))
    kern = functools.partial(_nsa_gather_kernel, DH=dh, NSEL=nsel, qpos=qpos, g_lane=g_lane)
    return pl.pallas_call(
        kern, grid_spec=gs,
        out_shape=jax.ShapeDtypeStruct((nb * SROWS, qw), F32),
        compiler_params=_cparams(("arbitrary",)), name="nsa_sel_win",
    )(sel, z_s, z_s, zsm_s, ocmp, win, *([cache] * (G * nsel)))


def _lane_vec(depth, lane, values):
    n = values.shape[1]
    return jnp.zeros((depth, 1, LANE), F32).at[:, 0, lane:lane + n].set(values.astype(F32))


def kernel(x_prompt, x_sample, cache_nsa_kv, cache_nsa_win, state_gla, state_ssd, state_ssd_conv, state_mlstm_c, state_mlstm_n, state_mlstm_m, page_table, c_prompt, c_sample, ada_w, ada_b, norm_mix_g, norm_ffn_g, w_in, gla_w_a, gla_b_a, gla_norm_g, ssd_conv_w, ssd_conv_b, ssd_dt_bias, ssd_a_log, ssd_d, ssd_norm_g, ml_b_i, ml_b_f, ml_norm_g, w_branch, w_out, ffn_w_gate, ffn_w_up, ffn_w_down, final_norm_g):
    nbp, seq, d = x_prompt.shape
    nbs, dec_seq, _ = x_sample.shape
    assert dec_seq == 1 and seq % CHUNK == 0
    depth = w_in.shape[0]
    past_len = page_table.shape[1] * PAGE_SIZE
    bw = d // 4
    dh = bw // NSA_HEADS
    gw = NSA_KV * dh
    src, dst, n_packed, _ = _packed_offsets(d)
    kv_off = dst['nsa_kv']
    q_idx = dst['nsa_q'] // bw
    kv_idx = kv_off // (6 * gw)
    xbc_idx = dst['ssd_xbc'] // src['ssd_xbc'][1]
    zg_idx = dst['ssd_z'] // bw
    mlqk_idx = dst['ml_q'] // (2 * bw)
    mlvo_idx = dst['ml_v'] // (2 * bw)
    for name, width in (('gla_q', 3 * bw), ('nsa_q', bw), ('nsa_kv', 6 * gw), ('ssd_xbc', src['ssd_xbc'][1]),
                        ('ssd_z', bw), ('ml_q', 2 * bw), ('ml_v', 2 * bw), ('merge', Z_TILE)):
        assert dst[name] % width == 0
    assert dst['gla_q'] == 0
    g_lane, dt_lane, i_lane, f_lane = dst['nsa_g'], dst['ssd_dt'], dst['ml_i'], dst['ml_f']

    crow = 2 * SUBLANE
    c_all = jnp.concatenate([c_prompt, c_sample, jnp.zeros((crow - nbp - nbs, d), F32)], axis=0)
    mod = _ada(c_all, ada_w, ada_b)
    modp = mod[:, :nbp].reshape(depth, nbp, 1, 6 * d)
    mods = jnp.repeat(mod[:, nbp:nbp + nbs], SROWS, axis=1)

    r3 = lambda a: a.reshape(depth, 1, a.shape[-1])
    wa_pad = jnp.zeros((depth, LANE, gla_w_a.shape[2]), F32).at[:, dst['gla_a']:dst['gla_a'] + GLA_RANK].set(gla_w_a)
    dtb = _lane_vec(depth, dt_lane, ssd_dt_bias)
    alog = _lane_vec(depth, dt_lane, ssd_a_log)
    dvec = jnp.repeat(ssd_d, bw // SSD_HEADS, axis=1).reshape(depth, 1, bw)
    bif = _lane_vec(depth, i_lane, ml_b_i) + _lane_vec(depth, f_lane, ml_b_f)
    win4 = cache_nsa_win.reshape(depth, nbs, cache_nsa_win.shape[2], 2 * gw)

    xp = x_prompt.reshape(nbp * seq, d)
    xs = jnp.zeros((nbs, SROWS, d), F32).at[:, 0].set(x_sample[:, 0]).reshape(nbs * SROWS, d)

    zeros_like_p = lambda a: jnp.zeros((nbp,) + a.shape[2:], F32)
    gla0, ssd0, conv0, mlc0 = (zeros_like_p(a) for a in (state_gla, state_ssd, state_ssd_conv, state_mlstm_c))
    nm0 = jnp.zeros((nbp, SUBLANE, LANE), F32)
    nm_s = (jnp.zeros((depth, nbs, SUBLANE, LANE), F32).at[:, :, :ML_HEADS, :].set(state_mlstm_n)
            .at[:, :, ML_HEADS, :ML_HEADS].set(state_mlstm_m))

    out = {k: [] for k in ('rows_p', 'rows_s', 'win_p', 'win_s', 'gla_p', 'gla_s', 'ssd_p', 'ssd_s', 'conv_p',
                           'conv_s', 'mlc_p', 'mlc_s', 'mln_p', 'mln_s', 'mlm_p', 'mlm_s')}
    for l in range(depth):
        hp = _norm_mod_prompt(xp, r3(norm_mix_g), modp, l, 0, 1, seq)
        hs = _norm_mod_sample(xs, r3(norm_mix_g), mods, l, 0, 1)
        zp, zs = _proj_in(hp, hs, w_in, l, 2048)
        zgp, zgs = _proj_small(hp, hs, w_in, l, 2048)

        o_gla_p, gla_p = _gla_call(zp, zgp, wa_pad, r3(gla_b_a), r3(gla_norm_g), gla0, l, nbp, seq, CHUNK,
                                   CHUNK, BF16)
        o_gla_s, gla_s = _gla_call(zs, zgs, wa_pad, r3(gla_b_a), r3(gla_norm_g), state_gla[l], l, nbs, SROWS,
                                   SROWS, 1, F32)
        ssd_args = (ssd_conv_w, r3(ssd_conv_b), dtb, alog, dvec, r3(ssd_norm_g))
        o_ssd_p, ssd_p, conv_p = _ssd_call(zp, zgp, xbc_idx, zg_idx, *ssd_args, ssd0, conv0, l, nbp, seq,
                                           CHUNK, CHUNK, dt_lane, BF16)
        o_ssd_s, ssd_s, conv_s = _ssd_call(zs, zgs, xbc_idx, zg_idx, *ssd_args, state_ssd[l],
                                           state_ssd_conv[l], l, nbs, SROWS, SROWS, 1, dt_lane, F32)
        o_ml_p, mlc_p, mlnm_p = _mlstm_call(zp, zgp, mlqk_idx, mlvo_idx, bif, r3(ml_norm_g), mlc0, nm0, l,
                                            nbp, seq, CHUNK, CHUNK, i_lane, f_lane, BF16)
        o_ml_s, mlc_s, mlnm_s = _mlstm_call(zs, zgs, mlqk_idx, mlvo_idx, bif, r3(ml_norm_g), state_mlstm_c[l],
                                            nm_s[l], l, nbs, SROWS, SROWS, 1, i_lane, f_lane, F32)
        o_nsa_p = _nsa_prompt_call(zp, zgp, q_idx, kv_idx, nbp, seq, g_lane)
        kvc = _nsa_means_call(cache_nsa_kv, page_table, l)
        ocmp, ids = _nsa_score_call(zs, q_idx, kvc, past_len)
        ids2 = ids[:, :NSA_KV, :SEL_TOPK].reshape(nbs, NSA_KV * SEL_TOPK)
        o_nsa_s = _nsa_gather_call(zs, zgs, q_idx, kv_idx, ocmp, win4, cache_nsa_kv, page_table, ids2, l,
                                   past_len, g_lane)

        mp, ms = _merge((o_gla_p, o_nsa_p, o_ssd_p, o_ml_p), (o_gla_s, o_nsa_s, o_ssd_s, o_ml_s), zp, zs, w_branch,
                        l, dst['merge'], 1024, 512)
        xp, xs = _mm_resid(mp, ms, w_out, xp, xs, modp, mods, l, 2, seq, 1024, 512, "proj_out")
        hp = _norm_mod_prompt(xp, r3(norm_ffn_g), modp, l, 3, 4, seq)
        hs = _norm_mod_sample(xs, r3(norm_ffn_g), mods, l, 3, 4)
        ap, a_s = _mm_swiglu(hp, hs, ffn_w_gate, ffn_w_up, l, 1024, 512)
        xp, xs = _mm_resid(ap, a_s, ffn_w_down, xp, xs, modp, mods, l, 5, seq, 512, 512, "ffn_down")

        zp3 = zp.reshape(nbp, seq, n_packed)
        zs3 = zs.reshape(nbs, SROWS, n_packed)[:, :1]
        out['rows_p'].append(zp3[:, :, kv_off:kv_off + 4 * gw].reshape(nbp, seq, 4, NSA_KV, dh))
        out['rows_s'].append(zs3[:, :, kv_off:kv_off + 4 * gw].reshape(nbs, 1, 4, NSA_KV, dh))
        keep = min(WINDOW, seq)
        out['win_p'].append(zp3[:, seq - keep:, kv_off + 4 * gw:kv_off + 6 * gw].reshape(nbp, keep, 2, NSA_KV, dh))
        win_all = jnp.concatenate([cache_nsa_win[l], zs3[:, :, kv_off + 4 * gw:kv_off + 6 * gw]
                                   .reshape(nbs, 1, 2, NSA_KV, dh)], axis=1)
        out['win_s'].append(win_all[:, win_all.shape[1] - min(WINDOW, win_all.shape[1]):])
        for key, val in (('gla_p', gla_p), ('gla_s', gla_s), ('ssd_p', ssd_p), ('ssd_s', ssd_s), ('conv_p', conv_p),
                         ('conv_s', conv_s), ('mlc_p', mlc_p), ('mlc_s', mlc_s)):
            out[key].append(val)
        for tag, nm in (('p', mlnm_p), ('s', mlnm_s)):
            out['mln_' + tag].append(nm[:, :ML_HEADS, :])
            out['mlm_' + tag].append(nm[:, ML_HEADS, :ML_HEADS])

    y_prompt = _final_norm(xp, final_norm_g).reshape(nbp, seq, d)
    y_sample = _final_norm(xs, final_norm_g).reshape(nbs, SROWS, d)[:, :1]
    st = {k: jnp.stack(v) for k, v in out.items()}
    return (y_prompt, y_sample, st['rows_p'], st['rows_s'], st['win_p'], st['win_s'], st['gla_p'], st['gla_s'],
            st['ssd_p'], st['ssd_s'], st['conv_p'], st['conv_s'], st['mlc_p'], st['mlc_s'], st['mln_p'], st['mln_s'],
            st['mlm_p'], st['mlm_s'])
```

```python
import functools
import math

import jax
import jax.numpy as jnp
from jax import lax
from jax.experimental import pallas as pl
from jax.experimental.pallas import tpu as pltpu

F32 = jnp.float32
BF16 = jnp.bfloat16

PAGE_SIZE = 128
GLA_HEADS = 4
GLA_RANK = 16
GLA_TAU = 16.0
NSA_HEADS = 4
NSA_KV = 2
NSA_R = NSA_HEADS // NSA_KV
CMP_BLOCK = 32
SEL_BLOCK = 64
SEL_TOPK = 16
WINDOW = 512
FORCE_SCORE = 1e4
NEG = -1e30
SSD_HEADS = 8
SSD_GROUPS = 2
SSD_N = 128
SSD_CONV = 4
ML_HEADS = 4
CHUNK = 128
EPS = 1e-6

LANE = 128
SUBLANE = 8
VMEM_LIMIT = 56 * 1024 * 1024

SROWS = SUBLANE


def _cparams(sem):
    return pltpu.CompilerParams(dimension_semantics=sem, vmem_limit_bytes=VMEM_LIMIT)


def _layout(d_model):
    bw = d_model // 4
    gdk = (bw // GLA_HEADS) // 2
    conv_ch = bw + 2 * SSD_GROUPS * SSD_N
    splits = (('gla_q', GLA_HEADS * gdk), ('gla_k', GLA_HEADS * gdk), ('gla_v', bw), ('gla_r', bw),
              ('gla_a', GLA_RANK), ('nsa_q', bw), ('nsa_kv', 6 * NSA_KV * (bw // NSA_HEADS)),
              ('nsa_g', 3 * NSA_HEADS), ('ssd_z', bw), ('ssd_xbc', conv_ch), ('ssd_dt', SSD_HEADS),
              ('ml_q', bw), ('ml_k', bw), ('ml_v', bw), ('ml_o', bw), ('ml_i', ML_HEADS), ('ml_f', ML_HEADS),
              ('merge', 4 * d_model))
    src = {}
    off = 0
    for name, w in splits:
        src[name] = (off, w)
        off += w
    return src, off


_Z_ORDER = ('gla_q', 'gla_k', 'gla_v', 'gla_r', 'nsa_kv', 'ml_q', 'ml_k', 'ml_v', 'ml_o', 'ssd_xbc',
            'nsa_q', 'ssd_z', 'merge')
_Z_SMALL = ('gla_a', 'nsa_g', 'ssd_dt', 'ml_i', 'ml_f')


Z_TILE = 512


def _packed_offsets(d_model):
    src, d_in = _layout(d_model)
    dst = {}
    off = 0
    for name in _Z_ORDER:
        dst[name] = off
        off += src[name][1]
    lane = 0
    for name in _Z_SMALL:
        dst[name] = lane
        lane += src[name][1]
    assert lane <= LANE
    return src, dst, off, d_in


def _pack_w_in(w_in):
    depth, d_model, _ = w_in.shape
    src, dst, n_packed, d_in = _packed_offsets(d_model)
    assert w_in.shape[2] == d_in
    cols = lambda names: [w_in[:, :, src[n][0]:src[n][0] + src[n][1]] for n in names]
    mix = jnp.concatenate(cols(_Z_ORDER[:-1]), axis=2)
    gate = cols(_Z_ORDER[-1:])[0]
    used = sum(src[n][1] for n in _Z_SMALL)
    small = jnp.concatenate(cols(_Z_SMALL) + [jnp.zeros((depth, d_model, LANE - used), w_in.dtype)], axis=2)
    return tuple(jnp.swapaxes(w, 1, 2).astype(BF16) for w in (mix, gate, small))


def _sigmoid(x):
    return 1.0 / (1.0 + jnp.exp(-x))


def _silu(x):
    return x * _sigmoid(x)


def _log_sigmoid(x):
    return jnp.minimum(x, 0.0) - jnp.log1p(jnp.exp(-jnp.abs(x)))


def _softplus(x):
    return jnp.maximum(x, 0.0) + jnp.log1p(jnp.exp(-jnp.abs(x)))


def _dot(a, b):
    return jnp.dot(a, b, preferred_element_type=F32)


def _dot_nt(a, b):
    return lax.dot_general(a, b, (((1,), (1,)), ((), ())), preferred_element_type=F32)


def _dot_tn(a, b):
    return lax.dot_general(a, b, (((0,), (0,)), ((), ())), preferred_element_type=F32)


def _bf(x):
    return x.astype(BF16)


def _split2(x):
    h = x.astype(BF16)
    return h, (x - h.astype(F32)).astype(BF16)


def _causal(c):
    r = lax.broadcasted_iota(jnp.int32, (c, c), 0)
    s = lax.broadcasted_iota(jnp.int32, (c, c), 1)
    return r >= s


def _cumsum_rows(x):
    n = x.shape[0]
    row = lax.broadcasted_iota(jnp.int32, (n, 1), 0)
    step = 1
    while step < n:
        x = x + jnp.where(row >= step, pltpu.roll(x, step, 0), 0.0)
        step *= 2
    return x


def _rms(x, g):
    return x * lax.rsqrt(jnp.mean(x * x, axis=-1, keepdims=True) + EPS) * g


def _softmax_rows(s):
    m = jnp.max(s, axis=-1, keepdims=True)
    e = jnp.exp(s - m)
    return e / jnp.sum(e, axis=-1, keepdims=True)


def _alibi_slopes():
    return [2.0 ** (-8.0 * (h + 1) / NSA_HEADS) for h in range(NSA_HEADS)]


def _ada_kernel(c_ref, w_ref, b_ref, o_ref):
    a = _bf(_silu(c_ref[...]))
    o_ref[...] = _dot(a, _bf(w_ref[...])) + b_ref[...]


def _ada(c_all, ada_w, ada_b):
    depth, d, n = ada_w.shape
    rows = c_all.shape[0]
    tn = 1024
    return pl.pallas_call(
        _ada_kernel, grid=(depth, n // tn),
        in_specs=[pl.BlockSpec((rows, d), lambda l, j: (0, 0)),
                  pl.BlockSpec((None, d, tn), lambda l, j: (l, 0, j)),
                  pl.BlockSpec((None, 1, tn), lambda l, j: (l, 0, j))],
        out_specs=pl.BlockSpec((None, rows, tn), lambda l, j: (l, 0, j)),
        out_shape=jax.ShapeDtypeStruct((depth, rows, n), F32),
        compiler_params=_cparams(("arbitrary", "arbitrary")), name="ada",
    )(c_all, ada_w, ada_b.reshape(depth, 1, n))


def _norm_mod_kernel(x_ref, g_ref, sc_ref, sh_ref, o_ref):
    y = _rms(x_ref[...], g_ref[...])
    o_ref[...] = (y * (1.0 + sc_ref[...]) + sh_ref[...]).astype(o_ref.dtype)


def _norm_kernel(x_ref, g_ref, o_ref):
    o_ref[...] = _rms(x_ref[...], g_ref[...]).astype(o_ref.dtype)


def _norm_mod_prompt(x, g, modp, l, j_shift, j_scale, seq):
    m, d = x.shape
    tm = 512
    per = seq // tm
    return pl.pallas_call(
        _norm_mod_kernel, grid=(m // tm,),
        in_specs=[pl.BlockSpec((tm, d), lambda i: (i, 0)),
                  pl.BlockSpec((None, 1, d), lambda i: (l, 0, 0)),
                  pl.BlockSpec((None, None, 1, d), lambda i: (l, i // per, 0, j_scale)),
                  pl.BlockSpec((None, None, 1, d), lambda i: (l, i // per, 0, j_shift))],
        out_specs=pl.BlockSpec((tm, d), lambda i: (i, 0)),
        out_shape=jax.ShapeDtypeStruct((m, d), BF16),
        compiler_params=_cparams(("arbitrary",)), name="norm_mod_prompt",
    )(x, g, modp, modp)


def _norm_mod_sample(x, g, mods, l, j_shift, j_scale):
    m, d = x.shape
    return pl.pallas_call(
        _norm_mod_kernel, grid=(1,),
        in_specs=[pl.BlockSpec((m, d), lambda i: (0, 0)),
                  pl.BlockSpec((None, 1, d), lambda i: (l, 0, 0)),
                  pl.BlockSpec((None, m, d), lambda i: (l, 0, j_scale)),
                  pl.BlockSpec((None, m, d), lambda i: (l, 0, j_shift))],
        out_specs=pl.BlockSpec((m, d), lambda i: (0, 0)),
        out_shape=jax.ShapeDtypeStruct((m, d), BF16),
        compiler_params=_cparams(("arbitrary",)), name="norm_mod_sample",
    )(x, g, mods, mods)


def _final_norm(x, g):
    m, d = x.shape
    tm = min(m, 512)
    return pl.pallas_call(
        _norm_kernel, grid=(m // tm,),
        in_specs=[pl.BlockSpec((tm, d), lambda i: (i, 0)), pl.BlockSpec((1, d), lambda i: (0, 0))],
        out_specs=pl.BlockSpec((tm, d), lambda i: (i, 0)),
        out_shape=jax.ShapeDtypeStruct((m, d), F32),
        compiler_params=_cparams(("arbitrary",)), name="final_norm",
    )(x, g.reshape(1, d))


def _proj_nt_kernel(a_ref, as_ref, wt_ref, o_ref, os_ref):
    w = wt_ref[...]

    @pl.when(pl.program_id(1) == 0)
    def _():
        os_ref[...] = _dot_nt(as_ref[...], w).astype(os_ref.dtype)

    o_ref[...] = _dot_nt(a_ref[...], w).astype(o_ref.dtype)


def _proj_nt(a, a_s, w_t, l, tm, tn, name, out_dtype=F32):
    m, k = a.shape
    ms = a_s.shape[0]
    n = w_t.shape[1]
    return pl.pallas_call(
        _proj_nt_kernel, grid=(n // tn, m // tm),
        in_specs=[pl.BlockSpec((tm, k), lambda j, i: (i, 0)),
                  pl.BlockSpec((ms, k), lambda j, i: (0, 0)),
                  pl.BlockSpec((None, tn, k), lambda j, i: (l, j, 0))],
        out_specs=[pl.BlockSpec((tm, tn), lambda j, i: (i, j)),
                   pl.BlockSpec((ms, tn), lambda j, i: (0, j))],
        out_shape=[jax.ShapeDtypeStruct((m, n), out_dtype), jax.ShapeDtypeStruct((ms, n), out_dtype)],
        compiler_params=_cparams(("arbitrary", "arbitrary")), name=name,
    )(a, a_s, w_t)


def _mm_resid_kernel(a_ref, as_ref, w_ref, x_ref, xs_ref, gt_ref, gts_ref, o_ref, os_ref, wb_ref):
    @pl.when(pl.program_id(1) == 0)
    def _():
        wb_ref[...] = _bf(w_ref[...])
        os_ref[...] = xs_ref[...] + gts_ref[...] * _dot(as_ref[...], wb_ref[...])

    o_ref[...] = x_ref[...] + gt_ref[...] * _dot(a_ref[...], wb_ref[...])


def _mm_resid(a, a_s, w, x, x_s, modp, mods, l, j_gate, seq, tm, tn, name):
    m, k = a.shape
    ms = a_s.shape[0]
    n = w.shape[2]
    per = seq // tm
    nj = n // tn
    return pl.pallas_call(
        _mm_resid_kernel, grid=(nj, m // tm),
        in_specs=[pl.BlockSpec((tm, k), lambda j, i: (i, 0)),
                  pl.BlockSpec((ms, k), lambda j, i: (0, 0)),
                  pl.BlockSpec((None, k, tn), lambda j, i: (l, 0, j)),
                  pl.BlockSpec((tm, tn), lambda j, i: (i, j)),
                  pl.BlockSpec((ms, tn), lambda j, i: (0, j)),
                  pl.BlockSpec((None, None, 1, tn), lambda j, i: (l, i // per, 0, j_gate * nj + j)),
                  pl.BlockSpec((None, ms, tn), lambda j, i: (l, 0, j_gate * nj + j))],
        out_specs=[pl.BlockSpec((tm, tn), lambda j, i: (i, j)),
                   pl.BlockSpec((ms, tn), lambda j, i: (0, j))],
        out_shape=[jax.ShapeDtypeStruct((m, n), F32), jax.ShapeDtypeStruct((ms, n), F32)],
        scratch_shapes=[pltpu.VMEM((k, tn), BF16)],
        compiler_params=_cparams(("arbitrary", "arbitrary")), name=name,
    )(a, a_s, w, x, x_s, modp, mods)


def _mm_swiglu_kernel(a_ref, as_ref, wg_ref, wu_ref, o_ref, os_ref, wgb_ref, wub_ref):
    @pl.when(pl.program_id(1) == 0)
    def _():
        wgb_ref[...] = _bf(wg_ref[...])
        wub_ref[...] = _bf(wu_ref[...])
        a_s = as_ref[...]
        os_ref[...] = (_silu(_dot(a_s, wgb_ref[...])) * _dot(a_s, wub_ref[...])).astype(os_ref.dtype)

    a = a_ref[...]
    o_ref[...] = (_silu(_dot(a, wgb_ref[...])) * _dot(a, wub_ref[...])).astype(o_ref.dtype)


def _mm_swiglu(a, a_s, wg, wu, l, tm, tn):
    m, k = a.shape
    ms = a_s.shape[0]
    n = wg.shape[2]
    return pl.pallas_call(
        _mm_swiglu_kernel, grid=(n // tn, m // tm),
        in_specs=[pl.BlockSpec((tm, k), lambda j, i: (i, 0)),
                  pl.BlockSpec((ms, k), lambda j, i: (0, 0)),
                  pl.BlockSpec((None, k, tn), lambda j, i: (l, 0, j)),
                  pl.BlockSpec((None, k, tn), lambda j, i: (l, 0, j))],
        out_specs=[pl.BlockSpec((tm, tn), lambda j, i: (i, j)),
                   pl.BlockSpec((ms, tn), lambda j, i: (0, j))],
        out_shape=[jax.ShapeDtypeStruct((m, n), BF16), jax.ShapeDtypeStruct((ms, n), BF16)],
        scratch_shapes=[pltpu.VMEM((k, tn), BF16), pltpu.VMEM((k, tn), BF16)],
        compiler_params=_cparams(("arbitrary", "arbitrary")), name="ffn_gate_up",
    )(a, a_s, wg, wu)


def _merge_kernel(*refs):
    o_p = refs[0:4]
    o_s = refs[4:8]
    g_p = refs[8:12]
    g_s = refs[12:16]
    w_ref, out_ref, outs_ref, wb_ref = refs[16:20]

    @pl.when(pl.program_id(1) == 0)
    def _():
        wb_ref[...] = _bf(w_ref[...])
        acc = None
        for b in range(4):
            t = _sigmoid(g_s[b][...].astype(F32)) * _dot(_bf(o_s[b][...]), wb_ref[b])
            acc = t if acc is None else acc + t
        outs_ref[...] = acc.astype(outs_ref.dtype)

    acc = None
    for b in range(4):
        t = _sigmoid(g_p[b][...].astype(F32)) * _dot(_bf(o_p[b][...]), wb_ref[b])
        acc = t if acc is None else acc + t
    out_ref[...] = acc.astype(out_ref.dtype)


def _merge(o_p, o_s, z, z_s, w_branch, l, tm, tn):
    m = z.shape[0]
    ms = z_s.shape[0]
    bw = w_branch.shape[2]
    d = w_branch.shape[3]
    in_specs = ([pl.BlockSpec((tm, bw), lambda j, i: (i, 0)) for _ in range(4)]
                + [pl.BlockSpec((ms, bw), lambda j, i: (0, 0)) for _ in range(4)]
                + [pl.BlockSpec((tm, tn), lambda j, i, b=b: (i, b * (d // tn) + j)) for b in range(4)]
                + [pl.BlockSpec((ms, tn), lambda j, i, b=b: (0, b * (d // tn) + j)) for b in range(4)]
                + [pl.BlockSpec((None, 4, bw, tn), lambda j, i: (l, 0, 0, j))])
    return pl.pallas_call(
        _merge_kernel, grid=(d // tn, m // tm),
        in_specs=in_specs,
        out_specs=[pl.BlockSpec((tm, tn), lambda j, i: (i, j)), pl.BlockSpec((ms, tn), lambda j, i: (0, j))],
        out_shape=[jax.ShapeDtypeStruct((m, d), BF16), jax.ShapeDtypeStruct((ms, d), BF16)],
        scratch_shapes=[pltpu.VMEM((4, bw, tn), BF16)],
        compiler_params=_cparams(("arbitrary", "arbitrary")), name="merge",
    )(*o_p, *o_s, z, z, z, z, z_s, z_s, z_s, z_s, w_branch)


def _live_rows(c_rows, n_live):
    return lax.broadcasted_iota(jnp.int32, (c_rows, 1), 0) < n_live


def _gla_kernel(z_ref, small_ref, wa_ref, ba_ref, gn_ref, s0_ref, o_ref, sout_ref, st_scr, *, NB, C, L, H, DK, DV):
    c = pl.program_id(0)

    @pl.when(c == 0)
    def _():
        st_scr[...] = s0_ref[...]

    causal = _causal(C)
    wa = _bf(wa_ref[...])
    ba = ba_ref[...]
    gn = gn_ref[...]
    mid = C // 2

    def one_sequence(bi):
        z = z_ref[bi]
        q = z[:, 0:H * DK] * (DK ** -0.5)
        k = z[:, H * DK:2 * H * DK]
        v = z[:, 2 * H * DK:2 * H * DK + H * DV]
        r = z[:, 2 * H * DK + H * DV:]
        ga = _dot(_bf(small_ref[bi]), wa) + ba
        glog = _log_sigmoid(ga) * (1.0 / GLA_TAU)
        if L < C:
            live = _live_rows(C, L)
            glog = jnp.where(live, glog, 0.0)
            k = jnp.where(live, k, 0.0)
        b = _cumsum_rows(glog)
        bl = b[C - 1:C, :]
        bm = b[mid - 1:mid, :]
        qi = _bf(q * jnp.exp(b - bm))
        ki = _bf(k * jnp.exp(bm - b))
        qs = q * jnp.exp(b)
        kd = _bf(k * jnp.exp(bl - b))
        ebl_col = jnp.exp(b.T[:, C - 1:C])
        outs = []
        for h in range(H):
            ks = slice(h * DK, (h + 1) * DK)
            vs = slice(h * DV, (h + 1) * DV)
            vh = v[:, vs]
            a = jnp.where(causal, _dot_nt(qi[:, ks], ki[:, ks]), 0.0)
            st = st_scr[bi, h]
            o = _dot(_bf(jnp.concatenate([a, qs[:, ks]], axis=1)), _bf(jnp.concatenate([vh, st], axis=0)))
            st_scr[bi, h] = ebl_col[ks, :] * st + _dot_tn(kd[:, ks], _bf(vh))
            outs.append(_rms(o, gn[:, vs]))
        o_ref[bi] = (jnp.concatenate(outs, axis=1) * _silu(r)).astype(o_ref.dtype)

    for bi in range(NB):
        one_sequence(bi)

    @pl.when(c == pl.num_programs(0) - 1)
    def _():
        sout_ref[...] = st_scr[...]


def _gla_call(z, zsm, wa_pad, ba, gn, s0, l, nb, seq, C, L, out_dtype):
    H, DK, DV = s0.shape[1:]
    nch = seq // C
    width = 2 * H * DK + 2 * H * DV
    z3 = z.reshape(nb, seq, z.shape[1])
    zsm3 = zsm.reshape(nb, seq, LANE)
    kern = functools.partial(_gla_kernel, NB=nb, C=C, L=L, H=H, DK=DK, DV=DV)
    o, s = pl.pallas_call(
        kern, grid=(nch,),
        in_specs=[pl.BlockSpec((nb, C, width), lambda c: (0, c, 0)),
                  pl.BlockSpec((nb, C, LANE), lambda c: (0, c, 0)),
                  pl.BlockSpec((None, LANE, H * DK), lambda c: (l, 0, 0)),
                  pl.BlockSpec((None, 1, H * DK), lambda c: (l, 0, 0)),
                  pl.BlockSpec((None, 1, H * DV), lambda c: (l, 0, 0)),
                  pl.BlockSpec((nb, H, DK, DV), lambda c: (0, 0, 0, 0))],
        out_specs=[pl.BlockSpec((nb, C, H * DV), lambda c: (0, c, 0)),
                   pl.BlockSpec((nb, H, DK, DV), lambda c: (0, 0, 0, 0))],
        out_shape=[jax.ShapeDtypeStruct((nb, seq, H * DV), out_dtype),
                   jax.ShapeDtypeStruct((nb, H, DK, DV), F32)],
        scratch_shapes=[pltpu.VMEM((nb, H, DK, DV), F32)],
        compiler_params=_cparams(("arbitrary",)), name="gla",
    )(z3, zsm3, wa_pad, ba, gn, s0)
    return o.reshape(nb * seq, H * DV), s


def _ssd_kernel(xbc_ref, zg_ref, small_ref, cw_ref, cb_ref, dtb_ref, alog_ref, dvec_ref, gn_ref, s0_ref, conv0_ref,
                o_ref, sout_ref, convout_ref, s_scr, ext_scr, *, NB, C, L, H, G, N, P, dt_lane):
    c = pl.program_id(0)
    kh = SSD_CONV - 1
    base = SUBLANE - kh

    @pl.when(c == 0)
    def _():
        s_scr[...] = s0_ref[...]
        ext_scr[:, base:SUBLANE, :] = conv0_ref[...]

    cw = cw_ref[...]
    cb = cb_ref[...]
    dtb = dtb_ref[...]
    neg_a = -jnp.exp(alog_ref[...])
    dvec = dvec_ref[...]
    gn = gn_ref[...]
    causal = _causal(C)
    bw = H * P
    hpg = H // G

    def one_sequence(bi):
        ext_scr[bi, SUBLANE:SUBLANE + C, :] = xbc_ref[bi]
        conv = cb
        for j in range(SSD_CONV):
            conv = conv + ext_scr[bi, pl.ds(base + j, C), :] * cw[j:j + 1, :]
        hist = ext_scr[bi, pl.ds(base + L, kh), :]
        ext_scr[bi, base:SUBLANE, :] = hist
        xc = _silu(conv)
        sx = xc[:, 0:bw]
        bm = xc[:, bw:bw + G * N]
        cm = xc[:, bw + G * N:]
        dt = _softplus(small_ref[bi] + dtb)
        logg = dt * neg_a
        if L < C:
            live = _live_rows(C, L)
            logg = jnp.where(live, logg, 0.0)
            bm = jnp.where(live, bm, 0.0)
        b = _cumsum_rows(logg)
        b_t = b.T
        ys = []
        for g in range(G):
            cg = cm[:, g * N:(g + 1) * N]
            bg = bm[:, g * N:(g + 1) * N]
            gmat = _dot_nt(_bf(cg), _bf(bg))
            for hh in range(hpg):
                h = g * hpg + hh
                ln = dt_lane + h
                bc = b[:, ln:ln + 1]
                br = b_t[ln:ln + 1, :]
                dec = jnp.exp(jnp.where(causal, bc - br, -jnp.inf))
                xdt = sx[:, h * P:(h + 1) * P] * dt[:, ln:ln + 1]
                st = s_scr[bi, h]
                y = _dot(_bf(jnp.concatenate([gmat * dec, cg * jnp.exp(bc)], axis=1)),
                         _bf(jnp.concatenate([xdt, st], axis=0)))
                bl = bc[C - 1:C, :]
                s_scr[bi, h] = jnp.exp(bl) * st + _dot_tn(_bf(bg * jnp.exp(bl - bc)), _bf(xdt))
                ys.append(y)
        y_all = jnp.concatenate(ys, axis=1) + dvec * sx
        o_ref[bi] = _rms(y_all * _silu(zg_ref[bi]), gn).astype(o_ref.dtype)

    for bi in range(NB):
        one_sequence(bi)

    @pl.when(c == pl.num_programs(0) - 1)
    def _():
        sout_ref[...] = s_scr[...]
        convout_ref[...] = ext_scr[:, base:SUBLANE, :]


def _ssd_call(z, zsm, xbc_idx, zg_idx, cw, cb, dtb, alog, dvec, gn, s0, conv0, l, nb, seq, C, L, dt_lane,
              out_dtype):
    H, N, P = s0.shape[1:]
    G = SSD_GROUPS
    ch = conv0.shape[2]
    bw = H * P
    nch = seq // C
    z3 = z.reshape(nb, seq, z.shape[1])
    zsm3 = zsm.reshape(nb, seq, LANE)
    kern = functools.partial(_ssd_kernel, NB=nb, C=C, L=L, H=H, G=G, N=N, P=P, dt_lane=dt_lane)
    vec = lambda w: pl.BlockSpec((None, 1, w), lambda c: (l, 0, 0))
    o, s, cv = pl.pallas_call(
        kern, grid=(nch,),
        in_specs=[pl.BlockSpec((nb, C, ch), lambda c: (0, c, xbc_idx)),
                  pl.BlockSpec((nb, C, bw), lambda c: (0, c, zg_idx)),
                  pl.BlockSpec((nb, C, LANE), lambda c: (0, c, 0)),
                  pl.BlockSpec((None, SSD_CONV, ch), lambda c: (l, 0, 0)),
                  vec(ch), vec(LANE), vec(LANE), vec(bw), vec(bw),
                  pl.BlockSpec((nb, H, N, P), lambda c: (0, 0, 0, 0)),
                  pl.BlockSpec((nb, SSD_CONV - 1, ch), lambda c: (0, 0, 0))],
        out_specs=[pl.BlockSpec((nb, C, bw), lambda c: (0, c, 0)),
                   pl.BlockSpec((nb, H, N, P), lambda c: (0, 0, 0, 0)),
                   pl.BlockSpec((nb, SSD_CONV - 1, ch), lambda c: (0, 0, 0))],
        out_shape=[jax.ShapeDtypeStruct((nb, seq, bw), out_dtype),
                   jax.ShapeDtypeStruct((nb, H, N, P), F32),
                   jax.ShapeDtypeStruct((nb, SSD_CONV - 1, ch), F32)],
        scratch_shapes=[pltpu.VMEM((nb, H, N, P), F32), pltpu.VMEM((nb, SUBLANE + C, ch), F32)],
        compiler_params=_cparams(("arbitrary",)), name="ssd",
    )(z3, z3, zsm3, cw, cb, dtb, alog, dvec, gn, s0, conv0)
    return o.reshape(nb * seq, bw), s, cv


def _mlstm_kernel(qk_ref, vo_ref, small_ref, bif_ref, gn_ref, c0_ref, nm0_ref, o_ref, cout_ref, nmout_ref,
                  c_scr, nm_scr, *, NB, C, L, H, DH, i_lane, f_lane):
    c = pl.program_id(0)

    @pl.when(c == 0)
    def _():
        c_scr[...] = c0_ref[...]
        nm_scr[...] = nm0_ref[...]

    causal = _causal(C)
    lane = lax.broadcasted_iota(jnp.int32, (1, LANE), 1)
    bif = bif_ref[...]
    gn = gn_ref[...]

    def one_sequence(bi):
        qk = qk_ref[bi]
        vo = vo_ref[bi]
        q = qk[:, 0:H * DH]
        k = qk[:, H * DH:] * (DH ** -0.5)
        v = vo[:, 0:H * DH]
        og = vo[:, H * DH:]
        pre = small_ref[bi] + bif
        logf = _log_sigmoid(pre)
        ipre = pre
        if L < C:
            live = _live_rows(C, L)
            logf = jnp.where(live, logf, 0.0)
            ipre = jnp.where(live, pre, NEG)
        b = _cumsum_rows(logf)
        b_t = b.T
        i_t = ipre.T
        nm = nm_scr[bi]
        m_row = nm[H:H + 1, :]
        outs = []
        for h in range(H):
            sl = slice(h * DH, (h + 1) * DH)
            bc = b[:, f_lane + h:f_lane + h + 1]
            br = b_t[f_lane + h:f_lane + h + 1, :]
            ic = ipre[:, i_lane + h:i_lane + h + 1]
            ir = i_t[i_lane + h:i_lane + h + 1, :]
            m_prev = nm[H:H + 1, h:h + 1]
            logd = jnp.where(causal, bc - br + ir, -jnp.inf)
            gg = bc + m_prev
            m = jnp.maximum(gg, jnp.max(logd, axis=1, keepdims=True))
            dm = jnp.exp(logd - m)
            wi = jnp.exp(gg - m)
            qf = q[:, sl]
            kh = k[:, sl]
            vf = v[:, sl]
            vh = _bf(vf)
            qkm = _dot_nt(_bf(qf), _bf(kh)) * dm
            cst = c_scr[bi, h]
            nrow = nm[h:h + 1, :]
            num = _dot(_bf(jnp.concatenate([qkm, wi * qf], axis=1)), _bf(jnp.concatenate([vf, cst], axis=0)))
            den = jnp.sum(qkm, axis=1, keepdims=True) + wi * jnp.sum(qf * nrow, axis=1, keepdims=True)
            hid = num / jnp.maximum(jnp.abs(den), jnp.exp(-m))
            m_new = m[C - 1:C, :]
            bl = bc[C - 1:C, :]
            ws = jnp.exp(bl - bc + ic - m_new)
            wc = jnp.exp(bl + m_prev - m_new)
            wk = ws * kh
            c_scr[bi, h] = wc * cst + _dot_tn(_bf(wk), vh)
            nm_scr[bi, h:h + 1, :] = wc * nrow + jnp.sum(wk, axis=0, keepdims=True)
            m_row = jnp.where(lane == h, m_new, m_row)
            outs.append(_rms(hid, gn[:, sl]))
        nm_scr[bi, H:H + 1, :] = m_row
        o_ref[bi] = (jnp.concatenate(outs, axis=1) * _sigmoid(og)).astype(o_ref.dtype)

    for bi in range(NB):
        one_sequence(bi)

    @pl.when(c == pl.num_programs(0) - 1)
    def _():
        cout_ref[...] = c_scr[...]
        nmout_ref[...] = nm_scr[...]


def _mlstm_call(z, zsm, qk_idx, vo_idx, bif, gn, c0, nm0, l, nb, seq, C, L, i_lane, f_lane, out_dtype):
    H, DH = c0.shape[1:3]
    nch = seq // C
    z3 = z.reshape(nb, seq, z.shape[1])
    zsm3 = zsm.reshape(nb, seq, LANE)
    kern = functools.partial(_mlstm_kernel, NB=nb, C=C, L=L, H=H, DH=DH, i_lane=i_lane, f_lane=f_lane)
    o, cc, nm = pl.pallas_call(
        kern, grid=(nch,),
        in_specs=[pl.BlockSpec((nb, C, 2 * H * DH), lambda c: (0, c, qk_idx)),
                  pl.BlockSpec((nb, C, 2 * H * DH), lambda c: (0, c, vo_idx)),
                  pl.BlockSpec((nb, C, LANE), lambda c: (0, c, 0)),
                  pl.BlockSpec((None, 1, LANE), lambda c: (l, 0, 0)),
                  pl.BlockSpec((None, 1, H * DH), lambda c: (l, 0, 0)),
                  pl.BlockSpec((nb, H, DH, DH), lambda c: (0, 0, 0, 0)),
                  pl.BlockSpec((nb, SUBLANE, LANE), lambda c: (0, 0, 0))],
        out_specs=[pl.BlockSpec((nb, C, H * DH), lambda c: (0, c, 0)),
                   pl.BlockSpec((nb, H, DH, DH), lambda c: (0, 0, 0, 0)),
                   pl.BlockSpec((nb, SUBLANE, LANE), lambda c: (0, 0, 0))],
        out_shape=[jax.ShapeDtypeStruct((nb, seq, H * DH), out_dtype),
                   jax.ShapeDtypeStruct((nb, H, DH, DH), F32),
                   jax.ShapeDtypeStruct((nb, SUBLANE, LANE), F32)],
        scratch_shapes=[pltpu.VMEM((nb, H, DH, DH), F32), pltpu.VMEM((nb, SUBLANE, LANE), F32)],
        compiler_params=_cparams(("arbitrary",)), name="mlstm",
    )(z3, z3, zsm3, bif, gn, c0, nm0)
    return o.reshape(nb * seq, H * DH), cc, nm


def _rank_select(score, n_blocks, keep):
    lane = lax.broadcasted_iota(jnp.int32, score.shape, 1)
    cnt = jnp.zeros(score.shape, jnp.int32)
    for j in range(n_blocks):
        cj = score[:, j:j + 1]
        ahead = jnp.where(cj > score, 1, jnp.where(cj == score, jnp.where(lane > j, 1, 0), 0))
        cnt = cnt + ahead
    return cnt < keep


def _nsa_prompt_kernel(q_ref, kv_ref, small_ref, o_ref, kvb_ref, cmp_ref, *, T, TQ, DH, g_lane):
    qi = pl.program_id(1)
    G, R = NSA_KV, NSA_R
    gw = G * DH
    ncb = T // CMP_BLOCK
    nsb = T // SEL_BLOCK
    ratio = SEL_BLOCK // CMP_BLOCK
    slopes = _alibi_slopes()

    @pl.when(qi == 0)
    def _():
        kvb_ref[...] = _bf(kv_ref[...])
        hi, lo = _split2(kv_ref[:, 0:2 * gw])
        jj = lax.broadcasted_iota(jnp.int32, (ncb, T), 0) * CMP_BLOCK
        tt = lax.broadcasted_iota(jnp.int32, (ncb, T), 1)
        pm = jnp.where((tt >= jj) & (tt < jj + CMP_BLOCK), 1.0 / CMP_BLOCK, 0.0).astype(BF16)
        cmp_ref[...] = _bf(_dot(pm, hi) + _dot(pm, lo))

    q = q_ref[...]
    gate = _sigmoid(small_ref[...])
    posl = qi * TQ + lax.broadcasted_iota(jnp.int32, (1, TQ), 1)
    cbs = lax.broadcasted_iota(jnp.int32, (ncb, 1), 0)
    cvalid = (cbs * CMP_BLOCK + (CMP_BLOCK - 1)) <= posl
    cdist = posl.astype(F32) - (cbs.astype(F32) * CMP_BLOCK + (CMP_BLOCK - 1) / 2.0)
    sbs = lax.broadcasted_iota(jnp.int32, (nsb, 1), 0)
    svalid = sbs * SEL_BLOCK <= posl
    forced = (sbs == 0) | ((sbs * SEL_BLOCK <= posl) & (posl < sbs * SEL_BLOCK + SEL_BLOCK))
    pi = lax.broadcasted_iota(jnp.int32, (nsb, ncb), 0) * ratio
    pj = lax.broadcasted_iota(jnp.int32, (nsb, ncb), 1)
    pool = jnp.where((pj >= pi) & (pj < pi + ratio), 1.0, 0.0).astype(BF16)
    brow = lax.broadcasted_iota(jnp.int32, (nsb, TQ), 0)
    row2 = lax.broadcasted_iota(jnp.int32, (R * TQ, 1), 0)
    rr = lax.broadcasted_iota(jnp.int32, (TQ, TQ), 0)
    cc = lax.broadcasted_iota(jnp.int32, (TQ, TQ), 1)
    lower = rr >= cc
    lower2 = jnp.concatenate([lower] * R, axis=0)
    col = lax.broadcasted_iota(jnp.int32, (1, TQ), 1)
    nfar = WINDOW // TQ

    def tile(carry, kt, q2, slope2, kcol, vcol, mask):
        m, l, acc = carry
        k0 = pl.multiple_of(kt * TQ, TQ)
        ks = kvb_ref[pl.ds(k0, TQ), kcol:kcol + DH]
        vs = kvb_ref[pl.ds(k0, TQ), vcol:vcol + DH]
        s = _dot_nt(q2, ks) + slope2 * (k0 + col).astype(F32)
        if mask is not None:
            s = jnp.where(mask, s, NEG)
        mn = jnp.maximum(m, jnp.max(s, axis=1, keepdims=True))
        p = jnp.exp(s - mn)
        alpha = jnp.exp(m - mn)
        return (mn, alpha * l + jnp.sum(p, axis=1, keepdims=True), alpha * acc + _dot(_bf(p), vs))

    def sel_mask(sel, kt):
        kpos = kt * TQ + col
        expand = _bf(jnp.where((kpos >= sbs * SEL_BLOCK) & (kpos < sbs * SEL_BLOCK + SEL_BLOCK), 1.0, 0.0))
        return _dot_tn(sel, expand) > 0.5

    init = (jnp.full((R * TQ, 1), NEG, F32), jnp.zeros((R * TQ, 1), F32), jnp.zeros((R * TQ, DH), F32))
    o_cmp, sels, q2s, slope2s, sel_carry, o_win = [], [], [], [], [], []
    for g in range(G):
        kc = cmp_ref[:, g * DH:(g + 1) * DH]
        vc = cmp_ref[:, gw + g * DH:gw + (g + 1) * DH]
        qs = []
        imp = jnp.zeros((nsb, TQ), F32)
        for r in range(R):
            hd = g * R + r
            qh = _bf(q[:, hd * DH:(hd + 1) * DH] * (DH ** -0.5))
            qs.append(qh)
            s = jnp.where(cvalid, _dot_nt(kc, qh) - slopes[hd] * cdist, NEG)
            e = jnp.exp(s - jnp.max(s, axis=0, keepdims=True))
            p = jnp.where(cvalid, e / jnp.sum(e, axis=0, keepdims=True), 0.0)
            o_cmp.append(_dot_tn(_bf(p), vc))
            ph, plo = _split2(p)
            imp = imp + _dot(pool, ph) + _dot(pool, plo)
        score = jnp.where(svalid, jnp.where(forced, FORCE_SCORE, imp), -1.0)
        ahead = jnp.zeros((nsb, TQ), jnp.int32)
        for j in range(nsb):
            sj = score[j:j + 1, :]
            ahead = ahead + jnp.where(sj > score, 1, jnp.where(sj == score, jnp.where(brow > j, 1, 0), 0))
        sel = _bf(jnp.where((ahead < SEL_TOPK) & (score >= 0.0), 1.0, 0.0))
        q2 = jnp.concatenate(qs, axis=0)
        slope2 = slopes[g * R]
        for r in range(1, R):
            slope2 = jnp.where(row2 >= r * TQ, slopes[g * R + r], slope2)
        sels.append(sel)
        q2s.append(q2)
        slope2s.append(slope2)
        sel_carry.append(tile(init, qi, q2, slope2, 2 * gw + g * DH, 3 * gw + g * DH,
                              jnp.concatenate([sel_mask(sel, qi) & lower] * R, axis=0)))
        kcol, vcol = 4 * gw + g * DH, 5 * gw + g * DH
        carry = tile(init, qi, q2, slope2, kcol, vcol, lower2)
        for back in range(1, nfar + 1):
            kt = jnp.maximum(qi - back, 0)
            live = qi >= back
            mask = jnp.logical_and(jnp.logical_not(lower2), live) if back == nfar else live
            carry = tile(carry, kt, q2, slope2, kcol, vcol, mask)
        o_win.append(carry[2] / carry[1])

    def sel_tiles(kt, carries):
        return tuple(tile(carries[g], kt, q2s[g], slope2s[g], 2 * gw + g * DH, 3 * gw + g * DH,
                          jnp.concatenate([sel_mask(sels[g], kt)] * R, axis=0)) for g in range(G))

    sel_carry = lax.fori_loop(0, qi, sel_tiles, tuple(sel_carry))
    outs = [None] * (G * R)
    for g in range(G):
        o_slc = sel_carry[g][2] / sel_carry[g][1]
        for r in range(R):
            hd = g * R + r
            gc = gate[:, g_lane + hd:g_lane + hd + 1]
            gs = gate[:, g_lane + NSA_HEADS + hd:g_lane + NSA_HEADS + hd + 1]
            gwn = gate[:, g_lane + 2 * NSA_HEADS + hd:g_lane + 2 * NSA_HEADS + hd + 1]
            outs[hd] = gc * o_cmp[hd] + gs * o_slc[r * TQ:(r + 1) * TQ] + gwn * o_win[g][r * TQ:(r + 1) * TQ]
    o_ref[...] = jnp.concatenate(outs, axis=1).astype(o_ref.dtype)


def _nsa_prompt_call(z, zsm, q_idx, kv_idx, nb, seq, g_lane, tq=256):
    dh = LANE
    kvw = 6 * NSA_KV * dh
    qw = NSA_HEADS * dh
    nq = seq // tq
    assert WINDOW % tq == 0 and seq % tq == 0
    kern = functools.partial(_nsa_prompt_kernel, T=seq, TQ=tq, DH=dh, g_lane=g_lane)
    return pl.pallas_call(
        kern, grid=(nb, nq),
        in_specs=[pl.BlockSpec((tq, qw), lambda b, i: (b * nq + i, q_idx)),
                  pl.BlockSpec((seq, kvw), lambda b, i: (b, kv_idx)),
                  pl.BlockSpec((tq, LANE), lambda b, i: (b * nq + i, 0))],
        out_specs=pl.BlockSpec((tq, qw), lambda b, i: (b * nq + i, 0)),
        out_shape=jax.ShapeDtypeStruct((nb * seq, qw), BF16),
        scratch_shapes=[pltpu.VMEM((seq, kvw), BF16), pltpu.VMEM((seq // CMP_BLOCK, 2 * NSA_KV * dh), BF16)],
        compiler_params=_cparams(("arbitrary", "arbitrary")), name="nsa_prompt",
    )(z, z, zsm)


def _nsa_means_kernel(pt_ref, *refs, PP, DH):
    ins = refs[:PP]
    o_ref = refs[PP]
    nper = PAGE_SIZE // CMP_BLOCK
    for i in range(PP):
        x = ins[i][...].reshape(nper, CMP_BLOCK, 2, NSA_KV, DH)
        o_ref[i * nper:(i + 1) * nper] = jnp.sum(x, axis=1) * (1.0 / CMP_BLOCK)


def _nsa_means_call(cache, page_table, l, pp=8):
    nb, n_pages = page_table.shape
    dh = cache.shape[5]
    nper = PAGE_SIZE // CMP_BLOCK
    assert cache.shape[2] == PAGE_SIZE and cache.shape[4] == NSA_KV and n_pages % pp == 0
    in_specs = [pl.BlockSpec((None, None, PAGE_SIZE, 2, NSA_KV, dh),
                             lambda b, s, pt, i=i: (l, pt[b, s * pp + i], 0, 0, 0, 0)) for i in range(pp)]
    gs = pltpu.PrefetchScalarGridSpec(
        num_scalar_prefetch=1, grid=(nb, n_pages // pp), in_specs=in_specs,
        out_specs=pl.BlockSpec((None, nper * pp, 2, NSA_KV, dh), lambda b, s, pt: (b, s, 0, 0, 0)))
    means = pl.pallas_call(
        functools.partial(_nsa_means_kernel, PP=pp, DH=dh), grid_spec=gs,
        out_shape=jax.ShapeDtypeStruct((nb, n_pages * nper, 2, NSA_KV, dh), F32),
        compiler_params=_cparams(("arbitrary", "arbitrary")), name="nsa_cmp_means",
    )(page_table, *([cache] * pp))
    return means.reshape(nb, n_pages * nper, 2 * NSA_KV * dh)


def _nsa_score_kernel(q_ref, kvc_ref, ocmp_ref, ids_ref, *, DH, qpos):
    G, R = NSA_KV, NSA_R
    gw = G * DH
    ncb = kvc_ref.shape[0]
    ratio = SEL_BLOCK // CMP_BLOCK
    nsb = ncb // ratio
    slopes = _alibi_slopes()
    q = q_ref[...]
    kvc = _bf(kvc_ref[...])
    rows = q.shape[0]
    cb = lax.broadcasted_iota(jnp.int32, (1, ncb), 1)
    cvalid = (cb * CMP_BLOCK + (CMP_BLOCK - 1)) <= qpos
    cdist = float(qpos) - (cb.astype(F32) * CMP_BLOCK + (CMP_BLOCK - 1) / 2.0)
    pi = lax.broadcasted_iota(jnp.int32, (ncb, nsb), 0)
    pj = lax.broadcasted_iota(jnp.int32, (ncb, nsb), 1) * ratio
    pool = jnp.where((pi >= pj) & (pi < pj + ratio), 1.0, 0.0).astype(BF16)
    lane = lax.broadcasted_iota(jnp.int32, (rows, nsb), 1).astype(F32)
    lane_out = lax.broadcasted_iota(jnp.int32, (rows, LANE), 1)
    o_cmp = []
    id_rows = []
    for g in range(G):
        kc = kvc[:, g * DH:(g + 1) * DH]
        vc = kvc[:, gw + g * DH:gw + (g + 1) * DH]
        imp = jnp.zeros((rows, nsb), F32)
        for r in range(R):
            hd = g * R + r
            qh = _bf(q[:, hd * DH:(hd + 1) * DH] * (DH ** -0.5))
            s = jnp.where(cvalid, _dot_nt(qh, kc) - slopes[hd] * cdist, NEG)
            p = jnp.where(cvalid, _softmax_rows(s), 0.0)
            o_cmp.append(_dot(_bf(p), vc))
            ph, plo = _split2(p)
            imp = imp + _dot(ph, pool) + _dot(plo, pool)
        val = jnp.where(lane == 0.0, -1.0, imp)
        ids = jnp.zeros((rows, LANE), F32)
        for it in range(SEL_TOPK - 2):
            mx = jnp.max(val, axis=1, keepdims=True)
            ix = jnp.min(jnp.where(val == mx, lane, float(nsb)), axis=1, keepdims=True)
            ids = jnp.where(lane_out == it + 1, ix, ids)
            val = jnp.where(lane == ix, -2.0, val)
        id_rows.append(ids[0:1, :])
    ocmp_ref[...] = jnp.concatenate(o_cmp, axis=1)
    pad = jnp.zeros((ids_ref.shape[0] - G, LANE), F32)
    ids_ref[...] = jnp.concatenate(id_rows + [pad], axis=0).astype(jnp.int32)


def _nsa_score_call(z_s, q_idx, kvc, qpos):
    nb, ncb, w = kvc.shape
    dh = w // (2 * NSA_KV)
    qw = NSA_HEADS * dh
    return pl.pallas_call(
        functools.partial(_nsa_score_kernel, DH=dh, qpos=qpos), grid=(nb,),
        in_specs=[pl.BlockSpec((SROWS, qw), lambda b: (b, q_idx)),
                  pl.BlockSpec((None, ncb, w), lambda b: (b, 0, 0))],
        out_specs=[pl.BlockSpec((SROWS, qw), lambda b: (b, 0)),
                   pl.BlockSpec((None, SUBLANE, LANE), lambda b: (b, 0, 0))],
        out_shape=[jax.ShapeDtypeStruct((nb * SROWS, qw), F32),
                   jax.ShapeDtypeStruct((nb, SUBLANE, LANE), jnp.int32)],
        compiler_params=_cparams(("arbitrary",)), name="nsa_cmp_score",
    )(z_s, kvc)


def _nsa_gather_kernel(sel_ref, q_ref, new_ref, small_ref, ocmp_ref, win_ref, *rest,
                       DH, NSEL, qpos, g_lane):
    kv = rest[:NSA_KV * NSEL]
    o_ref = rest[NSA_KV * NSEL]
    b = pl.program_id(0)
    G, R = NSA_KV, NSA_R
    gw = G * DH
    slopes = _alibi_slopes()
    q = q_ref[...]
    new = new_ref[...]
    gate = _sigmoid(small_ref[...])
    ocmp = ocmp_ref[...]
    wlen = win_ref.shape[0]
    jw = lax.broadcasted_iota(jnp.int32, (1, wlen), 1)
    distw = wlen - jw
    validw = (distw < WINDOW) & (qpos - distw >= 0)
    off = lax.broadcasted_iota(jnp.int32, (1, SEL_BLOCK), 1)
    outs = []
    for g in range(G):
        ksn = _bf(new[:, 2 * gw + g * DH:2 * gw + (g + 1) * DH]).astype(F32)
        vsn = _bf(new[:, 3 * gw + g * DH:3 * gw + (g + 1) * DH]).astype(F32)
        kwn = _bf(new[:, 4 * gw + g * DH:4 * gw + (g + 1) * DH]).astype(F32)
        vwn = _bf(new[:, 5 * gw + g * DH:5 * gw + (g + 1) * DH]).astype(F32)
        kw = _bf(win_ref[:, g * DH:(g + 1) * DH])
        vw = _bf(win_ref[:, gw + g * DH:gw + (g + 1) * DH])
        kblk = [_bf(kv[g * NSEL + i][:, 0, g, :]) for i in range(NSEL)]
        vblk = [_bf(kv[g * NSEL + i][:, 1, g, :]) for i in range(NSEL)]
        nslot = NSA_KV * SEL_TOPK
        dists = [qpos - (sel_ref[b, 2 * nslot + g * SEL_TOPK + i] * SEL_BLOCK + off) for i in range(NSEL)]
        for r in range(R):
            hd = g * R + r
            slope = slopes[hd]
            qh = _bf(q[:, hd * DH:(hd + 1) * DH] * (DH ** -0.5))
            qf = qh.astype(F32)
            s_list = [jnp.where(dists[i] >= 0, _dot_nt(qh, kblk[i]) - slope * dists[i].astype(F32), NEG)
                      for i in range(NSEL)]
            s_new = jnp.sum(qf * ksn, axis=1, keepdims=True)
            mx = s_new
            for s in s_list:
                mx = jnp.maximum(mx, jnp.max(s, axis=1, keepdims=True))
            p_new = jnp.exp(s_new - mx)
            den = p_new
            acc = p_new * vsn
            for i, s in enumerate(s_list):
                p = jnp.exp(s - mx)
                den = den + jnp.sum(p, axis=1, keepdims=True)
                acc = acc + _dot(_bf(p), vblk[i])
            o_slc = acc / den
            s = jnp.where(validw, _dot_nt(qh, kw) - slope * distw.astype(F32), NEG)
            s_new = jnp.sum(qf * kwn, axis=1, keepdims=True)
            mx = jnp.maximum(s_new, jnp.max(s, axis=1, keepdims=True))
            p = jnp.exp(s - mx)
            p_new = jnp.exp(s_new - mx)
            o_win = (_dot(_bf(p), vw) + p_new * vwn) / (jnp.sum(p, axis=1, keepdims=True) + p_new)
            gc, gs, gwn = [gate[:, g_lane + c * NSA_HEADS + hd:g_lane + c * NSA_HEADS + hd + 1] for c in range(3)]
            outs.append(gc * ocmp[:, hd * DH:(hd + 1) * DH] + gs * o_slc + gwn * o_win)
    o_ref[...] = jnp.concatenate(outs, axis=1)


def _nsa_gather_call(z_s, zsm_s, q_idx, kv_idx, ocmp, win, cache, page_table, ids, l, qpos, g_lane):
    nb = page_table.shape[0]
    dh = cache.shape[5]
    G = NSA_KV
    qw = NSA_HEADS * dh
    nsel = SEL_TOPK - 1
    bpp = PAGE_SIZE // SEL_BLOCK
    wlen, ww = win.shape[2:]
    nslot = G * SEL_TOPK
    sel = jnp.concatenate([jnp.take_along_axis(page_table, ids // bpp, axis=1), ids % bpp, ids], axis=1)
    in_specs = [pl.BlockSpec((SROWS, qw), lambda b, s: (b, q_idx)),
                pl.BlockSpec((SROWS, 6 * G * dh), lambda b, s: (b, kv_idx)),
                pl.BlockSpec((SROWS, LANE), lambda b, s: (b, 0)),
                pl.BlockSpec((SROWS, qw), lambda b, s: (b, 0)),
                pl.BlockSpec((None, None, wlen, ww), lambda b, s: (l, b, 0, 0))]
    for g in range(G):
        for i in range(nsel):
            in_specs.append(pl.BlockSpec(
                (None, None, SEL_BLOCK, 2, G, dh),
                lambda b, s, j=g * SEL_TOPK + i: (l, s[b, j], s[b, nslot + j], 1, 0, 0)))
    gs = pltpu.PrefetchScalarGridSpec(
        num_scalar_prefetch=1, grid=(nb,), in_specs=in_specs,
        out_specs=pl.BlockSpec((SROWS, qw), lambda b, s: (b, 0)))
    kern = functools.partial(_nsa_gather_kernel, DH=dh, NSEL=nsel, qpos=qpos, g_lane=g_lane)
    return pl.pallas_call(
        kern, grid_spec=gs,
        out_shape=jax.ShapeDtypeStruct((nb * SROWS, qw), F32),
        compiler_params=_cparams(("arbitrary",)), name="nsa_sel_win",
    )(sel, z_s, z_s, zsm_s, ocmp, win, *([cache] * (G * nsel)))


def _lane_vec(depth, lane, values):
    n = values.shape[1]
    return jnp.zeros((depth, 1, LANE), F32).at[:, 0, lane:lane + n].set(values.astype(F32))


def kernel(x_prompt, x_sample, cache_nsa_kv, cache_nsa_win, state_gla, state_ssd, state_ssd_conv, state_mlstm_c, state_mlstm_n, state_mlstm_m, page_table, c_prompt, c_sample, ada_w, ada_b, norm_mix_g, norm_ffn_g, w_in, gla_w_a, gla_b_a, gla_norm_g, ssd_conv_w, ssd_conv_b, ssd_dt_bias, ssd_a_log, ssd_d, ssd_norm_g, ml_b_i, ml_b_f, ml_norm_g, w_branch, w_out, ffn_w_gate, ffn_w_up, ffn_w_down, final_norm_g):
    nbp, seq, d = x_prompt.shape
    nbs, dec_seq, _ = x_sample.shape
    assert dec_seq == 1 and seq % CHUNK == 0
    depth = w_in.shape[0]
    past_len = page_table.shape[1] * PAGE_SIZE
    bw = d // 4
    dh = bw // NSA_HEADS
    gw = NSA_KV * dh
    src, dst, n_packed, _ = _packed_offsets(d)
    kv_off = dst['nsa_kv']
    q_idx = dst['nsa_q'] // bw
    kv_idx = kv_off // (6 * gw)
    xbc_idx = dst['ssd_xbc'] // src['ssd_xbc'][1]
    zg_idx = dst['ssd_z'] // bw
    mlqk_idx = dst['ml_q'] // (2 * bw)
    mlvo_idx = dst['ml_v'] // (2 * bw)
    for name, width in (('gla_q', 3 * bw), ('nsa_q', bw), ('nsa_kv', 6 * gw), ('ssd_xbc', src['ssd_xbc'][1]),
                        ('ssd_z', bw), ('ml_q', 2 * bw), ('ml_v', 2 * bw), ('merge', Z_TILE)):
        assert dst[name] % width == 0
    assert dst['gla_q'] == 0
    g_lane, dt_lane, i_lane, f_lane = dst['nsa_g'], dst['ssd_dt'], dst['ml_i'], dst['ml_f']

    crow = 2 * SUBLANE
    c_all = jnp.concatenate([c_prompt, c_sample, jnp.zeros((crow - nbp - nbs, d), F32)], axis=0)
    mod = _ada(c_all, ada_w, ada_b)
    modp = mod[:, :nbp].reshape(depth, nbp, 1, 6 * d)
    mods = jnp.repeat(mod[:, nbp:nbp + nbs], SROWS, axis=1)

    w_mix, w_gate, w_small = _pack_w_in(w_in)
    n_mix = dst['merge']
    assert n_mix % (4 * LANE) == 0
    r3 = lambda a: a.reshape(depth, 1, a.shape[-1])
    wa_pad = jnp.zeros((depth, LANE, gla_w_a.shape[2]), F32).at[:, dst['gla_a']:dst['gla_a'] + GLA_RANK].set(gla_w_a)
    dtb = _lane_vec(depth, dt_lane, ssd_dt_bias)
    alog = _lane_vec(depth, dt_lane, ssd_a_log)
    dvec = jnp.repeat(ssd_d, bw // SSD_HEADS, axis=1).reshape(depth, 1, bw)
    bif = _lane_vec(depth, i_lane, ml_b_i) + _lane_vec(depth, f_lane, ml_b_f)
    win4 = cache_nsa_win.reshape(depth, nbs, cache_nsa_win.shape[2], 2 * gw)

    xp = x_prompt.reshape(nbp * seq, d)
    xs = jnp.zeros((nbs, SROWS, d), F32).at[:, 0].set(x_sample[:, 0]).reshape(nbs * SROWS, d)

    zeros_like_p = lambda a: jnp.zeros((nbp,) + a.shape[2:], F32)
    gla0, ssd0, conv0, mlc0 = (zeros_like_p(a) for a in (state_gla, state_ssd, state_ssd_conv, state_mlstm_c))
    nm0 = jnp.zeros((nbp, SUBLANE, LANE), F32)
    nm_s = (jnp.zeros((depth, nbs, SUBLANE, LANE), F32).at[:, :, :ML_HEADS, :].set(state_mlstm_n)
            .at[:, :, ML_HEADS, :ML_HEADS].set(state_mlstm_m))

    out = {k: [] for k in ('rows_p', 'rows_s', 'win_p', 'win_s', 'gla_p', 'gla_s', 'ssd_p', 'ssd_s', 'conv_p',
                           'conv_s', 'mlc_p', 'mlc_s', 'mln_p', 'mln_s', 'mlm_p', 'mlm_s')}
    for l in range(depth):
        hp = _norm_mod_prompt(xp, r3(norm_mix_g), modp, l, 0, 1, seq)
        hs = _norm_mod_sample(xs, r3(norm_mix_g), mods, l, 0, 1)
        zp, zs = _proj_nt(hp, hs, w_mix, l, 1024, n_mix // 4, "proj_in")
        zgp, zgs = _proj_nt(hp, hs, w_small, l, 2048, LANE, "proj_small")
        gtp, gts = _proj_nt(hp, hs, w_gate, l, 1024, d, "proj_gate", BF16)

        o_gla_p, gla_p = _gla_call(zp, zgp, wa_pad, r3(gla_b_a), r3(gla_norm_g), gla0, l, nbp, seq, CHUNK,
                                   CHUNK, BF16)
        o_gla_s, gla_s = _gla_call(zs, zgs, wa_pad, r3(gla_b_a), r3(gla_norm_g), state_gla[l], l, nbs, SROWS,
                                   SROWS, 1, F32)
        ssd_args = (ssd_conv_w, r3(ssd_conv_b), dtb, alog, dvec, r3(ssd_norm_g))
        o_ssd_p, ssd_p, conv_p = _ssd_call(zp, zgp, xbc_idx, zg_idx, *ssd_args, ssd0, conv0, l, nbp, seq,
                                           CHUNK, CHUNK, dt_lane, BF16)
        o_ssd_s, ssd_s, conv_s = _ssd_call(zs, zgs, xbc_idx, zg_idx, *ssd_args, state_ssd[l],
                                           state_ssd_conv[l], l, nbs, SROWS, SROWS, 1, dt_lane, F32)
        o_ml_p, mlc_p, mlnm_p = _mlstm_call(zp, zgp, mlqk_idx, mlvo_idx, bif, r3(ml_norm_g), mlc0, nm0, l,
                                            nbp, seq, CHUNK, CHUNK, i_lane, f_lane, BF16)
        o_ml_s, mlc_s, mlnm_s = _mlstm_call(zs, zgs, mlqk_idx, mlvo_idx, bif, r3(ml_norm_g), state_mlstm_c[l],
                                            nm_s[l], l, nbs, SROWS, SROWS, 1, i_lane, f_lane, F32)
        o_nsa_p = _nsa_prompt_call(zp, zgp, q_idx, kv_idx, nbp, seq, g_lane)
        kvc = _nsa_means_call(cache_nsa_kv, page_table, l)
        ocmp, ids = _nsa_score_call(zs, q_idx, kvc, past_len)
        ids2 = ids[:, :NSA_KV, :SEL_TOPK].reshape(nbs, NSA_KV * SEL_TOPK)
        o_nsa_s = _nsa_gather_call(zs, zgs, q_idx, kv_idx, ocmp, win4, cache_nsa_kv, page_table, ids2, l,
                                   past_len, g_lane)

        mp, ms = _merge((o_gla_p, o_nsa_p, o_ssd_p, o_ml_p), (o_gla_s, o_nsa_s, o_ssd_s, o_ml_s), gtp, gts, w_branch,
                        l, 1024, 512)
        xp, xs = _mm_resid(mp, ms, w_out, xp, xs, modp, mods, l, 2, seq, 1024, 512, "proj_out")
        hp = _norm_mod_prompt(xp, r3(norm_ffn_g), modp, l, 3, 4, seq)
        hs = _norm_mod_sample(xs, r3(norm_ffn_g), mods, l, 3, 4)
        ap, a_s = _mm_swiglu(hp, hs, ffn_w_gate, ffn_w_up, l, 1024, 512)
        xp, xs = _mm_resid(ap, a_s, ffn_w_down, xp, xs, modp, mods, l, 5, seq, 512, 512, "ffn_down")

        zp3 = zp.reshape(nbp, seq, n_mix)
        zs3 = zs.reshape(nbs, SROWS, n_mix)[:, :1]
        out['rows_p'].append(zp3[:, :, kv_off:kv_off + 4 * gw].reshape(nbp, seq, 4, NSA_KV, dh))
        out['rows_s'].append(zs3[:, :, kv_off:kv_off + 4 * gw].reshape(nbs, 1, 4, NSA_KV, dh))
        keep = min(WINDOW, seq)
        out['win_p'].append(zp3[:, seq - keep:, kv_off + 4 * gw:kv_off + 6 * gw].reshape(nbp, keep, 2, NSA_KV, dh))
        win_all = jnp.concatenate([cache_nsa_win[l], zs3[:, :, kv_off + 4 * gw:kv_off + 6 * gw]
                                   .reshape(nbs, 1, 2, NSA_KV, dh)], axis=1)
        out['win_s'].append(win_all[:, win_all.shape[1] - min(WINDOW, win_all.shape[1]):])
        for key, val in (('gla_p', gla_p), ('gla_s', gla_s), ('ssd_p', ssd_p), ('ssd_s', ssd_s), ('conv_p', conv_p),
                         ('conv_s', conv_s), ('mlc_p', mlc_p), ('mlc_s', mlc_s)):
            out[key].append(val)
        for tag, nm in (('p', mlnm_p), ('s', mlnm_s)):
            out['mln_' + tag].append(nm[:, :ML_HEADS, :])
            out['mlm_' + tag].append(nm[:, ML_HEADS, :ML_HEADS])

    y_prompt = _final_norm(xp, final_norm_g).reshape(nbp, seq, d)
    y_sample = _final_norm(xs, final_norm_g).reshape(nbs, SROWS, d)[:, :1]
    st = {k: jnp.stack(v) for k, v in out.items()}
    return (y_prompt, y_sample, st['rows_p'], st['rows_s'], st['win_p'], st['win_s'], st['gla_p'], st['gla_s'],
            st['ssd_p'], st['ssd_s'], st['conv_p'], st['conv_s'], st['mlc_p'], st['mlc_s'], st['mln_p'], st['mln_s'],
            st['mlm_p'], st['mlm_s'])
```

```python
import functools
import math

import jax
import jax.numpy as jnp
from jax import lax
from jax.experimental import pallas as pl
from jax.experimental.pallas import tpu as pltpu

F32 = jnp.float32
BF16 = jnp.bfloat16

PAGE_SIZE = 128
GLA_HEADS = 4
GLA_RANK = 16
GLA_TAU = 16.0
NSA_HEADS = 4
NSA_KV = 2
NSA_R = NSA_HEADS // NSA_KV
CMP_BLOCK = 32
SEL_BLOCK = 64
SEL_TOPK = 16
WINDOW = 512
FORCE_SCORE = 1e4
NEG = -1e30
SSD_HEADS = 8
SSD_GROUPS = 2
SSD_N = 128
SSD_CONV = 4
ML_HEADS = 4
CHUNK = 128
EPS = 1e-6

LANE = 128
SUBLANE = 8
VMEM_LIMIT = 56 * 1024 * 1024

SROWS = SUBLANE


def _cparams(sem):
    return pltpu.CompilerParams(dimension_semantics=sem, vmem_limit_bytes=VMEM_LIMIT)


def _layout(d_model):
    bw = d_model // 4
    gdk = (bw // GLA_HEADS) // 2
    conv_ch = bw + 2 * SSD_GROUPS * SSD_N
    splits = (('gla_q', GLA_HEADS * gdk), ('gla_k', GLA_HEADS * gdk), ('gla_v', bw), ('gla_r', bw),
              ('gla_a', GLA_RANK), ('nsa_q', bw), ('nsa_kv', 6 * NSA_KV * (bw // NSA_HEADS)),
              ('nsa_g', 3 * NSA_HEADS), ('ssd_z', bw), ('ssd_xbc', conv_ch), ('ssd_dt', SSD_HEADS),
              ('ml_q', bw), ('ml_k', bw), ('ml_v', bw), ('ml_o', bw), ('ml_i', ML_HEADS), ('ml_f', ML_HEADS),
              ('merge', 4 * d_model))
    src = {}
    off = 0
    for name, w in splits:
        src[name] = (off, w)
        off += w
    return src, off


_Z_ORDER = ('gla_q', 'gla_k', 'gla_v', 'gla_r', 'nsa_kv', 'ml_q', 'ml_k', 'ml_v', 'ml_o', 'ssd_xbc',
            'nsa_q', 'ssd_z', 'merge')
_Z_SMALL = ('gla_a', 'nsa_g', 'ssd_dt', 'ml_i', 'ml_f')


Z_TILE = 512
SMALL_TILE = LANE // 4


def _packed_offsets(d_model):
    src, d_in = _layout(d_model)
    dst = {}
    off = 0
    for name in _Z_ORDER:
        dst[name] = off
        off += src[name][1]
    starts = []
    for name in _Z_SMALL:
        o, w = src[name]
        if not starts or o + w > starts[-1] + SMALL_TILE:
            starts.append(o)
        dst[name] = (len(starts) - 1) * SMALL_TILE + o - starts[-1]
    assert len(starts) * SMALL_TILE <= LANE and starts[-1] + SMALL_TILE <= d_in
    dst['_small_starts'] = starts
    return src, dst, off, d_in


def _pack_rows_kernel(rows_ref, w_ref, o_ref):
    o_ref[...] = pltpu.einshape("nlk->lnk", w_ref[...]).astype(o_ref.dtype)


def _pack_rows(w_t, row_starts, tile):
    _, depth, d = w_t.shape
    nt = len(row_starts)
    gs = pltpu.PrefetchScalarGridSpec(
        num_scalar_prefetch=1, grid=(nt,),
        in_specs=[pl.BlockSpec((pl.Element(tile), pl.Element(depth), pl.Element(d)), lambda t, r: (r[t], 0, 0))],
        out_specs=pl.BlockSpec((depth, tile, d), lambda t, r: (0, t, 0)))
    return pl.pallas_call(
        _pack_rows_kernel, grid_spec=gs, out_shape=jax.ShapeDtypeStruct((depth, nt * tile, d), BF16),
        compiler_params=_cparams(("arbitrary",)), name="pack_w_in",
    )(jnp.asarray(row_starts, jnp.int32), w_t)


def _pack_w_in(w_in):
    depth, d_model, _ = w_in.shape
    src, dst, n_packed, d_in = _packed_offsets(d_model)
    assert w_in.shape[2] == d_in
    w_t = jnp.transpose(w_in, (2, 0, 1))
    tile = LANE

    def starts(names):
        out = []
        for n in names:
            assert src[n][1] % tile == 0
            out += [src[n][0] + i for i in range(0, src[n][1], tile)]
        return out

    mix = _pack_rows(w_t, starts(_Z_ORDER[:-1]), tile)
    gate = _pack_rows(w_t, starts(_Z_ORDER[-1:]), tile)
    small_starts = dst['_small_starts']
    small = _pack_rows(w_t, small_starts + [small_starts[-1]] * (LANE // SMALL_TILE - len(small_starts)), SMALL_TILE)
    return mix, gate, small


def _sigmoid(x):
    return 1.0 / (1.0 + jnp.exp(-x))


def _silu(x):
    return x * _sigmoid(x)


def _log_sigmoid(x):
    return jnp.minimum(x, 0.0) - jnp.log1p(jnp.exp(-jnp.abs(x)))


def _softplus(x):
    return jnp.maximum(x, 0.0) + jnp.log1p(jnp.exp(-jnp.abs(x)))


def _dot(a, b):
    return jnp.dot(a, b, preferred_element_type=F32)


def _dot_nt(a, b):
    return lax.dot_general(a, b, (((1,), (1,)), ((), ())), preferred_element_type=F32)


def _dot_tn(a, b):
    return lax.dot_general(a, b, (((0,), (0,)), ((), ())), preferred_element_type=F32)


def _bf(x):
    return x.astype(BF16)


def _split2(x):
    h = x.astype(BF16)
    return h, (x - h.astype(F32)).astype(BF16)


def _causal(c):
    r = lax.broadcasted_iota(jnp.int32, (c, c), 0)
    s = lax.broadcasted_iota(jnp.int32, (c, c), 1)
    return r >= s


def _cumsum_rows(x):
    n = x.shape[0]
    row = lax.broadcasted_iota(jnp.int32, (n, 1), 0)
    step = 1
    while step < n:
        x = x + jnp.where(row >= step, pltpu.roll(x, step, 0), 0.0)
        step *= 2
    return x


def _rms(x, g):
    return x * lax.rsqrt(jnp.mean(x * x, axis=-1, keepdims=True) + EPS) * g


def _softmax_rows(s):
    m = jnp.max(s, axis=-1, keepdims=True)
    e = jnp.exp(s - m)
    return e / jnp.sum(e, axis=-1, keepdims=True)


def _alibi_slopes():
    return [2.0 ** (-8.0 * (h + 1) / NSA_HEADS) for h in range(NSA_HEADS)]


def _ada_kernel(c_ref, w_ref, b_ref, o_ref):
    a = _bf(_silu(c_ref[...]))
    o_ref[...] = _dot(a, _bf(w_ref[...])) + b_ref[...]


def _ada(c_all, ada_w, ada_b):
    depth, d, n = ada_w.shape
    rows = c_all.shape[0]
    tn = 1024
    return pl.pallas_call(
        _ada_kernel, grid=(depth, n // tn),
        in_specs=[pl.BlockSpec((rows, d), lambda l, j: (0, 0)),
                  pl.BlockSpec((None, d, tn), lambda l, j: (l, 0, j)),
                  pl.BlockSpec((None, 1, tn), lambda l, j: (l, 0, j))],
        out_specs=pl.BlockSpec((None, rows, tn), lambda l, j: (l, 0, j)),
        out_shape=jax.ShapeDtypeStruct((depth, rows, n), F32),
        compiler_params=_cparams(("arbitrary", "arbitrary")), name="ada",
    )(c_all, ada_w, ada_b.reshape(depth, 1, n))


def _norm_mod_kernel(x_ref, g_ref, sc_ref, sh_ref, o_ref):
    y = _rms(x_ref[...], g_ref[...])
    o_ref[...] = (y * (1.0 + sc_ref[...]) + sh_ref[...]).astype(o_ref.dtype)


def _norm_kernel(x_ref, g_ref, o_ref):
    o_ref[...] = _rms(x_ref[...], g_ref[...]).astype(o_ref.dtype)


def _norm_mod_prompt(x, g, modp, l, j_shift, j_scale, seq):
    m, d = x.shape
    tm = 512
    per = seq // tm
    return pl.pallas_call(
        _norm_mod_kernel, grid=(m // tm,),
        in_specs=[pl.BlockSpec((tm, d), lambda i: (i, 0)),
                  pl.BlockSpec((None, 1, d), lambda i: (l, 0, 0)),
                  pl.BlockSpec((None, None, 1, d), lambda i: (l, i // per, 0, j_scale)),
                  pl.BlockSpec((None, None, 1, d), lambda i: (l, i // per, 0, j_shift))],
        out_specs=pl.BlockSpec((tm, d), lambda i: (i, 0)),
        out_shape=jax.ShapeDtypeStruct((m, d), BF16),
        compiler_params=_cparams(("arbitrary",)), name="norm_mod_prompt",
    )(x, g, modp, modp)


def _norm_mod_sample(x, g, mods, l, j_shift, j_scale):
    m, d = x.shape
    return pl.pallas_call(
        _norm_mod_kernel, grid=(1,),
        in_specs=[pl.BlockSpec((m, d), lambda i: (0, 0)),
                  pl.BlockSpec((None, 1, d), lambda i: (l, 0, 0)),
                  pl.BlockSpec((None, m, d), lambda i: (l, 0, j_scale)),
                  pl.BlockSpec((None, m, d), lambda i: (l, 0, j_shift))],
        out_specs=pl.BlockSpec((m, d), lambda i: (0, 0)),
        out_shape=jax.ShapeDtypeStruct((m, d), BF16),
        compiler_params=_cparams(("arbitrary",)), name="norm_mod_sample",
    )(x, g, mods, mods)


def _final_norm(x, g):
    m, d = x.shape
    tm = min(m, 512)
    return pl.pallas_call(
        _norm_kernel, grid=(m // tm,),
        in_specs=[pl.BlockSpec((tm, d), lambda i: (i, 0)), pl.BlockSpec((1, d), lambda i: (0, 0))],
        out_specs=pl.BlockSpec((tm, d), lambda i: (i, 0)),
        out_shape=jax.ShapeDtypeStruct((m, d), F32),
        compiler_params=_cparams(("arbitrary",)), name="final_norm",
    )(x, g.reshape(1, d))


def _proj_nt_kernel(a_ref, as_ref, wt_ref, o_ref, os_ref):
    w = wt_ref[...]

    @pl.when(pl.program_id(1) == 0)
    def _():
        os_ref[...] = _dot_nt(as_ref[...], w).astype(os_ref.dtype)

    o_ref[...] = _dot_nt(a_ref[...], w).astype(o_ref.dtype)


def _proj_nt(a, a_s, w_t, l, tm, tn, name, out_dtype=F32):
    m, k = a.shape
    ms = a_s.shape[0]
    n = w_t.shape[1]
    return pl.pallas_call(
        _proj_nt_kernel, grid=(n // tn, m // tm),
        in_specs=[pl.BlockSpec((tm, k), lambda j, i: (i, 0)),
                  pl.BlockSpec((ms, k), lambda j, i: (0, 0)),
                  pl.BlockSpec((None, tn, k), lambda j, i: (l, j, 0))],
        out_specs=[pl.BlockSpec((tm, tn), lambda j, i: (i, j)),
                   pl.BlockSpec((ms, tn), lambda j, i: (0, j))],
        out_shape=[jax.ShapeDtypeStruct((m, n), out_dtype), jax.ShapeDtypeStruct((ms, n), out_dtype)],
        compiler_params=_cparams(("arbitrary", "arbitrary")), name=name,
    )(a, a_s, w_t)


def _mm_resid_kernel(a_ref, as_ref, w_ref, x_ref, xs_ref, gt_ref, gts_ref, o_ref, os_ref, wb_ref):
    @pl.when(pl.program_id(1) == 0)
    def _():
        wb_ref[...] = _bf(w_ref[...])
        os_ref[...] = xs_ref[...] + gts_ref[...] * _dot(as_ref[...], wb_ref[...])

    o_ref[...] = x_ref[...] + gt_ref[...] * _dot(a_ref[...], wb_ref[...])


def _mm_resid(a, a_s, w, x, x_s, modp, mods, l, j_gate, seq, tm, tn, name):
    m, k = a.shape
    ms = a_s.shape[0]
    n = w.shape[2]
    per = seq // tm
    nj = n // tn
    return pl.pallas_call(
        _mm_resid_kernel, grid=(nj, m // tm),
        in_specs=[pl.BlockSpec((tm, k), lambda j, i: (i, 0)),
                  pl.BlockSpec((ms, k), lambda j, i: (0, 0)),
                  pl.BlockSpec((None, k, tn), lambda j, i: (l, 0, j)),
                  pl.BlockSpec((tm, tn), lambda j, i: (i, j)),
                  pl.BlockSpec((ms, tn), lambda j, i: (0, j)),
                  pl.BlockSpec((None, None, 1, tn), lambda j, i: (l, i // per, 0, j_gate * nj + j)),
                  pl.BlockSpec((None, ms, tn), lambda j, i: (l, 0, j_gate * nj + j))],
        out_specs=[pl.BlockSpec((tm, tn), lambda j, i: (i, j)),
                   pl.BlockSpec((ms, tn), lambda j, i: (0, j))],
        out_shape=[jax.ShapeDtypeStruct((m, n), F32), jax.ShapeDtypeStruct((ms, n), F32)],
        scratch_shapes=[pltpu.VMEM((k, tn), BF16)],
        compiler_params=_cparams(("arbitrary", "arbitrary")), name=name,
    )(a, a_s, w, x, x_s, modp, mods)


def _mm_swiglu_kernel(a_ref, as_ref, wg_ref, wu_ref, o_ref, os_ref, wgb_ref, wub_ref):
    @pl.when(pl.program_id(1) == 0)
    def _():
        wgb_ref[...] = _bf(wg_ref[...])
        wub_ref[...] = _bf(wu_ref[...])
        a_s = as_ref[...]
        os_ref[...] = (_silu(_dot(a_s, wgb_ref[...])) * _dot(a_s, wub_ref[...])).astype(os_ref.dtype)

    a = a_ref[...]
    o_ref[...] = (_silu(_dot(a, wgb_ref[...])) * _dot(a, wub_ref[...])).astype(o_ref.dtype)


def _mm_swiglu(a, a_s, wg, wu, l, tm, tn):
    m, k = a.shape
    ms = a_s.shape[0]
    n = wg.shape[2]
    return pl.pallas_call(
        _mm_swiglu_kernel, grid=(n // tn, m // tm),
        in_specs=[pl.BlockSpec((tm, k), lambda j, i: (i, 0)),
                  pl.BlockSpec((ms, k), lambda j, i: (0, 0)),
                  pl.BlockSpec((None, k, tn), lambda j, i: (l, 0, j)),
                  pl.BlockSpec((None, k, tn), lambda j, i: (l, 0, j))],
        out_specs=[pl.BlockSpec((tm, tn), lambda j, i: (i, j)),
                   pl.BlockSpec((ms, tn), lambda j, i: (0, j))],
        out_shape=[jax.ShapeDtypeStruct((m, n), BF16), jax.ShapeDtypeStruct((ms, n), BF16)],
        scratch_shapes=[pltpu.VMEM((k, tn), BF16), pltpu.VMEM((k, tn), BF16)],
        compiler_params=_cparams(("arbitrary", "arbitrary")), name="ffn_gate_up",
    )(a, a_s, wg, wu)


def _merge_kernel(*refs):
    o_p = refs[0:4]
    o_s = refs[4:8]
    g_p = refs[8:12]
    g_s = refs[12:16]
    w_ref, out_ref, outs_ref, wb_ref = refs[16:20]

    @pl.when(pl.program_id(1) == 0)
    def _():
        wb_ref[...] = _bf(w_ref[...])
        acc = None
        for b in range(4):
            t = _sigmoid(g_s[b][...].astype(F32)) * _dot(_bf(o_s[b][...]), wb_ref[b])
            acc = t if acc is None else acc + t
        outs_ref[...] = acc.astype(outs_ref.dtype)

    acc = None
    for b in range(4):
        t = _sigmoid(g_p[b][...].astype(F32)) * _dot(_bf(o_p[b][...]), wb_ref[b])
        acc = t if acc is None else acc + t
    out_ref[...] = acc.astype(out_ref.dtype)


def _merge(o_p, o_s, z, z_s, w_branch, l, tm, tn):
    m = z.shape[0]
    ms = z_s.shape[0]
    bw = w_branch.shape[2]
    d = w_branch.shape[3]
    in_specs = ([pl.BlockSpec((tm, bw), lambda j, i: (i, 0)) for _ in range(4)]
                + [pl.BlockSpec((ms, bw), lambda j, i: (0, 0)) for _ in range(4)]
                + [pl.BlockSpec((tm, tn), lambda j, i, b=b: (i, b * (d // tn) + j)) for b in range(4)]
                + [pl.BlockSpec((ms, tn), lambda j, i, b=b: (0, b * (d // tn) + j)) for b in range(4)]
                + [pl.BlockSpec((None, 4, bw, tn), lambda j, i: (l, 0, 0, j))])
    return pl.pallas_call(
        _merge_kernel, grid=(d // tn, m // tm),
        in_specs=in_specs,
        out_specs=[pl.BlockSpec((tm, tn), lambda j, i: (i, j)), pl.BlockSpec((ms, tn), lambda j, i: (0, j))],
        out_shape=[jax.ShapeDtypeStruct((m, d), BF16), jax.ShapeDtypeStruct((ms, d), BF16)],
        scratch_shapes=[pltpu.VMEM((4, bw, tn), BF16)],
        compiler_params=_cparams(("arbitrary", "arbitrary")), name="merge",
    )(*o_p, *o_s, z, z, z, z, z_s, z_s, z_s, z_s, w_branch)


def _live_rows(c_rows, n_live):
    return lax.broadcasted_iota(jnp.int32, (c_rows, 1), 0) < n_live


def _gla_kernel(z_ref, small_ref, wa_ref, ba_ref, gn_ref, s0_ref, o_ref, sout_ref, st_scr, *, NB, C, L, H, DK, DV):
    c = pl.program_id(0)

    @pl.when(c == 0)
    def _():
        st_scr[...] = s0_ref[...]

    causal = _causal(C)
    wa = _bf(wa_ref[...])
    ba = ba_ref[...]
    gn = gn_ref[...]
    mid = C // 2

    def one_sequence(bi):
        z = z_ref[bi]
        q = z[:, 0:H * DK] * (DK ** -0.5)
        k = z[:, H * DK:2 * H * DK]
        v = z[:, 2 * H * DK:2 * H * DK + H * DV]
        r = z[:, 2 * H * DK + H * DV:]
        ga = _dot(_bf(small_ref[bi]), wa) + ba
        glog = _log_sigmoid(ga) * (1.0 / GLA_TAU)
        if L < C:
            live = _live_rows(C, L)
            glog = jnp.where(live, glog, 0.0)
            k = jnp.where(live, k, 0.0)
        b = _cumsum_rows(glog)
        bl = b[C - 1:C, :]
        bm = b[mid - 1:mid, :]
        qi = _bf(q * jnp.exp(b - bm))
        ki = _bf(k * jnp.exp(bm - b))
        qs = q * jnp.exp(b)
        kd = _bf(k * jnp.exp(bl - b))
        ebl_col = jnp.exp(b.T[:, C - 1:C])
        outs = []
        for h in range(H):
            ks = slice(h * DK, (h + 1) * DK)
            vs = slice(h * DV, (h + 1) * DV)
            vh = v[:, vs]
            a = jnp.where(causal, _dot_nt(qi[:, ks], ki[:, ks]), 0.0)
            st = st_scr[bi, h]
            o = _dot(_bf(jnp.concatenate([a, qs[:, ks]], axis=1)), _bf(jnp.concatenate([vh, st], axis=0)))
            st_scr[bi, h] = ebl_col[ks, :] * st + _dot_tn(kd[:, ks], _bf(vh))
            outs.append(_rms(o, gn[:, vs]))
        o_ref[bi] = (jnp.concatenate(outs, axis=1) * _silu(r)).astype(o_ref.dtype)

    for bi in range(NB):
        one_sequence(bi)

    @pl.when(c == pl.num_programs(0) - 1)
    def _():
        sout_ref[...] = st_scr[...]


def _gla_call(z, zsm, wa_pad, ba, gn, s0, l, nb, seq, C, L, out_dtype):
    H, DK, DV = s0.shape[1:]
    nch = seq // C
    width = 2 * H * DK + 2 * H * DV
    z3 = z.reshape(nb, seq, z.shape[1])
    zsm3 = zsm.reshape(nb, seq, LANE)
    kern = functools.partial(_gla_kernel, NB=nb, C=C, L=L, H=H, DK=DK, DV=DV)
    o, s = pl.pallas_call(
        kern, grid=(nch,),
        in_specs=[pl.BlockSpec((nb, C, width), lambda c: (0, c, 0)),
                  pl.BlockSpec((nb, C, LANE), lambda c: (0, c, 0)),
                  pl.BlockSpec((None, LANE, H * DK), lambda c: (l, 0, 0)),
                  pl.BlockSpec((None, 1, H * DK), lambda c: (l, 0, 0)),
                  pl.BlockSpec((None, 1, H * DV), lambda c: (l, 0, 0)),
                  pl.BlockSpec((nb, H, DK, DV), lambda c: (0, 0, 0, 0))],
        out_specs=[pl.BlockSpec((nb, C, H * DV), lambda c: (0, c, 0)),
                   pl.BlockSpec((nb, H, DK, DV), lambda c: (0, 0, 0, 0))],
        out_shape=[jax.ShapeDtypeStruct((nb, seq, H * DV), out_dtype),
                   jax.ShapeDtypeStruct((nb, H, DK, DV), F32)],
        scratch_shapes=[pltpu.VMEM((nb, H, DK, DV), F32)],
        compiler_params=_cparams(("arbitrary",)), name="gla",
    )(z3, zsm3, wa_pad, ba, gn, s0)
    return o.reshape(nb * seq, H * DV), s


def _ssd_kernel(xbc_ref, zg_ref, small_ref, cw_ref, cb_ref, dtb_ref, alog_ref, dvec_ref, gn_ref, s0_ref, conv0_ref,
                o_ref, sout_ref, convout_ref, s_scr, ext_scr, *, NB, C, L, H, G, N, P, dt_lane):
    c = pl.program_id(0)
    kh = SSD_CONV - 1
    base = SUBLANE - kh

    @pl.when(c == 0)
    def _():
        s_scr[...] = s0_ref[...]
        ext_scr[:, base:SUBLANE, :] = conv0_ref[...]

    cw = cw_ref[...]
    cb = cb_ref[...]
    dtb = dtb_ref[...]
    neg_a = -jnp.exp(alog_ref[...])
    dvec = dvec_ref[...]
    gn = gn_ref[...]
    causal = _causal(C)
    bw = H * P
    hpg = H // G

    def one_sequence(bi):
        ext_scr[bi, SUBLANE:SUBLANE + C, :] = xbc_ref[bi]
        conv = cb
        for j in range(SSD_CONV):
            conv = conv + ext_scr[bi, pl.ds(base + j, C), :] * cw[j:j + 1, :]
        hist = ext_scr[bi, pl.ds(base + L, kh), :]
        ext_scr[bi, base:SUBLANE, :] = hist
        xc = _silu(conv)
        sx = xc[:, 0:bw]
        bm = xc[:, bw:bw + G * N]
        cm = xc[:, bw + G * N:]
        dt = _softplus(small_ref[bi] + dtb)
        logg = dt * neg_a
        if L < C:
            live = _live_rows(C, L)
            logg = jnp.where(live, logg, 0.0)
            bm = jnp.where(live, bm, 0.0)
        b = _cumsum_rows(logg)
        b_t = b.T
        ys = []
        for g in range(G):
            cg = cm[:, g * N:(g + 1) * N]
            bg = bm[:, g * N:(g + 1) * N]
            gmat = _dot_nt(_bf(cg), _bf(bg))
            for hh in range(hpg):
                h = g * hpg + hh
                ln = dt_lane + h
                bc = b[:, ln:ln + 1]
                br = b_t[ln:ln + 1, :]
                dec = jnp.exp(jnp.where(causal, bc - br, -jnp.inf))
                xdt = sx[:, h * P:(h + 1) * P] * dt[:, ln:ln + 1]
                st = s_scr[bi, h]
                y = _dot(_bf(jnp.concatenate([gmat * dec, cg * jnp.exp(bc)], axis=1)),
                         _bf(jnp.concatenate([xdt, st], axis=0)))
                bl = bc[C - 1:C, :]
                s_scr[bi, h] = jnp.exp(bl) * st + _dot_tn(_bf(bg * jnp.exp(bl - bc)), _bf(xdt))
                ys.append(y)
        y_all = jnp.concatenate(ys, axis=1) + dvec * sx
        o_ref[bi] = _rms(y_all * _silu(zg_ref[bi]), gn).astype(o_ref.dtype)

    for bi in range(NB):
        one_sequence(bi)

    @pl.when(c == pl.num_programs(0) - 1)
    def _():
        sout_ref[...] = s_scr[...]
        convout_ref[...] = ext_scr[:, base:SUBLANE, :]


def _ssd_call(z, zsm, xbc_idx, zg_idx, cw, cb, dtb, alog, dvec, gn, s0, conv0, l, nb, seq, C, L, dt_lane,
              out_dtype):
    H, N, P = s0.shape[1:]
    G = SSD_GROUPS
    ch = conv0.shape[2]
    bw = H * P
    nch = seq // C
    z3 = z.reshape(nb, seq, z.shape[1])
    zsm3 = zsm.reshape(nb, seq, LANE)
    kern = functools.partial(_ssd_kernel, NB=nb, C=C, L=L, H=H, G=G, N=N, P=P, dt_lane=dt_lane)
    vec = lambda w: pl.BlockSpec((None, 1, w), lambda c: (l, 0, 0))
    o, s, cv = pl.pallas_call(
        kern, grid=(nch,),
        in_specs=[pl.BlockSpec((nb, C, ch), lambda c: (0, c, xbc_idx)),
                  pl.BlockSpec((nb, C, bw), lambda c: (0, c, zg_idx)),
                  pl.BlockSpec((nb, C, LANE), lambda c: (0, c, 0)),
                  pl.BlockSpec((None, SSD_CONV, ch), lambda c: (l, 0, 0)),
                  vec(ch), vec(LANE), vec(LANE), vec(bw), vec(bw),
                  pl.BlockSpec((nb, H, N, P), lambda c: (0, 0, 0, 0)),
                  pl.BlockSpec((nb, SSD_CONV - 1, ch), lambda c: (0, 0, 0))],
        out_specs=[pl.BlockSpec((nb, C, bw), lambda c: (0, c, 0)),
                   pl.BlockSpec((nb, H, N, P), lambda c: (0, 0, 0, 0)),
                   pl.BlockSpec((nb, SSD_CONV - 1, ch), lambda c: (0, 0, 0))],
        out_shape=[jax.ShapeDtypeStruct((nb, seq, bw), out_dtype),
                   jax.ShapeDtypeStruct((nb, H, N, P), F32),
                   jax.ShapeDtypeStruct((nb, SSD_CONV - 1, ch), F32)],
        scratch_shapes=[pltpu.VMEM((nb, H, N, P), F32), pltpu.VMEM((nb, SUBLANE + C, ch), F32)],
        compiler_params=_cparams(("arbitrary",)), name="ssd",
    )(z3, z3, zsm3, cw, cb, dtb, alog, dvec, gn, s0, conv0)
    return o.reshape(nb * seq, bw), s, cv


def _mlstm_kernel(qk_ref, vo_ref, small_ref, bif_ref, gn_ref, c0_ref, nm0_ref, o_ref, cout_ref, nmout_ref,
                  c_scr, nm_scr, *, NB, C, L, H, DH, i_lane, f_lane):
    c = pl.program_id(0)

    @pl.when(c == 0)
    def _():
        c_scr[...] = c0_ref[...]
        nm_scr[...] = nm0_ref[...]

    causal = _causal(C)
    lane = lax.broadcasted_iota(jnp.int32, (1, LANE), 1)
    bif = bif_ref[...]
    gn = gn_ref[...]

    def one_sequence(bi):
        qk = qk_ref[bi]
        vo = vo_ref[bi]
        q = qk[:, 0:H * DH]
        k = qk[:, H * DH:] * (DH ** -0.5)
        v = vo[:, 0:H * DH]
        og = vo[:, H * DH:]
        pre = small_ref[bi] + bif
        logf = _log_sigmoid(pre)
        ipre = pre
        if L < C:
            live = _live_rows(C, L)
            logf = jnp.where(live, logf, 0.0)
            ipre = jnp.where(live, pre, NEG)
        b = _cumsum_rows(logf)
        b_t = b.T
        i_t = ipre.T
        nm = nm_scr[bi]
        m_row = nm[H:H + 1, :]
        outs = []
        for h in range(H):
            sl = slice(h * DH, (h + 1) * DH)
            bc = b[:, f_lane + h:f_lane + h + 1]
            br = b_t[f_lane + h:f_lane + h + 1, :]
            ic = ipre[:, i_lane + h:i_lane + h + 1]
            ir = i_t[i_lane + h:i_lane + h + 1, :]
            m_prev = nm[H:H + 1, h:h + 1]
            logd = jnp.where(causal, bc - br + ir, -jnp.inf)
            gg = bc + m_prev
            m = jnp.maximum(gg, jnp.max(logd, axis=1, keepdims=True))
            dm = jnp.exp(logd - m)
            wi = jnp.exp(gg - m)
            qf = q[:, sl]
            kh = k[:, sl]
            vf = v[:, sl]
            vh = _bf(vf)
            qkm = _dot_nt(_bf(qf), _bf(kh)) * dm
            cst = c_scr[bi, h]
            nrow = nm[h:h + 1, :]
            num = _dot(_bf(jnp.concatenate([qkm, wi * qf], axis=1)), _bf(jnp.concatenate([vf, cst], axis=0)))
            den = jnp.sum(qkm, axis=1, keepdims=True) + wi * jnp.sum(qf * nrow, axis=1, keepdims=True)
            hid = num / jnp.maximum(jnp.abs(den), jnp.exp(-m))
            m_new = m[C - 1:C, :]
            bl = bc[C - 1:C, :]
            ws = jnp.exp(bl - bc + ic - m_new)
            wc = jnp.exp(bl + m_prev - m_new)
            wk = ws * kh
            c_scr[bi, h] = wc * cst + _dot_tn(_bf(wk), vh)
            nm_scr[bi, h:h + 1, :] = wc * nrow + jnp.sum(wk, axis=0, keepdims=True)
            m_row = jnp.where(lane == h, m_new, m_row)
            outs.append(_rms(hid, gn[:, sl]))
        nm_scr[bi, H:H + 1, :] = m_row
        o_ref[bi] = (jnp.concatenate(outs, axis=1) * _sigmoid(og)).astype(o_ref.dtype)

    for bi in range(NB):
        one_sequence(bi)

    @pl.when(c == pl.num_programs(0) - 1)
    def _():
        cout_ref[...] = c_scr[...]
        nmout_ref[...] = nm_scr[...]


def _mlstm_call(z, zsm, qk_idx, vo_idx, bif, gn, c0, nm0, l, nb, seq, C, L, i_lane, f_lane, out_dtype):
    H, DH = c0.shape[1:3]
    nch = seq // C
    z3 = z.reshape(nb, seq, z.shape[1])
    zsm3 = zsm.reshape(nb, seq, LANE)
    kern = functools.partial(_mlstm_kernel, NB=nb, C=C, L=L, H=H, DH=DH, i_lane=i_lane, f_lane=f_lane)
    o, cc, nm = pl.pallas_call(
        kern, grid=(nch,),
        in_specs=[pl.BlockSpec((nb, C, 2 * H * DH), lambda c: (0, c, qk_idx)),
                  pl.BlockSpec((nb, C, 2 * H * DH), lambda c: (0, c, vo_idx)),
                  pl.BlockSpec((nb, C, LANE), lambda c: (0, c, 0)),
                  pl.BlockSpec((None, 1, LANE), lambda c: (l, 0, 0)),
                  pl.BlockSpec((None, 1, H * DH), lambda c: (l, 0, 0)),
                  pl.BlockSpec((nb, H, DH, DH), lambda c: (0, 0, 0, 0)),
                  pl.BlockSpec((nb, SUBLANE, LANE), lambda c: (0, 0, 0))],
        out_specs=[pl.BlockSpec((nb, C, H * DH), lambda c: (0, c, 0)),
                   pl.BlockSpec((nb, H, DH, DH), lambda c: (0, 0, 0, 0)),
                   pl.BlockSpec((nb, SUBLANE, LANE), lambda c: (0, 0, 0))],
        out_shape=[jax.ShapeDtypeStruct((nb, seq, H * DH), out_dtype),
                   jax.ShapeDtypeStruct((nb, H, DH, DH), F32),
                   jax.ShapeDtypeStruct((nb, SUBLANE, LANE), F32)],
        scratch_shapes=[pltpu.VMEM((nb, H, DH, DH), F32), pltpu.VMEM((nb, SUBLANE, LANE), F32)],
        compiler_params=_cparams(("arbitrary",)), name="mlstm",
    )(z3, z3, zsm3, bif, gn, c0, nm0)
    return o.reshape(nb * seq, H * DH), cc, nm


def _rank_select(score, n_blocks, keep):
    lane = lax.broadcasted_iota(jnp.int32, score.shape, 1)
    cnt = jnp.zeros(score.shape, jnp.int32)
    for j in range(n_blocks):
        cj = score[:, j:j + 1]
        ahead = jnp.where(cj > score, 1, jnp.where(cj == score, jnp.where(lane > j, 1, 0), 0))
        cnt = cnt + ahead
    return cnt < keep


def _nsa_prompt_kernel(q_ref, kv_ref, small_ref, o_ref, kvb_ref, cmp_ref, *, T, TQ, DH, g_lane):
    qi = pl.program_id(1)
    G, R = NSA_KV, NSA_R
    gw = G * DH
    ncb = T // CMP_BLOCK
    nsb = T // SEL_BLOCK
    ratio = SEL_BLOCK // CMP_BLOCK
    slopes = _alibi_slopes()

    @pl.when(qi == 0)
    def _():
        kvb_ref[...] = _bf(kv_ref[...])
        hi, lo = _split2(kv_ref[:, 0:2 * gw])
        jj = lax.broadcasted_iota(jnp.int32, (ncb, T), 0) * CMP_BLOCK
        tt = lax.broadcasted_iota(jnp.int32, (ncb, T), 1)
        pm = jnp.where((tt >= jj) & (tt < jj + CMP_BLOCK), 1.0 / CMP_BLOCK, 0.0).astype(BF16)
        cmp_ref[...] = _bf(_dot(pm, hi) + _dot(pm, lo))

    q = q_ref[...]
    gate = _sigmoid(small_ref[...])
    posl = qi * TQ + lax.broadcasted_iota(jnp.int32, (1, TQ), 1)
    cbs = lax.broadcasted_iota(jnp.int32, (ncb, 1), 0)
    cvalid = (cbs * CMP_BLOCK + (CMP_BLOCK - 1)) <= posl
    cdist = posl.astype(F32) - (cbs.astype(F32) * CMP_BLOCK + (CMP_BLOCK - 1) / 2.0)
    sbs = lax.broadcasted_iota(jnp.int32, (nsb, 1), 0)
    svalid = sbs * SEL_BLOCK <= posl
    forced = (sbs == 0) | ((sbs * SEL_BLOCK <= posl) & (posl < sbs * SEL_BLOCK + SEL_BLOCK))
    pi = lax.broadcasted_iota(jnp.int32, (nsb, ncb), 0) * ratio
    pj = lax.broadcasted_iota(jnp.int32, (nsb, ncb), 1)
    pool = jnp.where((pj >= pi) & (pj < pi + ratio), 1.0, 0.0).astype(BF16)
    brow = lax.broadcasted_iota(jnp.int32, (nsb, TQ), 0)
    row2 = lax.broadcasted_iota(jnp.int32, (R * TQ, 1), 0)
    rr = lax.broadcasted_iota(jnp.int32, (TQ, TQ), 0)
    cc = lax.broadcasted_iota(jnp.int32, (TQ, TQ), 1)
    lower = rr >= cc
    lower2 = jnp.concatenate([lower] * R, axis=0)
    col = lax.broadcasted_iota(jnp.int32, (1, TQ), 1)
    nfar = WINDOW // TQ

    def tile(carry, kt, q2, slope2, kcol, vcol, mask):
        m, l, acc = carry
        k0 = pl.multiple_of(kt * TQ, TQ)
        ks = kvb_ref[pl.ds(k0, TQ), kcol:kcol + DH]
        vs = kvb_ref[pl.ds(k0, TQ), vcol:vcol + DH]
        s = _dot_nt(q2, ks) + slope2 * (k0 + col).astype(F32)
        if mask is not None:
            s = jnp.where(mask, s, NEG)
        mn = jnp.maximum(m, jnp.max(s, axis=1, keepdims=True))
        p = jnp.exp(s - mn)
        alpha = jnp.exp(m - mn)
        return (mn, alpha * l + jnp.sum(p, axis=1, keepdims=True), alpha * acc + _dot(_bf(p), vs))

    def sel_mask(sel, kt):
        kpos = kt * TQ + col
        expand = _bf(jnp.where((kpos >= sbs * SEL_BLOCK) & (kpos < sbs * SEL_BLOCK + SEL_BLOCK), 1.0, 0.0))
        return _dot_tn(sel, expand) > 0.5

    init = (jnp.full((R * TQ, 1), NEG, F32), jnp.zeros((R * TQ, 1), F32), jnp.zeros((R * TQ, DH), F32))
    o_cmp, sels, q2s, slope2s, sel_carry, o_win = [], [], [], [], [], []
    for g in range(G):
        kc = cmp_ref[:, g * DH:(g + 1) * DH]
        vc = cmp_ref[:, gw + g * DH:gw + (g + 1) * DH]
        qs = []
        imp = jnp.zeros((nsb, TQ), F32)
        for r in range(R):
            hd = g * R + r
            qh = _bf(q[:, hd * DH:(hd + 1) * DH] * (DH ** -0.5))
            qs.append(qh)
            s = jnp.where(cvalid, _dot_nt(kc, qh) - slopes[hd] * cdist, NEG)
            e = jnp.exp(s - jnp.max(s, axis=0, keepdims=True))
            p = jnp.where(cvalid, e / jnp.sum(e, axis=0, keepdims=True), 0.0)
            o_cmp.append(_dot_tn(_bf(p), vc))
            ph, plo = _split2(p)
            imp = imp + _dot(pool, ph) + _dot(pool, plo)
        score = jnp.where(svalid, jnp.where(forced, FORCE_SCORE, imp), -1.0)
        ahead = jnp.zeros((nsb, TQ), jnp.int32)
        for j in range(nsb):
            sj = score[j:j + 1, :]
            ahead = ahead + jnp.where(sj > score, 1, jnp.where(sj == score, jnp.where(brow > j, 1, 0), 0))
        sel = _bf(jnp.where((ahead < SEL_TOPK) & (score >= 0.0), 1.0, 0.0))
        q2 = jnp.concatenate(qs, axis=0)
        slope2 = slopes[g * R]
        for r in range(1, R):
            slope2 = jnp.where(row2 >= r * TQ, slopes[g * R + r], slope2)
        sels.append(sel)
        q2s.append(q2)
        slope2s.append(slope2)
        sel_carry.append(tile(init, qi, q2, slope2, 2 * gw + g * DH, 3 * gw + g * DH,
                              jnp.concatenate([sel_mask(sel, qi) & lower] * R, axis=0)))
        kcol, vcol = 4 * gw + g * DH, 5 * gw + g * DH
        carry = tile(init, qi, q2, slope2, kcol, vcol, lower2)
        for back in range(1, nfar + 1):
            kt = jnp.maximum(qi - back, 0)
            live = qi >= back
            mask = jnp.logical_and(jnp.logical_not(lower2), live) if back == nfar else live
            carry = tile(carry, kt, q2, slope2, kcol, vcol, mask)
        o_win.append(carry[2] / carry[1])

    def sel_tiles(kt, carries):
        return tuple(tile(carries[g], kt, q2s[g], slope2s[g], 2 * gw + g * DH, 3 * gw + g * DH,
                          jnp.concatenate([sel_mask(sels[g], kt)] * R, axis=0)) for g in range(G))

    sel_carry = lax.fori_loop(0, qi, sel_tiles, tuple(sel_carry))
    outs = [None] * (G * R)
    for g in range(G):
        o_slc = sel_carry[g][2] / sel_carry[g][1]
        for r in range(R):
            hd = g * R + r
            gc = gate[:, g_lane + hd:g_lane + hd + 1]
            gs = gate[:, g_lane + NSA_HEADS + hd:g_lane + NSA_HEADS + hd + 1]
            gwn = gate[:, g_lane + 2 * NSA_HEADS + hd:g_lane + 2 * NSA_HEADS + hd + 1]
            outs[hd] = gc * o_cmp[hd] + gs * o_slc[r * TQ:(r + 1) * TQ] + gwn * o_win[g][r * TQ:(r + 1) * TQ]
    o_ref[...] = jnp.concatenate(outs, axis=1).astype(o_ref.dtype)


def _nsa_prompt_call(z, zsm, q_idx, kv_idx, nb, seq, g_lane, tq=256):
    dh = LANE
    kvw = 6 * NSA_KV * dh
    qw = NSA_HEADS * dh
    nq = seq // tq
    assert WINDOW % tq == 0 and seq % tq == 0
    kern = functools.partial(_nsa_prompt_kernel, T=seq, TQ=tq, DH=dh, g_lane=g_lane)
    return pl.pallas_call(
        kern, grid=(nb, nq),
        in_specs=[pl.BlockSpec((tq, qw), lambda b, i: (b * nq + i, q_idx)),
                  pl.BlockSpec((seq, kvw), lambda b, i: (b, kv_idx)),
                  pl.BlockSpec((tq, LANE), lambda b, i: (b * nq + i, 0))],
        out_specs=pl.BlockSpec((tq, qw), lambda b, i: (b * nq + i, 0)),
        out_shape=jax.ShapeDtypeStruct((nb * seq, qw), BF16),
        scratch_shapes=[pltpu.VMEM((seq, kvw), BF16), pltpu.VMEM((seq // CMP_BLOCK, 2 * NSA_KV * dh), BF16)],
        compiler_params=_cparams(("arbitrary", "arbitrary")), name="nsa_prompt",
    )(z, z, zsm)


def _nsa_means_kernel(pt_ref, *refs, PP, DH):
    ins = refs[:PP]
    o_ref = refs[PP]
    nper = PAGE_SIZE // CMP_BLOCK
    for i in range(PP):
        x = ins[i][...].reshape(nper, CMP_BLOCK, 2, NSA_KV, DH)
        o_ref[i * nper:(i + 1) * nper] = jnp.sum(x, axis=1) * (1.0 / CMP_BLOCK)


def _nsa_means_call(cache, page_table, l, pp=8):
    nb, n_pages = page_table.shape
    dh = cache.shape[5]
    nper = PAGE_SIZE // CMP_BLOCK
    assert cache.shape[2] == PAGE_SIZE and cache.shape[4] == NSA_KV and n_pages % pp == 0
    in_specs = [pl.BlockSpec((None, None, PAGE_SIZE, 2, NSA_KV, dh),
                             lambda b, s, pt, i=i: (l, pt[b, s * pp + i], 0, 0, 0, 0)) for i in range(pp)]
    gs = pltpu.PrefetchScalarGridSpec(
        num_scalar_prefetch=1, grid=(nb, n_pages // pp), in_specs=in_specs,
        out_specs=pl.BlockSpec((None, nper * pp, 2, NSA_KV, dh), lambda b, s, pt: (b, s, 0, 0, 0)))
    means = pl.pallas_call(
        functools.partial(_nsa_means_kernel, PP=pp, DH=dh), grid_spec=gs,
        out_shape=jax.ShapeDtypeStruct((nb, n_pages * nper, 2, NSA_KV, dh), F32),
        compiler_params=_cparams(("arbitrary", "arbitrary")), name="nsa_cmp_means",
    )(page_table, *([cache] * pp))
    return means.reshape(nb, n_pages * nper, 2 * NSA_KV * dh)


def _nsa_score_kernel(q_ref, kvc_ref, ocmp_ref, ids_ref, *, DH, qpos):
    G, R = NSA_KV, NSA_R
    gw = G * DH
    ncb = kvc_ref.shape[0]
    ratio = SEL_BLOCK // CMP_BLOCK
    nsb = ncb // ratio
    slopes = _alibi_slopes()
    q = q_ref[...]
    kvc = _bf(kvc_ref[...])
    rows = q.shape[0]
    cb = lax.broadcasted_iota(jnp.int32, (1, ncb), 1)
    cvalid = (cb * CMP_BLOCK + (CMP_BLOCK - 1)) <= qpos
    cdist = float(qpos) - (cb.astype(F32) * CMP_BLOCK + (CMP_BLOCK - 1) / 2.0)
    pi = lax.broadcasted_iota(jnp.int32, (ncb, nsb), 0)
    pj = lax.broadcasted_iota(jnp.int32, (ncb, nsb), 1) * ratio
    pool = jnp.where((pi >= pj) & (pi < pj + ratio), 1.0, 0.0).astype(BF16)
    lane = lax.broadcasted_iota(jnp.int32, (rows, nsb), 1).astype(F32)
    lane_out = lax.broadcasted_iota(jnp.int32, (rows, LANE), 1)
    o_cmp = []
    id_rows = []
    for g in range(G):
        kc = kvc[:, g * DH:(g + 1) * DH]
        vc = kvc[:, gw + g * DH:gw + (g + 1) * DH]
        imp = jnp.zeros((rows, nsb), F32)
        for r in range(R):
            hd = g * R + r
            qh = _bf(q[:, hd * DH:(hd + 1) * DH] * (DH ** -0.5))
            s = jnp.where(cvalid, _dot_nt(qh, kc) - slopes[hd] * cdist, NEG)
            p = jnp.where(cvalid, _softmax_rows(s), 0.0)
            o_cmp.append(_dot(_bf(p), vc))
            ph, plo = _split2(p)
            imp = imp + _dot(ph, pool) + _dot(plo, pool)
        val = jnp.where(lane == 0.0, -1.0, imp)
        ids = jnp.zeros((rows, LANE), F32)
        for it in range(SEL_TOPK - 2):
            mx = jnp.max(val, axis=1, keepdims=True)
            ix = jnp.min(jnp.where(val == mx, lane, float(nsb)), axis=1, keepdims=True)
            ids = jnp.where(lane_out == it + 1, ix, ids)
            val = jnp.where(lane == ix, -2.0, val)
        id_rows.append(ids[0:1, :])
    ocmp_ref[...] = jnp.concatenate(o_cmp, axis=1)
    pad = jnp.zeros((ids_ref.shape[0] - G, LANE), F32)
    ids_ref[...] = jnp.concatenate(id_rows + [pad], axis=0).astype(jnp.int32)


def _nsa_score_call(z_s, q_idx, kvc, qpos):
    nb, ncb, w = kvc.shape
    dh = w // (2 * NSA_KV)
    qw = NSA_HEADS * dh
    return pl.pallas_call(
        functools.partial(_nsa_score_kernel, DH=dh, qpos=qpos), grid=(nb,),
        in_specs=[pl.BlockSpec((SROWS, qw), lambda b: (b, q_idx)),
                  pl.BlockSpec((None, ncb, w), lambda b: (b, 0, 0))],
        out_specs=[pl.BlockSpec((SROWS, qw), lambda b: (b, 0)),
                   pl.BlockSpec((None, SUBLANE, LANE), lambda b: (b, 0, 0))],
        out_shape=[jax.ShapeDtypeStruct((nb * SROWS, qw), F32),
                   jax.ShapeDtypeStruct((nb, SUBLANE, LANE), jnp.int32)],
        compiler_params=_cparams(("arbitrary",)), name="nsa_cmp_score",
    )(z_s, kvc)


def _nsa_gather_kernel(sel_ref, q_ref, new_ref, small_ref, ocmp_ref, win_ref, *rest,
                       DH, NSEL, qpos, g_lane):
    kv = rest[:NSA_KV * NSEL]
    o_ref = rest[NSA_KV * NSEL]
    b = pl.program_id(0)
    G, R = NSA_KV, NSA_R
    gw = G * DH
    slopes = _alibi_slopes()
    q = q_ref[...]
    new = new_ref[...]
    gate = _sigmoid(small_ref[...])
    ocmp = ocmp_ref[...]
    wlen = win_ref.shape[0]
    jw = lax.broadcasted_iota(jnp.int32, (1, wlen), 1)
    distw = wlen - jw
    validw = (distw < WINDOW) & (qpos - distw >= 0)
    off = lax.broadcasted_iota(jnp.int32, (1, SEL_BLOCK), 1)
    outs = []
    for g in range(G):
        ksn = _bf(new[:, 2 * gw + g * DH:2 * gw + (g + 1) * DH]).astype(F32)
        vsn = _bf(new[:, 3 * gw + g * DH:3 * gw + (g + 1) * DH]).astype(F32)
        kwn = _bf(new[:, 4 * gw + g * DH:4 * gw + (g + 1) * DH]).astype(F32)
        vwn = _bf(new[:, 5 * gw + g * DH:5 * gw + (g + 1) * DH]).astype(F32)
        kw = _bf(win_ref[:, 0, g, :])
        vw = _bf(win_ref[:, 1, g, :])
        kblk = [_bf(kv[g * NSEL + i][:, 0, g, :]) for i in range(NSEL)]
        vblk = [_bf(kv[g * NSEL + i][:, 1, g, :]) for i in range(NSEL)]
        nslot = NSA_KV * SEL_TOPK
        dists = [qpos - (sel_ref[b, 2 * nslot + g * SEL_TOPK + i] * SEL_BLOCK + off) for i in range(NSEL)]
        for r in range(R):
            hd = g * R + r
            slope = slopes[hd]
            qh = _bf(q[:, hd * DH:(hd + 1) * DH] * (DH ** -0.5))
            qf = qh.astype(F32)
            s_list = [jnp.where(dists[i] >= 0, _dot_nt(qh, kblk[i]) - slope * dists[i].astype(F32), NEG)
                      for i in range(NSEL)]
            s_new = jnp.sum(qf * ksn, axis=1, keepdims=True)
            mx = s_new
            for s in s_list:
                mx = jnp.maximum(mx, jnp.max(s, axis=1, keepdims=True))
            p_new = jnp.exp(s_new - mx)
            den = p_new
            acc = p_new * vsn
            for i, s in enumerate(s_list):
                p = jnp.exp(s - mx)
                den = den + jnp.sum(p, axis=1, keepdims=True)
                acc = acc + _dot(_bf(p), vblk[i])
            o_slc = acc / den
            s = jnp.where(validw, _dot_nt(qh, kw) - slope * distw.astype(F32), NEG)
            s_new = jnp.sum(qf * kwn, axis=1, keepdims=True)
            mx = jnp.maximum(s_new, jnp.max(s, axis=1, keepdims=True))
            p = jnp.exp(s - mx)
            p_new = jnp.exp(s_new - mx)
            o_win = (_dot(_bf(p), vw) + p_new * vwn) / (jnp.sum(p, axis=1, keepdims=True) + p_new)
            gc, gs, gwn = [gate[:, g_lane + c * NSA_HEADS + hd:g_lane + c * NSA_HEADS + hd + 1] for c in range(3)]
            outs.append(gc * ocmp[:, hd * DH:(hd + 1) * DH] + gs * o_slc + gwn * o_win)
    o_ref[...] = jnp.concatenate(outs, axis=1)


def _nsa_gather_call(z_s, zsm_s, q_idx, kv_idx, ocmp, win, cache, page_table, ids, l, qpos, g_lane):
    nb = page_table.shape[0]
    dh = cache.shape[5]
    G = NSA_KV
    qw = NSA_HEADS * dh
    nsel = SEL_TOPK - 1
    bpp = PAGE_SIZE // SEL_BLOCK
    wlen = win.shape[2]
    nslot = G * SEL_TOPK
    sel = jnp.concatenate([jnp.take_along_axis(page_table, ids // bpp, axis=1), ids % bpp, ids], axis=1)
    in_specs = [pl.BlockSpec((SROWS, qw), lambda b, s: (b, q_idx)),
                pl.BlockSpec((SROWS, 6 * G * dh), lambda b, s: (b, kv_idx)),
                pl.BlockSpec((SROWS, LANE), lambda b, s: (b, 0)),
                pl.BlockSpec((SROWS, qw), lambda b, s: (b, 0)),
                pl.BlockSpec((None, None, wlen, 2, G, dh), lambda b, s: (l, b, 0, 0, 0, 0))]
    for g in range(G):
        for i in range(nsel):
            in_specs.append(pl.BlockSpec(
                (None, None, SEL_BLOCK, 2, G, dh),
                lambda b, s, j=g * SEL_TOPK + i: (l, s[b, j], s[b, nslot + j], 1, 0, 0)))
    gs = pltpu.PrefetchScalarGridSpec(
        num_scalar_prefetch=1, grid=(nb,), in_specs=in_specs,
        out_specs=pl.BlockSpec((SROWS, qw), lambda b, s: (b, 0)))
    kern = functools.partial(_nsa_gather_kernel, DH=dh, NSEL=nsel, qpos=qpos, g_lane=g_lane)
    return pl.pallas_call(
        kern, grid_spec=gs,
        out_shape=jax.ShapeDtypeStruct((nb * SROWS, qw), F32),
        compiler_params=_cparams(("arbitrary",)), name="nsa_sel_win",
    )(sel, z_s, z_s, zsm_s, ocmp, win, *([cache] * (G * nsel)))


def _split_cols_kernel(*refs, n_in, G, DH):
    z_refs = refs[:n_in]
    o_ref = refs[-1]
    per = z_refs[0].shape[1] // DH
    for idx in range(n_in * per):
        o_ref[:, idx // G, idx % G, :] = z_refs[idx // per][:, (idx % per) * DH:(idx % per + 1) * DH]


def _split_cols(z, col0, ncomp, prev, l, depth, nblk, rows, row_block, G, dh):
    width = ncomp * G * dh
    n_in = max(1, width // Z_TILE)
    wblk = width // n_in
    assert col0 % wblk == 0
    kern = functools.partial(_split_cols_kernel, n_in=n_in, G=G, DH=dh)
    in_specs = [pl.BlockSpec((rows, wblk), lambda j, k=k: (row_block(j), col0 // wblk + k)) for k in range(n_in)]
    args = [z] * n_in
    aliases = {}
    if prev is not None:
        in_specs.append(pl.BlockSpec(memory_space=pl.ANY))
        args.append(prev)
        aliases = {n_in: 0}
    return pl.pallas_call(
        kern, grid=(nblk,), in_specs=in_specs,
        out_specs=pl.BlockSpec((None, None, rows, ncomp, G, dh), lambda j: (l, j, 0, 0, 0, 0)),
        out_shape=jax.ShapeDtypeStruct((depth, nblk, rows, ncomp, G, dh), z.dtype),
        input_output_aliases=aliases,
        compiler_params=_cparams(("arbitrary",)), name="nsa_rows_out",
    )(*args)


def _lane_vec(depth, lane, values):
    n = values.shape[1]
    return jnp.zeros((depth, 1, LANE), F32).at[:, 0, lane:lane + n].set(values.astype(F32))


def kernel(x_prompt, x_sample, cache_nsa_kv, cache_nsa_win, state_gla, state_ssd, state_ssd_conv, state_mlstm_c, state_mlstm_n, state_mlstm_m, page_table, c_prompt, c_sample, ada_w, ada_b, norm_mix_g, norm_ffn_g, w_in, gla_w_a, gla_b_a, gla_norm_g, ssd_conv_w, ssd_conv_b, ssd_dt_bias, ssd_a_log, ssd_d, ssd_norm_g, ml_b_i, ml_b_f, ml_norm_g, w_branch, w_out, ffn_w_gate, ffn_w_up, ffn_w_down, final_norm_g):
    nbp, seq, d = x_prompt.shape
    nbs, dec_seq, _ = x_sample.shape
    assert dec_seq == 1 and seq % CHUNK == 0
    depth = w_in.shape[0]
    past_len = page_table.shape[1] * PAGE_SIZE
    bw = d // 4
    dh = bw // NSA_HEADS
    gw = NSA_KV * dh
    src, dst, n_packed, _ = _packed_offsets(d)
    kv_off = dst['nsa_kv']
    q_idx = dst['nsa_q'] // bw
    kv_idx = kv_off // (6 * gw)
    xbc_idx = dst['ssd_xbc'] // src['ssd_xbc'][1]
    zg_idx = dst['ssd_z'] // bw
    mlqk_idx = dst['ml_q'] // (2 * bw)
    mlvo_idx = dst['ml_v'] // (2 * bw)
    for name, width in (('gla_q', 3 * bw), ('nsa_q', bw), ('nsa_kv', 6 * gw), ('ssd_xbc', src['ssd_xbc'][1]),
                        ('ssd_z', bw), ('ml_q', 2 * bw), ('ml_v', 2 * bw), ('merge', Z_TILE)):
        assert dst[name] % width == 0
    assert dst['gla_q'] == 0
    g_lane, dt_lane, i_lane, f_lane = dst['nsa_g'], dst['ssd_dt'], dst['ml_i'], dst['ml_f']

    crow = 2 * SUBLANE
    c_all = jnp.concatenate([c_prompt, c_sample, jnp.zeros((crow - nbp - nbs, d), F32)], axis=0)
    mod = _ada(c_all, ada_w, ada_b)
    modp = mod[:, :nbp].reshape(depth, nbp, 1, 6 * d)
    mods = jnp.repeat(mod[:, nbp:nbp + nbs], SROWS, axis=1)

    w_mix, w_gate, w_small = _pack_w_in(w_in)
    n_mix = dst['merge']
    assert n_mix % (4 * LANE) == 0
    r3 = lambda a: a.reshape(depth, 1, a.shape[-1])
    wa_pad = jnp.zeros((depth, LANE, gla_w_a.shape[2]), F32).at[:, dst['gla_a']:dst['gla_a'] + GLA_RANK].set(gla_w_a)
    dtb = _lane_vec(depth, dt_lane, ssd_dt_bias)
    alog = _lane_vec(depth, dt_lane, ssd_a_log)
    dvec = jnp.repeat(ssd_d, bw // SSD_HEADS, axis=1).reshape(depth, 1, bw)
    bif = _lane_vec(depth, i_lane, ml_b_i) + _lane_vec(depth, f_lane, ml_b_f)

    xp = x_prompt.reshape(nbp * seq, d)
    xs = jnp.zeros((nbs, SROWS, d), F32).at[:, 0].set(x_sample[:, 0]).reshape(nbs * SROWS, d)

    zeros_like_p = lambda a: jnp.zeros((nbp,) + a.shape[2:], F32)
    gla0, ssd0, conv0, mlc0 = (zeros_like_p(a) for a in (state_gla, state_ssd, state_ssd_conv, state_mlstm_c))
    nm0 = jnp.zeros((nbp, SUBLANE, LANE), F32)
    nm_s = (jnp.zeros((depth, nbs, SUBLANE, LANE), F32).at[:, :, :ML_HEADS, :].set(state_mlstm_n)
            .at[:, :, ML_HEADS, :ML_HEADS].set(state_mlstm_m))

    out = {k: [] for k in ('rows_s', 'win_s', 'gla_p', 'gla_s', 'ssd_p', 'ssd_s', 'conv_p',
                           'conv_s', 'mlc_p', 'mlc_s', 'mln_p', 'mln_s', 'mlm_p', 'mlm_s')}
    rows_p = win_p = None
    rows_tile = 1024
    keep = min(WINDOW, seq)
    assert seq % keep == 0 and (nbp * seq) % rows_tile == 0
    for l in range(depth):
        hp = _norm_mod_prompt(xp, r3(norm_mix_g), modp, l, 0, 1, seq)
        hs = _norm_mod_sample(xs, r3(norm_mix_g), mods, l, 0, 1)
        zp, zs = _proj_nt(hp, hs, w_mix, l, 1024, n_mix // 4, "proj_in")
        zgp, zgs = _proj_nt(hp, hs, w_small, l, 2048, LANE, "proj_small")
        gtp, gts = _proj_nt(hp, hs, w_gate, l, 1024, d, "proj_gate", BF16)

        o_gla_p, gla_p = _gla_call(zp, zgp, wa_pad, r3(gla_b_a), r3(gla_norm_g), gla0, l, nbp, seq, CHUNK,
                                   CHUNK, BF16)
        o_gla_s, gla_s = _gla_call(zs, zgs, wa_pad, r3(gla_b_a), r3(gla_norm_g), state_gla[l], l, nbs, SROWS,
                                   SROWS, 1, F32)
        ssd_args = (ssd_conv_w, r3(ssd_conv_b), dtb, alog, dvec, r3(ssd_norm_g))
        o_ssd_p, ssd_p, conv_p = _ssd_call(zp, zgp, xbc_idx, zg_idx, *ssd_args, ssd0, conv0, l, nbp, seq,
                                           CHUNK, CHUNK, dt_lane, BF16)
        o_ssd_s, ssd_s, conv_s = _ssd_call(zs, zgs, xbc_idx, zg_idx, *ssd_args, state_ssd[l],
                                           state_ssd_conv[l], l, nbs, SROWS, SROWS, 1, dt_lane, F32)
        o_ml_p, mlc_p, mlnm_p = _mlstm_call(zp, zgp, mlqk_idx, mlvo_idx, bif, r3(ml_norm_g), mlc0, nm0, l,
                                            nbp, seq, CHUNK, CHUNK, i_lane, f_lane, BF16)
        o_ml_s, mlc_s, mlnm_s = _mlstm_call(zs, zgs, mlqk_idx, mlvo_idx, bif, r3(ml_norm_g), state_mlstm_c[l],
                                            nm_s[l], l, nbs, SROWS, SROWS, 1, i_lane, f_lane, F32)
        o_nsa_p = _nsa_prompt_call(zp, zgp, q_idx, kv_idx, nbp, seq, g_lane)
        kvc = _nsa_means_call(cache_nsa_kv, page_table, l)
        ocmp, ids = _nsa_score_call(zs, q_idx, kvc, past_len)
        ids2 = ids[:, :NSA_KV, :SEL_TOPK].reshape(nbs, NSA_KV * SEL_TOPK)
        o_nsa_s = _nsa_gather_call(zs, zgs, q_idx, kv_idx, ocmp, cache_nsa_win, cache_nsa_kv, page_table, ids2, l,
                                   past_len, g_lane)

        mp, ms = _merge((o_gla_p, o_nsa_p, o_ssd_p, o_ml_p), (o_gla_s, o_nsa_s, o_ssd_s, o_ml_s), gtp, gts, w_branch,
                        l, 1024, 512)
        xp, xs = _mm_resid(mp, ms, w_out, xp, xs, modp, mods, l, 2, seq, 1024, 512, "proj_out")
        hp = _norm_mod_prompt(xp, r3(norm_ffn_g), modp, l, 3, 4, seq)
        hs = _norm_mod_sample(xs, r3(norm_ffn_g), mods, l, 3, 4)
        ap, a_s = _mm_swiglu(hp, hs, ffn_w_gate, ffn_w_up, l, 1024, 512)
        xp, xs = _mm_resid(ap, a_s, ffn_w_down, xp, xs, modp, mods, l, 5, seq, 512, 512, "ffn_down")

        zs3 = zs.reshape(nbs, SROWS, n_mix)[:, :1]
        rows_p = _split_cols(zp, kv_off, 4, rows_p, l, depth, nbp * seq // rows_tile, rows_tile, lambda j: j,
                             NSA_KV, dh)
        win_p = _split_cols(zp, kv_off + 4 * gw, 2, win_p, l, depth, nbp, keep,
                            lambda j: (j + 1) * (seq // keep) - 1, NSA_KV, dh)
        out['rows_s'].append(zs3[:, :, kv_off:kv_off + 4 * gw].reshape(nbs, 1, 4, NSA_KV, dh))
        win_all = jnp.concatenate([cache_nsa_win[l], zs3[:, :, kv_off + 4 * gw:kv_off + 6 * gw]
                                   .reshape(nbs, 1, 2, NSA_KV, dh)], axis=1)
        out['win_s'].append(win_all[:, win_all.shape[1] - min(WINDOW, win_all.shape[1]):])
        for key, val in (('gla_p', gla_p), ('gla_s', gla_s), ('ssd_p', ssd_p), ('ssd_s', ssd_s), ('conv_p', conv_p),
                         ('conv_s', conv_s), ('mlc_p', mlc_p), ('mlc_s', mlc_s)):
            out[key].append(val)
        for tag, nm in (('p', mlnm_p), ('s', mlnm_s)):
            out['mln_' + tag].append(nm[:, :ML_HEADS, :])
            out['mlm_' + tag].append(nm[:, ML_HEADS, :ML_HEADS])

    y_prompt = _final_norm(xp, final_norm_g).reshape(nbp, seq, d)
    y_sample = _final_norm(xs, final_norm_g).reshape(nbs, SROWS, d)[:, :1]
    st = {k: jnp.stack(v) for k, v in out.items()}
    st['rows_p'] = rows_p.reshape(depth, nbp, seq, 4, NSA_KV, dh)
    st['win_p'] = win_p
    return (y_prompt, y_sample, st['rows_p'], st['rows_s'], st['win_p'], st['win_s'], st['gla_p'], st['gla_s'],
            st['ssd_p'], st['ssd_s'], st['conv_p'], st['conv_s'], st['mlc_p'], st['mlc_s'], st['mln_p'], st['mln_s'],
            st['mlm_p'], st['mlm_s'])
```

```python
import functools
import math

import jax
import jax.numpy as jnp
from jax import lax
from jax.experimental import pallas as pl
from jax.experimental.pallas import tpu as pltpu

F32 = jnp.float32
BF16 = jnp.bfloat16

PAGE_SIZE = 128
GLA_HEADS = 4
GLA_RANK = 16
GLA_TAU = 16.0
NSA_HEADS = 4
NSA_KV = 2
NSA_R = NSA_HEADS // NSA_KV
CMP_BLOCK = 32
SEL_BLOCK = 64
SEL_TOPK = 16
WINDOW = 512
FORCE_SCORE = 1e4
NEG = -1e30
SSD_HEADS = 8
SSD_GROUPS = 2
SSD_N = 128
SSD_CONV = 4
ML_HEADS = 4
CHUNK = 128
EPS = 1e-6

LANE = 128
SUBLANE = 8
VMEM_LIMIT = 56 * 1024 * 1024

SROWS = SUBLANE


def _cparams(sem):
    return pltpu.CompilerParams(dimension_semantics=sem, vmem_limit_bytes=VMEM_LIMIT)


def _layout(d_model):
    bw = d_model // 4
    gdk = (bw // GLA_HEADS) // 2
    conv_ch = bw + 2 * SSD_GROUPS * SSD_N
    splits = (('gla_q', GLA_HEADS * gdk), ('gla_k', GLA_HEADS * gdk), ('gla_v', bw), ('gla_r', bw),
              ('gla_a', GLA_RANK), ('nsa_q', bw), ('nsa_kv', 6 * NSA_KV * (bw // NSA_HEADS)),
              ('nsa_g', 3 * NSA_HEADS), ('ssd_z', bw), ('ssd_xbc', conv_ch), ('ssd_dt', SSD_HEADS),
              ('ml_q', bw), ('ml_k', bw), ('ml_v', bw), ('ml_o', bw), ('ml_i', ML_HEADS), ('ml_f', ML_HEADS),
              ('merge', 4 * d_model))
    src = {}
    off = 0
    for name, w in splits:
        src[name] = (off, w)
        off += w
    return src, off


_Z_ORDER = ('gla_q', 'gla_k', 'gla_v', 'gla_r', 'nsa_kv', 'ml_q', 'ml_k', 'ml_v', 'ml_o', 'ssd_xbc',
            'nsa_q', 'ssd_z', 'merge')
_Z_SMALL = ('gla_a', 'nsa_g', 'ssd_dt', 'ml_i', 'ml_f')


Z_TILE = 512
SMALL_TILE = LANE // 4


def _packed_offsets(d_model):
    src, d_in = _layout(d_model)
    dst = {}
    off = 0
    for name in _Z_ORDER:
        dst[name] = off
        off += src[name][1]
    starts = []
    for name in _Z_SMALL:
        o, w = src[name]
        if not starts or o + w > starts[-1] + SMALL_TILE:
            starts.append(o)
        dst[name] = (len(starts) - 1) * SMALL_TILE + o - starts[-1]
    assert len(starts) * SMALL_TILE <= LANE and starts[-1] + SMALL_TILE <= d_in
    dst['_small_starts'] = starts
    return src, dst, off, d_in


def _pack_rows_kernel(rows_ref, w_ref, o_ref):
    o_ref[...] = pltpu.einshape("nlk->lnk", w_ref[...]).astype(o_ref.dtype)


def _pack_rows(w_t, row_starts, tile):
    _, depth, d = w_t.shape
    nt = len(row_starts)
    gs = pltpu.PrefetchScalarGridSpec(
        num_scalar_prefetch=1, grid=(nt,),
        in_specs=[pl.BlockSpec((pl.Element(tile), pl.Element(depth), pl.Element(d)), lambda t, r: (r[t], 0, 0))],
        out_specs=pl.BlockSpec((depth, tile, d), lambda t, r: (0, t, 0)))
    return pl.pallas_call(
        _pack_rows_kernel, grid_spec=gs, out_shape=jax.ShapeDtypeStruct((depth, nt * tile, d), BF16),
        compiler_params=_cparams(("arbitrary",)), name="pack_w_in",
    )(jnp.asarray(row_starts, jnp.int32), w_t)


def _pack_w_in(w_in):
    depth, d_model, _ = w_in.shape
    src, dst, n_packed, d_in = _packed_offsets(d_model)
    assert w_in.shape[2] == d_in
    w_t = jnp.transpose(w_in, (2, 0, 1))
    tile = LANE

    def starts(names):
        out = []
        for n in names:
            assert src[n][1] % tile == 0
            out += [src[n][0] + i for i in range(0, src[n][1], tile)]
        return out

    mix = _pack_rows(w_t, starts(_Z_ORDER[:-1]), tile)
    gate = _pack_rows(w_t, starts(_Z_ORDER[-1:]), tile)
    small_starts = dst['_small_starts']
    small = _pack_rows(w_t, small_starts + [small_starts[-1]] * (LANE // SMALL_TILE - len(small_starts)), SMALL_TILE)
    return mix, gate, small


def _sigmoid(x):
    return 1.0 / (1.0 + jnp.exp(-x))


def _silu(x):
    return x * _sigmoid(x)


def _log_sigmoid(x):
    return jnp.minimum(x, 0.0) - jnp.log1p(jnp.exp(-jnp.abs(x)))


def _softplus(x):
    return jnp.maximum(x, 0.0) + jnp.log1p(jnp.exp(-jnp.abs(x)))


def _dot(a, b):
    return jnp.dot(a, b, preferred_element_type=F32)


def _dot_nt(a, b):
    return lax.dot_general(a, b, (((1,), (1,)), ((), ())), preferred_element_type=F32)


def _dot_tn(a, b):
    return lax.dot_general(a, b, (((0,), (0,)), ((), ())), preferred_element_type=F32)


def _bf(x):
    return x.astype(BF16)


def _split2(x):
    h = x.astype(BF16)
    return h, (x - h.astype(F32)).astype(BF16)


def _causal(c):
    r = lax.broadcasted_iota(jnp.int32, (c, c), 0)
    s = lax.broadcasted_iota(jnp.int32, (c, c), 1)
    return r >= s


def _cumsum_rows(x):
    n = x.shape[0]
    row = lax.broadcasted_iota(jnp.int32, (n, 1), 0)
    step = 1
    while step < n:
        x = x + jnp.where(row >= step, pltpu.roll(x, step, 0), 0.0)
        step *= 2
    return x


def _rms(x, g):
    return x * lax.rsqrt(jnp.mean(x * x, axis=-1, keepdims=True) + EPS) * g


def _softmax_rows(s):
    m = jnp.max(s, axis=-1, keepdims=True)
    e = jnp.exp(s - m)
    return e / jnp.sum(e, axis=-1, keepdims=True)


def _alibi_slopes():
    return [2.0 ** (-8.0 * (h + 1) / NSA_HEADS) for h in range(NSA_HEADS)]


def _ada_kernel(c_ref, w_ref, b_ref, o_ref):
    a = _bf(_silu(c_ref[...]))
    o_ref[...] = _dot(a, _bf(w_ref[...])) + b_ref[...]


def _ada(c_all, ada_w, ada_b):
    depth, d, n = ada_w.shape
    rows = c_all.shape[0]
    tn = 1024
    return pl.pallas_call(
        _ada_kernel, grid=(depth, n // tn),
        in_specs=[pl.BlockSpec((rows, d), lambda l, j: (0, 0)),
                  pl.BlockSpec((None, d, tn), lambda l, j: (l, 0, j)),
                  pl.BlockSpec((None, 1, tn), lambda l, j: (l, 0, j))],
        out_specs=pl.BlockSpec((None, rows, tn), lambda l, j: (l, 0, j)),
        out_shape=jax.ShapeDtypeStruct((depth, rows, n), F32),
        compiler_params=_cparams(("arbitrary", "arbitrary")), name="ada",
    )(c_all, ada_w, ada_b.reshape(depth, 1, n))


def _norm_mod_kernel(x_ref, g_ref, sc_ref, sh_ref, o_ref):
    y = _rms(x_ref[...], g_ref[...])
    o_ref[...] = (y * (1.0 + sc_ref[...]) + sh_ref[...]).astype(o_ref.dtype)


def _norm_kernel(x_ref, g_ref, o_ref):
    o_ref[...] = _rms(x_ref[...], g_ref[...]).astype(o_ref.dtype)


def _norm_mod_prompt(x, g, modp, l, j_shift, j_scale, seq):
    m, d = x.shape
    tm = 1024
    per = seq // tm
    return pl.pallas_call(
        _norm_mod_kernel, grid=(m // tm,),
        in_specs=[pl.BlockSpec((tm, d), lambda i: (i, 0)),
                  pl.BlockSpec((None, 1, d), lambda i: (l, 0, 0)),
                  pl.BlockSpec((None, None, 1, d), lambda i: (l, i // per, 0, j_scale)),
                  pl.BlockSpec((None, None, 1, d), lambda i: (l, i // per, 0, j_shift))],
        out_specs=pl.BlockSpec((tm, d), lambda i: (i, 0)),
        out_shape=jax.ShapeDtypeStruct((m, d), BF16),
        compiler_params=_cparams(("arbitrary",)), name="norm_mod_prompt",
    )(x, g, modp, modp)


def _norm_mod_sample(x, g, mods, l, j_shift, j_scale):
    m, d = x.shape
    return pl.pallas_call(
        _norm_mod_kernel, grid=(1,),
        in_specs=[pl.BlockSpec((m, d), lambda i: (0, 0)),
                  pl.BlockSpec((None, 1, d), lambda i: (l, 0, 0)),
                  pl.BlockSpec((None, m, d), lambda i: (l, 0, j_scale)),
                  pl.BlockSpec((None, m, d), lambda i: (l, 0, j_shift))],
        out_specs=pl.BlockSpec((m, d), lambda i: (0, 0)),
        out_shape=jax.ShapeDtypeStruct((m, d), BF16),
        compiler_params=_cparams(("arbitrary",)), name="norm_mod_sample",
    )(x, g, mods, mods)


def _final_norm(x, g):
    m, d = x.shape
    tm = min(m, 512)
    return pl.pallas_call(
        _norm_kernel, grid=(m // tm,),
        in_specs=[pl.BlockSpec((tm, d), lambda i: (i, 0)), pl.BlockSpec((1, d), lambda i: (0, 0))],
        out_specs=pl.BlockSpec((tm, d), lambda i: (i, 0)),
        out_shape=jax.ShapeDtypeStruct((m, d), F32),
        compiler_params=_cparams(("arbitrary",)), name="final_norm",
    )(x, g.reshape(1, d))


def _proj_nt_kernel(a_ref, as_ref, wt_ref, o_ref, os_ref):
    w = wt_ref[...]

    @pl.when(pl.program_id(1) == 0)
    def _():
        os_ref[...] = _dot_nt(as_ref[...], w).astype(os_ref.dtype)

    o_ref[...] = _dot_nt(a_ref[...], w).astype(o_ref.dtype)


def _proj_nt(a, a_s, w_t, l, tm, tn, name, out_dtype=F32):
    m, k = a.shape
    ms = a_s.shape[0]
    n = w_t.shape[1]
    return pl.pallas_call(
        _proj_nt_kernel, grid=(n // tn, m // tm),
        in_specs=[pl.BlockSpec((tm, k), lambda j, i: (i, 0)),
                  pl.BlockSpec((ms, k), lambda j, i: (0, 0)),
                  pl.BlockSpec((None, tn, k), lambda j, i: (l, j, 0))],
        out_specs=[pl.BlockSpec((tm, tn), lambda j, i: (i, j)),
                   pl.BlockSpec((ms, tn), lambda j, i: (0, j))],
        out_shape=[jax.ShapeDtypeStruct((m, n), out_dtype), jax.ShapeDtypeStruct((ms, n), out_dtype)],
        compiler_params=_cparams(("arbitrary", "arbitrary")), name=name,
    )(a, a_s, w_t)


def _mm_resid_kernel(a_ref, as_ref, w_ref, x_ref, xs_ref, gt_ref, gts_ref, o_ref, os_ref, wb_ref):
    @pl.when(pl.program_id(1) == 0)
    def _():
        wb_ref[...] = _bf(w_ref[...])
        os_ref[...] = xs_ref[...] + gts_ref[...] * _dot(as_ref[...], wb_ref[...])

    o_ref[...] = x_ref[...] + gt_ref[...] * _dot(a_ref[...], wb_ref[...])


def _mm_resid(a, a_s, w, x, x_s, modp, mods, l, j_gate, seq, tm, tn, name):
    m, k = a.shape
    ms = a_s.shape[0]
    n = w.shape[2]
    per = seq // tm
    nj = n // tn
    return pl.pallas_call(
        _mm_resid_kernel, grid=(nj, m // tm),
        in_specs=[pl.BlockSpec((tm, k), lambda j, i: (i, 0)),
                  pl.BlockSpec((ms, k), lambda j, i: (0, 0)),
                  pl.BlockSpec((None, k, tn), lambda j, i: (l, 0, j)),
                  pl.BlockSpec((tm, tn), lambda j, i: (i, j)),
                  pl.BlockSpec((ms, tn), lambda j, i: (0, j)),
                  pl.BlockSpec((None, None, 1, tn), lambda j, i: (l, i // per, 0, j_gate * nj + j)),
                  pl.BlockSpec((None, ms, tn), lambda j, i: (l, 0, j_gate * nj + j))],
        out_specs=[pl.BlockSpec((tm, tn), lambda j, i: (i, j)),
                   pl.BlockSpec((ms, tn), lambda j, i: (0, j))],
        out_shape=[jax.ShapeDtypeStruct((m, n), F32), jax.ShapeDtypeStruct((ms, n), F32)],
        scratch_shapes=[pltpu.VMEM((k, tn), BF16)],
        compiler_params=_cparams(("arbitrary", "arbitrary")), name=name,
    )(a, a_s, w, x, x_s, modp, mods)


def _mm_swiglu_kernel(a_ref, as_ref, wg_ref, wu_ref, o_ref, os_ref, wgb_ref, wub_ref):
    @pl.when(pl.program_id(1) == 0)
    def _():
        wgb_ref[...] = _bf(wg_ref[...])
        wub_ref[...] = _bf(wu_ref[...])
        a_s = as_ref[...]
        os_ref[...] = (_silu(_dot(a_s, wgb_ref[...])) * _dot(a_s, wub_ref[...])).astype(os_ref.dtype)

    a = a_ref[...]
    o_ref[...] = (_silu(_dot(a, wgb_ref[...])) * _dot(a, wub_ref[...])).astype(o_ref.dtype)


def _mm_swiglu(a, a_s, wg, wu, l, tm, tn):
    m, k = a.shape
    ms = a_s.shape[0]
    n = wg.shape[2]
    return pl.pallas_call(
        _mm_swiglu_kernel, grid=(n // tn, m // tm),
        in_specs=[pl.BlockSpec((tm, k), lambda j, i: (i, 0)),
                  pl.BlockSpec((ms, k), lambda j, i: (0, 0)),
                  pl.BlockSpec((None, k, tn), lambda j, i: (l, 0, j)),
                  pl.BlockSpec((None, k, tn), lambda j, i: (l, 0, j))],
        out_specs=[pl.BlockSpec((tm, tn), lambda j, i: (i, j)),
                   pl.BlockSpec((ms, tn), lambda j, i: (0, j))],
        out_shape=[jax.ShapeDtypeStruct((m, n), BF16), jax.ShapeDtypeStruct((ms, n), BF16)],
        scratch_shapes=[pltpu.VMEM((k, tn), BF16), pltpu.VMEM((k, tn), BF16)],
        compiler_params=_cparams(("arbitrary", "arbitrary")), name="ffn_gate_up",
    )(a, a_s, wg, wu)


def _merge_kernel(*refs):
    o_p = refs[0:4]
    o_s = refs[4:8]
    g_p = refs[8:12]
    g_s = refs[12:16]
    w_ref, out_ref, outs_ref, wb_ref = refs[16:20]

    @pl.when(pl.program_id(1) == 0)
    def _():
        wb_ref[...] = _bf(w_ref[...])
        acc = None
        for b in range(4):
            t = _sigmoid(g_s[b][...].astype(F32)) * _dot(_bf(o_s[b][...]), wb_ref[b])
            acc = t if acc is None else acc + t
        outs_ref[...] = acc.astype(outs_ref.dtype)

    acc = None
    for b in range(4):
        t = _sigmoid(g_p[b][...].astype(F32)) * _dot(_bf(o_p[b][...]), wb_ref[b])
        acc = t if acc is None else acc + t
    out_ref[...] = acc.astype(out_ref.dtype)


def _merge(o_p, o_s, z, z_s, w_branch, l, tm, tn):
    m = z.shape[0]
    ms = z_s.shape[0]
    bw = w_branch.shape[2]
    d = w_branch.shape[3]
    in_specs = ([pl.BlockSpec((tm, bw), lambda j, i: (i, 0)) for _ in range(4)]
                + [pl.BlockSpec((ms, bw), lambda j, i: (0, 0)) for _ in range(4)]
                + [pl.BlockSpec((tm, tn), lambda j, i, b=b: (i, b * (d // tn) + j)) for b in range(4)]
                + [pl.BlockSpec((ms, tn), lambda j, i, b=b: (0, b * (d // tn) + j)) for b in range(4)]
                + [pl.BlockSpec((None, 4, bw, tn), lambda j, i: (l, 0, 0, j))])
    return pl.pallas_call(
        _merge_kernel, grid=(d // tn, m // tm),
        in_specs=in_specs,
        out_specs=[pl.BlockSpec((tm, tn), lambda j, i: (i, j)), pl.BlockSpec((ms, tn), lambda j, i: (0, j))],
        out_shape=[jax.ShapeDtypeStruct((m, d), BF16), jax.ShapeDtypeStruct((ms, d), BF16)],
        scratch_shapes=[pltpu.VMEM((4, bw, tn), BF16)],
        compiler_params=_cparams(("arbitrary", "arbitrary")), name="merge",
    )(*o_p, *o_s, z, z, z, z, z_s, z_s, z_s, z_s, w_branch)


def _live_rows(c_rows, n_live):
    return lax.broadcasted_iota(jnp.int32, (c_rows, 1), 0) < n_live


def _gla_kernel(z_ref, small_ref, wa_ref, ba_ref, gn_ref, s0_ref, o_ref, sout_ref, st_scr, *, NB, C, L, H, DK, DV):
    c = pl.program_id(0)

    @pl.when(c == 0)
    def _():
        st_scr[...] = s0_ref[...]

    causal = _causal(C)
    wa = _bf(wa_ref[...])
    ba = ba_ref[...]
    gn = gn_ref[...]
    mid = C // 2

    def one_sequence(bi):
        z = z_ref[bi]
        q = z[:, 0:H * DK] * (DK ** -0.5)
        k = z[:, H * DK:2 * H * DK]
        v = z[:, 2 * H * DK:2 * H * DK + H * DV]
        r = z[:, 2 * H * DK + H * DV:]
        ga = _dot(_bf(small_ref[bi]), wa) + ba
        glog = _log_sigmoid(ga) * (1.0 / GLA_TAU)
        if L < C:
            live = _live_rows(C, L)
            glog = jnp.where(live, glog, 0.0)
            k = jnp.where(live, k, 0.0)
        b = _cumsum_rows(glog)
        bl = b[C - 1:C, :]
        bm = b[mid - 1:mid, :]
        qi = _bf(q * jnp.exp(b - bm))
        ki = _bf(k * jnp.exp(bm - b))
        qs = q * jnp.exp(b)
        kd = _bf(k * jnp.exp(bl - b))
        ebl_col = jnp.exp(b.T[:, C - 1:C])
        outs = []
        for h in range(H):
            ks = slice(h * DK, (h + 1) * DK)
            vs = slice(h * DV, (h + 1) * DV)
            vh = v[:, vs]
            a = jnp.where(causal, _dot_nt(qi[:, ks], ki[:, ks]), 0.0)
            st = st_scr[bi, h]
            o = _dot(_bf(jnp.concatenate([a, qs[:, ks]], axis=1)), _bf(jnp.concatenate([vh, st], axis=0)))
            st_scr[bi, h] = ebl_col[ks, :] * st + _dot_tn(kd[:, ks], _bf(vh))
            outs.append(_rms(o, gn[:, vs]))
        o_ref[bi] = (jnp.concatenate(outs, axis=1) * _silu(r)).astype(o_ref.dtype)

    for bi in range(NB):
        one_sequence(bi)

    @pl.when(c == pl.num_programs(0) - 1)
    def _():
        sout_ref[...] = st_scr[...]


def _gla_call(z, zsm, wa_pad, ba, gn, s0, l, nb, seq, C, L, out_dtype):
    H, DK, DV = s0.shape[1:]
    nch = seq // C
    width = 2 * H * DK + 2 * H * DV
    z3 = z.reshape(nb, seq, z.shape[1])
    zsm3 = zsm.reshape(nb, seq, LANE)
    kern = functools.partial(_gla_kernel, NB=nb, C=C, L=L, H=H, DK=DK, DV=DV)
    o, s = pl.pallas_call(
        kern, grid=(nch,),
        in_specs=[pl.BlockSpec((nb, C, width), lambda c: (0, c, 0)),
                  pl.BlockSpec((nb, C, LANE), lambda c: (0, c, 0)),
                  pl.BlockSpec((None, LANE, H * DK), lambda c: (l, 0, 0)),
                  pl.BlockSpec((None, 1, H * DK), lambda c: (l, 0, 0)),
                  pl.BlockSpec((None, 1, H * DV), lambda c: (l, 0, 0)),
                  pl.BlockSpec((nb, H, DK, DV), lambda c: (0, 0, 0, 0))],
        out_specs=[pl.BlockSpec((nb, C, H * DV), lambda c: (0, c, 0)),
                   pl.BlockSpec((nb, H, DK, DV), lambda c: (0, 0, 0, 0))],
        out_shape=[jax.ShapeDtypeStruct((nb, seq, H * DV), out_dtype),
                   jax.ShapeDtypeStruct((nb, H, DK, DV), F32)],
        scratch_shapes=[pltpu.VMEM((nb, H, DK, DV), F32)],
        compiler_params=_cparams(("arbitrary",)), name="gla",
    )(z3, zsm3, wa_pad, ba, gn, s0)
    return o.reshape(nb * seq, H * DV), s


def _ssd_kernel(xbc_ref, zg_ref, small_ref, cw_ref, cb_ref, dtb_ref, alog_ref, dvec_ref, gn_ref, s0_ref, conv0_ref,
                o_ref, sout_ref, convout_ref, s_scr, ext_scr, *, NB, C, L, H, G, N, P, dt_lane):
    c = pl.program_id(0)
    kh = SSD_CONV - 1
    base = SUBLANE - kh

    @pl.when(c == 0)
    def _():
        s_scr[...] = s0_ref[...]
        ext_scr[:, base:SUBLANE, :] = conv0_ref[...]

    cw = cw_ref[...]
    cb = cb_ref[...]
    dtb = dtb_ref[...]
    neg_a = -jnp.exp(alog_ref[...])
    dvec = dvec_ref[...]
    gn = gn_ref[...]
    causal = _causal(C)
    bw = H * P
    hpg = H // G

    def one_sequence(bi):
        ext_scr[bi, SUBLANE:SUBLANE + C, :] = xbc_ref[bi]
        conv = cb
        for j in range(SSD_CONV):
            conv = conv + ext_scr[bi, pl.ds(base + j, C), :] * cw[j:j + 1, :]
        hist = ext_scr[bi, pl.ds(base + L, kh), :]
        ext_scr[bi, base:SUBLANE, :] = hist
        xc = _silu(conv)
        sx = xc[:, 0:bw]
        bm = xc[:, bw:bw + G * N]
        cm = xc[:, bw + G * N:]
        dt = _softplus(small_ref[bi] + dtb)
        logg = dt * neg_a
        if L < C:
            live = _live_rows(C, L)
            logg = jnp.where(live, logg, 0.0)
            bm = jnp.where(live, bm, 0.0)
        b = _cumsum_rows(logg)
        b_t = b.T
        ys = []
        for g in range(G):
            cg = cm[:, g * N:(g + 1) * N]
            bg = bm[:, g * N:(g + 1) * N]
            gmat = _dot_nt(_bf(cg), _bf(bg))
            for hh in range(hpg):
                h = g * hpg + hh
                ln = dt_lane + h
                bc = b[:, ln:ln + 1]
                br = b_t[ln:ln + 1, :]
                dec = jnp.exp(jnp.where(causal, bc - br, -jnp.inf))
                xdt = sx[:, h * P:(h + 1) * P] * dt[:, ln:ln + 1]
                st = s_scr[bi, h]
                y = _dot(_bf(jnp.concatenate([gmat * dec, cg * jnp.exp(bc)], axis=1)),
                         _bf(jnp.concatenate([xdt, st], axis=0)))
                bl = bc[C - 1:C, :]
                s_scr[bi, h] = jnp.exp(bl) * st + _dot_tn(_bf(bg * jnp.exp(bl - bc)), _bf(xdt))
                ys.append(y)
        y_all = jnp.concatenate(ys, axis=1) + dvec * sx
        o_ref[bi] = _rms(y_all * _silu(zg_ref[bi]), gn).astype(o_ref.dtype)

    for bi in range(NB):
        one_sequence(bi)

    @pl.when(c == pl.num_programs(0) - 1)
    def _():
        sout_ref[...] = s_scr[...]
        convout_ref[...] = ext_scr[:, base:SUBLANE, :]


def _ssd_call(z, zsm, xbc_idx, zg_idx, cw, cb, dtb, alog, dvec, gn, s0, conv0, l, nb, seq, C, L, dt_lane,
              out_dtype):
    H, N, P = s0.shape[1:]
    G = SSD_GROUPS
    ch = conv0.shape[2]
    bw = H * P
    nch = seq // C
    z3 = z.reshape(nb, seq, z.shape[1])
    zsm3 = zsm.reshape(nb, seq, LANE)
    kern = functools.partial(_ssd_kernel, NB=nb, C=C, L=L, H=H, G=G, N=N, P=P, dt_lane=dt_lane)
    vec = lambda w: pl.BlockSpec((None, 1, w), lambda c: (l, 0, 0))
    o, s, cv = pl.pallas_call(
        kern, grid=(nch,),
        in_specs=[pl.BlockSpec((nb, C, ch), lambda c: (0, c, xbc_idx)),
                  pl.BlockSpec((nb, C, bw), lambda c: (0, c, zg_idx)),
                  pl.BlockSpec((nb, C, LANE), lambda c: (0, c, 0)),
                  pl.BlockSpec((None, SSD_CONV, ch), lambda c: (l, 0, 0)),
                  vec(ch), vec(LANE), vec(LANE), vec(bw), vec(bw),
                  pl.BlockSpec((nb, H, N, P), lambda c: (0, 0, 0, 0)),
                  pl.BlockSpec((nb, SSD_CONV - 1, ch), lambda c: (0, 0, 0))],
        out_specs=[pl.BlockSpec((nb, C, bw), lambda c: (0, c, 0)),
                   pl.BlockSpec((nb, H, N, P), lambda c: (0, 0, 0, 0)),
                   pl.BlockSpec((nb, SSD_CONV - 1, ch), lambda c: (0, 0, 0))],
        out_shape=[jax.ShapeDtypeStruct((nb, seq, bw), out_dtype),
                   jax.ShapeDtypeStruct((nb, H, N, P), F32),
                   jax.ShapeDtypeStruct((nb, SSD_CONV - 1, ch), F32)],
        scratch_shapes=[pltpu.VMEM((nb, H, N, P), F32), pltpu.VMEM((nb, SUBLANE + C, ch), F32)],
        compiler_params=_cparams(("arbitrary",)), name="ssd",
    )(z3, z3, zsm3, cw, cb, dtb, alog, dvec, gn, s0, conv0)
    return o.reshape(nb * seq, bw), s, cv


def _mlstm_kernel(qk_ref, vo_ref, small_ref, bif_ref, gn_ref, c0_ref, nm0_ref, o_ref, cout_ref, nmout_ref,
                  c_scr, nm_scr, *, NB, C, L, H, DH, i_lane, f_lane):
    c = pl.program_id(0)

    @pl.when(c == 0)
    def _():
        c_scr[...] = c0_ref[...]
        nm_scr[...] = nm0_ref[...]

    causal_t = lax.broadcasted_iota(jnp.int32, (C, C), 0) <= lax.broadcasted_iota(jnp.int32, (C, C), 1)
    lane = lax.broadcasted_iota(jnp.int32, (1, LANE), 1)
    bif = bif_ref[...]
    gn = gn_ref[...]

    def one_sequence(bi):
        qk = qk_ref[bi]
        vo = vo_ref[bi]
        q = qk[:, 0:H * DH]
        k = qk[:, H * DH:] * (DH ** -0.5)
        v = vo[:, 0:H * DH]
        og = vo[:, H * DH:]
        pre = small_ref[bi] + bif
        logf = _log_sigmoid(pre)
        ipre = pre
        if L < C:
            live = _live_rows(C, L)
            logf = jnp.where(live, logf, 0.0)
            ipre = jnp.where(live, pre, NEG)
        b = _cumsum_rows(logf)
        b_t = b.T
        nm = nm_scr[bi]
        nm_bf = _bf(nm)
        m_row = nm[H:H + 1, :]
        outs = []
        for h in range(H):
            sl = slice(h * DH, (h + 1) * DH)
            br = b_t[f_lane + h:f_lane + h + 1, :]
            uc = ipre[:, i_lane + h:i_lane + h + 1] - b[:, f_lane + h:f_lane + h + 1]
            m_prev = nm[H:H + 1, h:h + 1]
            logd = jnp.where(causal_t, br + uc, -jnp.inf)
            gg = br + m_prev
            m = jnp.maximum(gg, jnp.max(logd, axis=0, keepdims=True))
            dm = jnp.exp(logd - m)
            wi = jnp.exp(gg - m)
            qf = q[:, sl]
            qh = _bf(qf)
            kh = k[:, sl]
            vh = _bf(v[:, sl])
            sc = _dot_nt(jnp.concatenate([_bf(kh), nm_bf], axis=0), qh)
            qk_t = sc[0:C, :] * dm
            cst = c_scr[bi, h]
            nq = sc[C + h:C + h + 1, :]
            den = jnp.sum(qk_t, axis=0, keepdims=True) + wi * nq
            rden = 1.0 / jnp.maximum(jnp.abs(den), jnp.exp(-m))
            rows = jnp.concatenate([wi, rden, jnp.zeros((SUBLANE - 2, C), F32)], axis=0)
            if C < LANE:
                rows = jnp.concatenate([rows, jnp.zeros((SUBLANE, LANE - C), F32)], axis=1)
            cols = rows.T[0:C, :]
            hid = (_dot_tn(_bf(qk_t), vh) + cols[:, 0:1] * _dot(qh, _bf(cst))) * cols[:, 1:2]
            m_new = m[:, C - 1:C]
            bl = br[:, C - 1:C]
            ws = jnp.exp(uc + (bl - m_new))
            wc = jnp.exp(bl + m_prev - m_new)
            nrow = nm[h:h + 1, :]
            wk = ws * kh
            c_scr[bi, h] = wc * cst + _dot_tn(_bf(wk), vh)
            nm_scr[bi, h:h + 1, :] = wc * nrow + jnp.sum(wk, axis=0, keepdims=True)
            m_row = jnp.where(lane == h, m_new, m_row)
            outs.append(_rms(hid, gn[:, sl]))
        nm_scr[bi, H:H + 1, :] = m_row
        o_ref[bi] = (jnp.concatenate(outs, axis=1) * _sigmoid(og)).astype(o_ref.dtype)

    for bi in range(NB):
        one_sequence(bi)

    @pl.when(c == pl.num_programs(0) - 1)
    def _():
        cout_ref[...] = c_scr[...]
        nmout_ref[...] = nm_scr[...]


def _mlstm_call(z, zsm, qk_idx, vo_idx, bif, gn, c0, nm0, l, nb, seq, C, L, i_lane, f_lane, out_dtype):
    H, DH = c0.shape[1:3]
    nch = seq // C
    z3 = z.reshape(nb, seq, z.shape[1])
    zsm3 = zsm.reshape(nb, seq, LANE)
    kern = functools.partial(_mlstm_kernel, NB=nb, C=C, L=L, H=H, DH=DH, i_lane=i_lane, f_lane=f_lane)
    o, cc, nm = pl.pallas_call(
        kern, grid=(nch,),
        in_specs=[pl.BlockSpec((nb, C, 2 * H * DH), lambda c: (0, c, qk_idx)),
                  pl.BlockSpec((nb, C, 2 * H * DH), lambda c: (0, c, vo_idx)),
                  pl.BlockSpec((nb, C, LANE), lambda c: (0, c, 0)),
                  pl.BlockSpec((None, 1, LANE), lambda c: (l, 0, 0)),
                  pl.BlockSpec((None, 1, H * DH), lambda c: (l, 0, 0)),
                  pl.BlockSpec((nb, H, DH, DH), lambda c: (0, 0, 0, 0)),
                  pl.BlockSpec((nb, SUBLANE, LANE), lambda c: (0, 0, 0))],
        out_specs=[pl.BlockSpec((nb, C, H * DH), lambda c: (0, c, 0)),
                   pl.BlockSpec((nb, H, DH, DH), lambda c: (0, 0, 0, 0)),
                   pl.BlockSpec((nb, SUBLANE, LANE), lambda c: (0, 0, 0))],
        out_shape=[jax.ShapeDtypeStruct((nb, seq, H * DH), out_dtype),
                   jax.ShapeDtypeStruct((nb, H, DH, DH), F32),
                   jax.ShapeDtypeStruct((nb, SUBLANE, LANE), F32)],
        scratch_shapes=[pltpu.VMEM((nb, H, DH, DH), F32), pltpu.VMEM((nb, SUBLANE, LANE), F32)],
        compiler_params=_cparams(("arbitrary",)), name="mlstm",
    )(z3, z3, zsm3, bif, gn, c0, nm0)
    return o.reshape(nb * seq, H * DH), cc, nm


def _rank_select(score, n_blocks, keep):
    lane = lax.broadcasted_iota(jnp.int32, score.shape, 1)
    cnt = jnp.zeros(score.shape, jnp.int32)
    for j in range(n_blocks):
        cj = score[:, j:j + 1]
        ahead = jnp.where(cj > score, 1, jnp.where(cj == score, jnp.where(lane > j, 1, 0), 0))
        cnt = cnt + ahead
    return cnt < keep


def _nsa_prompt_kernel(q_ref, kv_ref, small_ref, o_ref, kvb_ref, cmp_ref, *, T, TQ, DH, g_lane):
    qi = pl.program_id(1)
    G, R = NSA_KV, NSA_R
    gw = G * DH
    ncb = T // CMP_BLOCK
    nsb = T // SEL_BLOCK
    ratio = SEL_BLOCK // CMP_BLOCK
    slopes = _alibi_slopes()

    @pl.when(qi == 0)
    def _():
        kvb_ref[...] = _bf(kv_ref[...])
        hi, lo = _split2(kv_ref[:, 0:2 * gw])
        jj = lax.broadcasted_iota(jnp.int32, (ncb, T), 0) * CMP_BLOCK
        tt = lax.broadcasted_iota(jnp.int32, (ncb, T), 1)
        pm = jnp.where((tt >= jj) & (tt < jj + CMP_BLOCK), 1.0 / CMP_BLOCK, 0.0).astype(BF16)
        cmp_ref[...] = _bf(_dot(pm, hi) + _dot(pm, lo))

    q = q_ref[...]
    gate = _sigmoid(small_ref[...])
    posl = qi * TQ + lax.broadcasted_iota(jnp.int32, (1, TQ), 1)
    cbs = lax.broadcasted_iota(jnp.int32, (ncb, 1), 0)
    cvalid = (cbs * CMP_BLOCK + (CMP_BLOCK - 1)) <= posl
    cdist = posl.astype(F32) - (cbs.astype(F32) * CMP_BLOCK + (CMP_BLOCK - 1) / 2.0)
    sbs = lax.broadcasted_iota(jnp.int32, (nsb, 1), 0)
    svalid = sbs * SEL_BLOCK <= posl
    forced = (sbs == 0) | ((sbs * SEL_BLOCK <= posl) & (posl < sbs * SEL_BLOCK + SEL_BLOCK))
    pi = lax.broadcasted_iota(jnp.int32, (nsb, ncb), 0) * ratio
    pj = lax.broadcasted_iota(jnp.int32, (nsb, ncb), 1)
    pool = jnp.where((pj >= pi) & (pj < pi + ratio), 1.0, 0.0).astype(BF16)
    brow = lax.broadcasted_iota(jnp.int32, (nsb, TQ), 0)
    row2 = lax.broadcasted_iota(jnp.int32, (R * TQ, 1), 0)
    rr = lax.broadcasted_iota(jnp.int32, (TQ, TQ), 0)
    cc = lax.broadcasted_iota(jnp.int32, (TQ, TQ), 1)
    lower = rr >= cc
    lower2 = jnp.concatenate([lower] * R, axis=0)
    col = lax.broadcasted_iota(jnp.int32, (1, TQ), 1)
    nfar = WINDOW // TQ

    def tile(carry, kt, q2, slope2, kcol, vcol, mask):
        m, l, acc = carry
        k0 = pl.multiple_of(kt * TQ, TQ)
        ks = kvb_ref[pl.ds(k0, TQ), kcol:kcol + DH]
        vs = kvb_ref[pl.ds(k0, TQ), vcol:vcol + DH]
        s = _dot_nt(q2, ks) + slope2 * (k0 + col).astype(F32)
        if mask is not None:
            s = jnp.where(mask, s, NEG)
        mn = jnp.maximum(m, jnp.max(s, axis=1, keepdims=True))
        p = jnp.exp(s - mn)
        alpha = jnp.exp(m - mn)
        return (mn, alpha * l + jnp.sum(p, axis=1, keepdims=True), alpha * acc + _dot(_bf(p), vs))

    def sel_mask(sel, kt):
        kpos = kt * TQ + col
        expand = _bf(jnp.where((kpos >= sbs * SEL_BLOCK) & (kpos < sbs * SEL_BLOCK + SEL_BLOCK), 1.0, 0.0))
        return _dot_tn(sel, expand) > 0.5

    init = (jnp.full((R * TQ, 1), NEG, F32), jnp.zeros((R * TQ, 1), F32), jnp.zeros((R * TQ, DH), F32))
    o_cmp, sels, q2s, slope2s, sel_carry, o_win = [], [], [], [], [], []
    for g in range(G):
        kc = cmp_ref[:, g * DH:(g + 1) * DH]
        vc = cmp_ref[:, gw + g * DH:gw + (g + 1) * DH]
        qs = []
        imp = jnp.zeros((nsb, TQ), F32)
        for r in range(R):
            hd = g * R + r
            qh = _bf(q[:, hd * DH:(hd + 1) * DH] * (DH ** -0.5))
            qs.append(qh)
            s = jnp.where(cvalid, _dot_nt(kc, qh) - slopes[hd] * cdist, NEG)
            e = jnp.exp(s - jnp.max(s, axis=0, keepdims=True))
            p = jnp.where(cvalid, e / jnp.sum(e, axis=0, keepdims=True), 0.0)
            o_cmp.append(_dot_tn(_bf(p), vc))
            ph, plo = _split2(p)
            imp = imp + _dot(pool, ph) + _dot(pool, plo)
        score = jnp.where(svalid, jnp.where(forced, FORCE_SCORE, imp), -1.0)
        ahead = jnp.zeros((nsb, TQ), jnp.int32)
        for j in range(nsb):
            sj = score[j:j + 1, :]
            ahead = ahead + jnp.where(sj > score, 1, jnp.where(sj == score, jnp.where(brow > j, 1, 0), 0))
        sel = _bf(jnp.where((ahead < SEL_TOPK) & (score >= 0.0), 1.0, 0.0))
        q2 = jnp.concatenate(qs, axis=0)
        slope2 = slopes[g * R]
        for r in range(1, R):
            slope2 = jnp.where(row2 >= r * TQ, slopes[g * R + r], slope2)
        sels.append(sel)
        q2s.append(q2)
        slope2s.append(slope2)
        sel_carry.append(tile(init, qi, q2, slope2, 2 * gw + g * DH, 3 * gw + g * DH,
                              jnp.concatenate([sel_mask(sel, qi) & lower] * R, axis=0)))
        kcol, vcol = 4 * gw + g * DH, 5 * gw + g * DH
        carry = tile(init, qi, q2, slope2, kcol, vcol, lower2)
        for back in range(1, nfar + 1):
            kt = jnp.maximum(qi - back, 0)
            live = qi >= back
            mask = jnp.logical_and(jnp.logical_not(lower2), live) if back == nfar else live
            carry = tile(carry, kt, q2, slope2, kcol, vcol, mask)
        o_win.append(carry[2] / carry[1])

    def sel_tiles(kt, carries):
        return tuple(tile(carries[g], kt, q2s[g], slope2s[g], 2 * gw + g * DH, 3 * gw + g * DH,
                          jnp.concatenate([sel_mask(sels[g], kt)] * R, axis=0)) for g in range(G))

    sel_carry = lax.fori_loop(0, qi, sel_tiles, tuple(sel_carry))
    outs = [None] * (G * R)
    for g in range(G):
        o_slc = sel_carry[g][2] / sel_carry[g][1]
        for r in range(R):
            hd = g * R + r
            gc = gate[:, g_lane + hd:g_lane + hd + 1]
            gs = gate[:, g_lane + NSA_HEADS + hd:g_lane + NSA_HEADS + hd + 1]
            gwn = gate[:, g_lane + 2 * NSA_HEADS + hd:g_lane + 2 * NSA_HEADS + hd + 1]
            outs[hd] = gc * o_cmp[hd] + gs * o_slc[r * TQ:(r + 1) * TQ] + gwn * o_win[g][r * TQ:(r + 1) * TQ]
    o_ref[...] = jnp.concatenate(outs, axis=1).astype(o_ref.dtype)


def _nsa_prompt_call(z, zsm, q_idx, kv_idx, nb, seq, g_lane, tq=256):
    dh = LANE
    kvw = 6 * NSA_KV * dh
    qw = NSA_HEADS * dh
    nq = seq // tq
    assert WINDOW % tq == 0 and seq % tq == 0
    kern = functools.partial(_nsa_prompt_kernel, T=seq, TQ=tq, DH=dh, g_lane=g_lane)
    return pl.pallas_call(
        kern, grid=(nb, nq),
        in_specs=[pl.BlockSpec((tq, qw), lambda b, i: (b * nq + i, q_idx)),
                  pl.BlockSpec((seq, kvw), lambda b, i: (b, kv_idx)),
                  pl.BlockSpec((tq, LANE), lambda b, i: (b * nq + i, 0))],
        out_specs=pl.BlockSpec((tq, qw), lambda b, i: (b * nq + i, 0)),
        out_shape=jax.ShapeDtypeStruct((nb * seq, qw), BF16),
        scratch_shapes=[pltpu.VMEM((seq, kvw), BF16), pltpu.VMEM((seq // CMP_BLOCK, 2 * NSA_KV * dh), BF16)],
        compiler_params=_cparams(("arbitrary", "arbitrary")), name="nsa_prompt",
    )(z, z, zsm)


def _nsa_means_kernel(pt_ref, *refs, PP, DH):
    ins = refs[:PP]
    o_ref = refs[PP]
    nper = PAGE_SIZE // CMP_BLOCK
    for i in range(PP):
        x = ins[i][...].reshape(nper, CMP_BLOCK, 2, NSA_KV, DH)
        o_ref[i * nper:(i + 1) * nper] = jnp.sum(x, axis=1) * (1.0 / CMP_BLOCK)


def _nsa_means_call(cache, page_table, l, pp=16):
    nb, n_pages = page_table.shape
    dh = cache.shape[5]
    nper = PAGE_SIZE // CMP_BLOCK
    assert cache.shape[2] == PAGE_SIZE and cache.shape[4] == NSA_KV and n_pages % pp == 0
    in_specs = [pl.BlockSpec((None, None, PAGE_SIZE, 2, NSA_KV, dh),
                             lambda b, s, pt, i=i: (l, pt[b, s * pp + i], 0, 0, 0, 0)) for i in range(pp)]
    gs = pltpu.PrefetchScalarGridSpec(
        num_scalar_prefetch=1, grid=(nb, n_pages // pp), in_specs=in_specs,
        out_specs=pl.BlockSpec((None, nper * pp, 2, NSA_KV, dh), lambda b, s, pt: (b, s, 0, 0, 0)))
    means = pl.pallas_call(
        functools.partial(_nsa_means_kernel, PP=pp, DH=dh), grid_spec=gs,
        out_shape=jax.ShapeDtypeStruct((nb, n_pages * nper, 2, NSA_KV, dh), F32),
        compiler_params=_cparams(("arbitrary", "arbitrary")), name="nsa_cmp_means",
    )(page_table, *([cache] * pp))
    return means.reshape(nb, n_pages * nper, 2 * NSA_KV * dh)


def _nsa_score_kernel(q_ref, kvc_ref, ocmp_ref, ids_ref, *, DH, qpos):
    G, R = NSA_KV, NSA_R
    gw = G * DH
    ncb = kvc_ref.shape[0]
    ratio = SEL_BLOCK // CMP_BLOCK
    nsb = ncb // ratio
    slopes = _alibi_slopes()
    q = q_ref[...]
    kvc = _bf(kvc_ref[...])
    rows = q.shape[0]
    cb = lax.broadcasted_iota(jnp.int32, (1, ncb), 1)
    cvalid = (cb * CMP_BLOCK + (CMP_BLOCK - 1)) <= qpos
    cdist = float(qpos) - (cb.astype(F32) * CMP_BLOCK + (CMP_BLOCK - 1) / 2.0)
    pi = lax.broadcasted_iota(jnp.int32, (ncb, nsb), 0)
    pj = lax.broadcasted_iota(jnp.int32, (ncb, nsb), 1) * ratio
    pool = jnp.where((pi >= pj) & (pi < pj + ratio), 1.0, 0.0).astype(BF16)
    lane = lax.broadcasted_iota(jnp.int32, (rows, nsb), 1).astype(F32)
    lane_out = lax.broadcasted_iota(jnp.int32, (rows, LANE), 1)
    o_cmp = []
    id_rows = []
    for g in range(G):
        kc = kvc[:, g * DH:(g + 1) * DH]
        vc = kvc[:, gw + g * DH:gw + (g + 1) * DH]
        imp = jnp.zeros((rows, nsb), F32)
        for r in range(R):
            hd = g * R + r
            qh = _bf(q[:, hd * DH:(hd + 1) * DH] * (DH ** -0.5))
            s = jnp.where(cvalid, _dot_nt(qh, kc) - slopes[hd] * cdist, NEG)
            p = jnp.where(cvalid, _softmax_rows(s), 0.0)
            o_cmp.append(_dot(_bf(p), vc))
            ph, plo = _split2(p)
            imp = imp + _dot(ph, pool) + _dot(plo, pool)
        val = jnp.where(lane == 0.0, -1.0, imp)
        ids = jnp.zeros((rows, LANE), F32)
        for it in range(SEL_TOPK - 2):
            mx = jnp.max(val, axis=1, keepdims=True)
            ix = jnp.min(jnp.where(val == mx, lane, float(nsb)), axis=1, keepdims=True)
            ids = jnp.where(lane_out == it + 1, ix, ids)
            val = jnp.where(lane == ix, -2.0, val)
        id_rows.append(ids[0:1, :])
    ocmp_ref[...] = jnp.concatenate(o_cmp, axis=1)
    pad = jnp.zeros((ids_ref.shape[0] - G, LANE), F32)
    ids_ref[...] = jnp.concatenate(id_rows + [pad], axis=0).astype(jnp.int32)


def _nsa_score_call(z_s, q_idx, kvc, qpos):
    nb, ncb, w = kvc.shape
    dh = w // (2 * NSA_KV)
    qw = NSA_HEADS * dh
    return pl.pallas_call(
        functools.partial(_nsa_score_kernel, DH=dh, qpos=qpos), grid=(nb,),
        in_specs=[pl.BlockSpec((SROWS, qw), lambda b: (b, q_idx)),
                  pl.BlockSpec((None, ncb, w), lambda b: (b, 0, 0))],
        out_specs=[pl.BlockSpec((SROWS, qw), lambda b: (b, 0)),
                   pl.BlockSpec((None, SUBLANE, LANE), lambda b: (b, 0, 0))],
        out_shape=[jax.ShapeDtypeStruct((nb * SROWS, qw), F32),
                   jax.ShapeDtypeStruct((nb, SUBLANE, LANE), jnp.int32)],
        compiler_params=_cparams(("arbitrary",)), name="nsa_cmp_score",
    )(z_s, kvc)


def _nsa_gather_kernel(sel_ref, q_ref, new_ref, small_ref, ocmp_ref, win_ref, *rest,
                       DH, NSEL, qpos, g_lane):
    kv = rest[:NSA_KV * NSEL]
    o_ref = rest[NSA_KV * NSEL]
    b = pl.program_id(0)
    G, R = NSA_KV, NSA_R
    gw = G * DH
    slopes = _alibi_slopes()
    q = q_ref[...]
    new = new_ref[...]
    gate = _sigmoid(small_ref[...])
    ocmp = ocmp_ref[...]
    wlen = win_ref.shape[0]
    jw = lax.broadcasted_iota(jnp.int32, (1, wlen), 1)
    distw = wlen - jw
    validw = (distw < WINDOW) & (qpos - distw >= 0)
    off = lax.broadcasted_iota(jnp.int32, (1, SEL_BLOCK), 1)
    outs = []
    for g in range(G):
        ksn = _bf(new[:, 2 * gw + g * DH:2 * gw + (g + 1) * DH]).astype(F32)
        vsn = _bf(new[:, 3 * gw + g * DH:3 * gw + (g + 1) * DH]).astype(F32)
        kwn = _bf(new[:, 4 * gw + g * DH:4 * gw + (g + 1) * DH]).astype(F32)
        vwn = _bf(new[:, 5 * gw + g * DH:5 * gw + (g + 1) * DH]).astype(F32)
        kw = _bf(win_ref[:, 0, g, :])
        vw = _bf(win_ref[:, 1, g, :])
        kblk = [_bf(kv[g * NSEL + i][:, 0, g, :]) for i in range(NSEL)]
        vblk = [_bf(kv[g * NSEL + i][:, 1, g, :]) for i in range(NSEL)]
        nslot = NSA_KV * SEL_TOPK
        dists = [qpos - (sel_ref[b, 2 * nslot + g * SEL_TOPK + i] * SEL_BLOCK + off) for i in range(NSEL)]
        for r in range(R):
            hd = g * R + r
            slope = slopes[hd]
            qh = _bf(q[:, hd * DH:(hd + 1) * DH] * (DH ** -0.5))
            qf = qh.astype(F32)
            s_list = [jnp.where(dists[i] >= 0, _dot_nt(qh, kblk[i]) - slope * dists[i].astype(F32), NEG)
                      for i in range(NSEL)]
            s_new = jnp.sum(qf * ksn, axis=1, keepdims=True)
            mx = s_new
            for s in s_list:
                mx = jnp.maximum(mx, jnp.max(s, axis=1, keepdims=True))
            p_new = jnp.exp(s_new - mx)
            den = p_new
            acc = p_new * vsn
            for i, s in enumerate(s_list):
                p = jnp.exp(s - mx)
                den = den + jnp.sum(p, axis=1, keepdims=True)
                acc = acc + _dot(_bf(p), vblk[i])
            o_slc = acc / den
            s = jnp.where(validw, _dot_nt(qh, kw) - slope * distw.astype(F32), NEG)
            s_new = jnp.sum(qf * kwn, axis=1, keepdims=True)
            mx = jnp.maximum(s_new, jnp.max(s, axis=1, keepdims=True))
            p = jnp.exp(s - mx)
            p_new = jnp.exp(s_new - mx)
            o_win = (_dot(_bf(p), vw) + p_new * vwn) / (jnp.sum(p, axis=1, keepdims=True) + p_new)
            gc, gs, gwn = [gate[:, g_lane + c * NSA_HEADS + hd:g_lane + c * NSA_HEADS + hd + 1] for c in range(3)]
            outs.append(gc * ocmp[:, hd * DH:(hd + 1) * DH] + gs * o_slc + gwn * o_win)
    o_ref[...] = jnp.concatenate(outs, axis=1)


def _nsa_gather_call(z_s, zsm_s, q_idx, kv_idx, ocmp, win, cache, page_table, ids, l, qpos, g_lane):
    nb = page_table.shape[0]
    dh = cache.shape[5]
    G = NSA_KV
    qw = NSA_HEADS * dh
    nsel = SEL_TOPK - 1
    bpp = PAGE_SIZE // SEL_BLOCK
    wlen = win.shape[2]
    nslot = G * SEL_TOPK
    sel = jnp.concatenate([jnp.take_along_axis(page_table, ids // bpp, axis=1), ids % bpp, ids], axis=1)
    in_specs = [pl.BlockSpec((SROWS, qw), lambda b, s: (b, q_idx)),
                pl.BlockSpec((SROWS, 6 * G * dh), lambda b, s: (b, kv_idx)),
                pl.BlockSpec((SROWS, LANE), lambda b, s: (b, 0)),
                pl.BlockSpec((SROWS, qw), lambda b, s: (b, 0)),
                pl.BlockSpec((None, None, wlen, 2, G, dh), lambda b, s: (l, b, 0, 0, 0, 0))]
    for g in range(G):
        for i in range(nsel):
            in_specs.append(pl.BlockSpec(
                (None, None, SEL_BLOCK, 2, G, dh),
                lambda b, s, j=g * SEL_TOPK + i: (l, s[b, j], s[b, nslot + j], 1, 0, 0)))
    gs = pltpu.PrefetchScalarGridSpec(
        num_scalar_prefetch=1, grid=(nb,), in_specs=in_specs,
        out_specs=pl.BlockSpec((SROWS, qw), lambda b, s: (b, 0)))
    kern = functools.partial(_nsa_gather_kernel, DH=dh, NSEL=nsel, qpos=qpos, g_lane=g_lane)
    return pl.pallas_call(
        kern, grid_spec=gs,
        out_shape=jax.ShapeDtypeStruct((nb * SROWS, qw), F32),
        compiler_params=_cparams(("arbitrary",)), name="nsa_sel_win",
    )(sel, z_s, z_s, zsm_s, ocmp, win, *([cache] * (G * nsel)))


def _split_cols_kernel(*refs, n_in, G, DH):
    z_refs = refs[:n_in]
    o_ref = refs[-1]
    per = z_refs[0].shape[1] // DH
    for idx in range(n_in * per):
        o_ref[:, idx // G, idx % G, :] = z_refs[idx // per][:, (idx % per) * DH:(idx % per + 1) * DH]


def _split_cols(z, col0, ncomp, prev, l, depth, nblk, rows, row_block, G, dh):
    width = ncomp * G * dh
    n_in = max(1, width // Z_TILE)
    wblk = width // n_in
    assert col0 % wblk == 0
    kern = functools.partial(_split_cols_kernel, n_in=n_in, G=G, DH=dh)
    in_specs = [pl.BlockSpec((rows, wblk), lambda j, k=k: (row_block(j), col0 // wblk + k)) for k in range(n_in)]
    args = [z] * n_in
    aliases = {}
    if prev is not None:
        in_specs.append(pl.BlockSpec(memory_space=pl.ANY))
        args.append(prev)
        aliases = {n_in: 0}
    return pl.pallas_call(
        kern, grid=(nblk,), in_specs=in_specs,
        out_specs=pl.BlockSpec((None, None, rows, ncomp, G, dh), lambda j: (l, j, 0, 0, 0, 0)),
        out_shape=jax.ShapeDtypeStruct((depth, nblk, rows, ncomp, G, dh), z.dtype),
        input_output_aliases=aliases,
        compiler_params=_cparams(("arbitrary",)), name="nsa_rows_out",
    )(*args)


def _lane_vec(depth, lane, values):
    n = values.shape[1]
    return jnp.zeros((depth, 1, LANE), F32).at[:, 0, lane:lane + n].set(values.astype(F32))


def kernel(x_prompt, x_sample, cache_nsa_kv, cache_nsa_win, state_gla, state_ssd, state_ssd_conv, state_mlstm_c, state_mlstm_n, state_mlstm_m, page_table, c_prompt, c_sample, ada_w, ada_b, norm_mix_g, norm_ffn_g, w_in, gla_w_a, gla_b_a, gla_norm_g, ssd_conv_w, ssd_conv_b, ssd_dt_bias, ssd_a_log, ssd_d, ssd_norm_g, ml_b_i, ml_b_f, ml_norm_g, w_branch, w_out, ffn_w_gate, ffn_w_up, ffn_w_down, final_norm_g):
    nbp, seq, d = x_prompt.shape
    nbs, dec_seq, _ = x_sample.shape
    assert dec_seq == 1 and seq % CHUNK == 0
    depth = w_in.shape[0]
    past_len = page_table.shape[1] * PAGE_SIZE
    bw = d // 4
    dh = bw // NSA_HEADS
    gw = NSA_KV * dh
    src, dst, n_packed, _ = _packed_offsets(d)
    kv_off = dst['nsa_kv']
    q_idx = dst['nsa_q'] // bw
    kv_idx = kv_off // (6 * gw)
    xbc_idx = dst['ssd_xbc'] // src['ssd_xbc'][1]
    zg_idx = dst['ssd_z'] // bw
    mlqk_idx = dst['ml_q'] // (2 * bw)
    mlvo_idx = dst['ml_v'] // (2 * bw)
    for name, width in (('gla_q', 3 * bw), ('nsa_q', bw), ('nsa_kv', 6 * gw), ('ssd_xbc', src['ssd_xbc'][1]),
                        ('ssd_z', bw), ('ml_q', 2 * bw), ('ml_v', 2 * bw), ('merge', Z_TILE)):
        assert dst[name] % width == 0
    assert dst['gla_q'] == 0
    g_lane, dt_lane, i_lane, f_lane = dst['nsa_g'], dst['ssd_dt'], dst['ml_i'], dst['ml_f']

    crow = 2 * SUBLANE
    c_all = jnp.concatenate([c_prompt, c_sample, jnp.zeros((crow - nbp - nbs, d), F32)], axis=0)
    mod = _ada(c_all, ada_w, ada_b)
    modp = mod[:, :nbp].reshape(depth, nbp, 1, 6 * d)
    mods = jnp.repeat(mod[:, nbp:nbp + nbs], SROWS, axis=1)

    w_mix, w_gate, w_small = _pack_w_in(w_in)
    n_mix = dst['merge']
    assert n_mix % (4 * LANE) == 0
    r3 = lambda a: a.reshape(depth, 1, a.shape[-1])
    wa_pad = jnp.zeros((depth, LANE, gla_w_a.shape[2]), F32).at[:, dst['gla_a']:dst['gla_a'] + GLA_RANK].set(gla_w_a)
    dtb = _lane_vec(depth, dt_lane, ssd_dt_bias)
    alog = _lane_vec(depth, dt_lane, ssd_a_log)
    dvec = jnp.repeat(ssd_d, bw // SSD_HEADS, axis=1).reshape(depth, 1, bw)
    bif = _lane_vec(depth, i_lane, ml_b_i) + _lane_vec(depth, f_lane, ml_b_f)

    xp = x_prompt.reshape(nbp * seq, d)
    xs = jnp.zeros((nbs, SROWS, d), F32).at[:, 0].set(x_sample[:, 0]).reshape(nbs * SROWS, d)

    zeros_like_p = lambda a: jnp.zeros((nbp,) + a.shape[2:], F32)
    gla0, ssd0, conv0, mlc0 = (zeros_like_p(a) for a in (state_gla, state_ssd, state_ssd_conv, state_mlstm_c))
    nm0 = jnp.zeros((nbp, SUBLANE, LANE), F32)
    nm_s = (jnp.zeros((depth, nbs, SUBLANE, LANE), F32).at[:, :, :ML_HEADS, :].set(state_mlstm_n)
            .at[:, :, ML_HEADS, :ML_HEADS].set(state_mlstm_m))

    out = {k: [] for k in ('rows_s', 'win_s', 'gla_p', 'gla_s', 'ssd_p', 'ssd_s', 'conv_p',
                           'conv_s', 'mlc_p', 'mlc_s', 'mln_p', 'mln_s', 'mlm_p', 'mlm_s')}
    rows_p = win_p = None
    rows_tile = 1024
    keep = min(WINDOW, seq)
    assert seq % keep == 0 and (nbp * seq) % rows_tile == 0
    for l in range(depth):
        hp = _norm_mod_prompt(xp, r3(norm_mix_g), modp, l, 0, 1, seq)
        hs = _norm_mod_sample(xs, r3(norm_mix_g), mods, l, 0, 1)
        zp, zs = _proj_nt(hp, hs, w_mix, l, 1024, n_mix // 4, "proj_in")
        zgp, zgs = _proj_nt(hp, hs, w_small, l, 2048, LANE, "proj_small")
        gtp, gts = _proj_nt(hp, hs, w_gate, l, 1024, d, "proj_gate", BF16)

        o_gla_p, gla_p = _gla_call(zp, zgp, wa_pad, r3(gla_b_a), r3(gla_norm_g), gla0, l, nbp, seq, CHUNK,
                                   CHUNK, BF16)
        o_gla_s, gla_s = _gla_call(zs, zgs, wa_pad, r3(gla_b_a), r3(gla_norm_g), state_gla[l], l, nbs, SROWS,
                                   SROWS, 1, F32)
        ssd_args = (ssd_conv_w, r3(ssd_conv_b), dtb, alog, dvec, r3(ssd_norm_g))
        o_ssd_p, ssd_p, conv_p = _ssd_call(zp, zgp, xbc_idx, zg_idx, *ssd_args, ssd0, conv0, l, nbp, seq,
                                           CHUNK, CHUNK, dt_lane, BF16)
        o_ssd_s, ssd_s, conv_s = _ssd_call(zs, zgs, xbc_idx, zg_idx, *ssd_args, state_ssd[l],
                                           state_ssd_conv[l], l, nbs, SROWS, SROWS, 1, dt_lane, F32)
        o_ml_p, mlc_p, mlnm_p = _mlstm_call(zp, zgp, mlqk_idx, mlvo_idx, bif, r3(ml_norm_g), mlc0, nm0, l,
                                            nbp, seq, CHUNK, CHUNK, i_lane, f_lane, BF16)
        o_ml_s, mlc_s, mlnm_s = _mlstm_call(zs, zgs, mlqk_idx, mlvo_idx, bif, r3(ml_norm_g), state_mlstm_c[l],
                                            nm_s[l], l, nbs, SROWS, SROWS, 1, i_lane, f_lane, F32)
        o_nsa_p = _nsa_prompt_call(zp, zgp, q_idx, kv_idx, nbp, seq, g_lane)
        kvc = _nsa_means_call(cache_nsa_kv, page_table, l)
        ocmp, ids = _nsa_score_call(zs, q_idx, kvc, past_len)
        ids2 = ids[:, :NSA_KV, :SEL_TOPK].reshape(nbs, NSA_KV * SEL_TOPK)
        o_nsa_s = _nsa_gather_call(zs, zgs, q_idx, kv_idx, ocmp, cache_nsa_win, cache_nsa_kv, page_table, ids2, l,
                                   past_len, g_lane)

        mp, ms = _merge((o_gla_p, o_nsa_p, o_ssd_p, o_ml_p), (o_gla_s, o_nsa_s, o_ssd_s, o_ml_s), gtp, gts, w_branch,
                        l, 1024, 512)
        xp, xs = _mm_resid(mp, ms, w_out, xp, xs, modp, mods, l, 2, seq, 1024, 512, "proj_out")
        hp = _norm_mod_prompt(xp, r3(norm_ffn_g), modp, l, 3, 4, seq)
        hs = _norm_mod_sample(xs, r3(norm_ffn_g), mods, l, 3, 4)
        ap, a_s = _mm_swiglu(hp, hs, ffn_w_gate, ffn_w_up, l, 1024, 512)
        xp, xs = _mm_resid(ap, a_s, ffn_w_down, xp, xs, modp, mods, l, 5, seq, 512, 512, "ffn_down")

        zs3 = zs.reshape(nbs, SROWS, n_mix)[:, :1]
        rows_p = _split_cols(zp, kv_off, 4, rows_p, l, depth, nbp * seq // rows_tile, rows_tile, lambda j: j,
                             NSA_KV, dh)
        win_p = _split_cols(zp, kv_off + 4 * gw, 2, win_p, l, depth, nbp, keep,
                            lambda j: (j + 1) * (seq // keep) - 1, NSA_KV, dh)
        out['rows_s'].append(zs3[:, :, kv_off:kv_off + 4 * gw].reshape(nbs, 1, 4, NSA_KV, dh))
        win_all = jnp.concatenate([cache_nsa_win[l], zs3[:, :, kv_off + 4 * gw:kv_off + 6 * gw]
                                   .reshape(nbs, 1, 2, NSA_KV, dh)], axis=1)
        out['win_s'].append(win_all[:, win_all.shape[1] - min(WINDOW, win_all.shape[1]):])
        for key, val in (('gla_p', gla_p), ('gla_s', gla_s), ('ssd_p', ssd_p), ('ssd_s', ssd_s), ('conv_p', conv_p),
                         ('conv_s', conv_s), ('mlc_p', mlc_p), ('mlc_s', mlc_s)):
            out[key].append(val)
        for tag, nm in (('p', mlnm_p), ('s', mlnm_s)):
            out['mln_' + tag].append(nm[:, :ML_HEADS, :])
            out['mlm_' + tag].append(nm[:, ML_HEADS, :ML_HEADS])

    y_prompt = _final_norm(xp, final_norm_g).reshape(nbp, seq, d)
    y_sample = _final_norm(xs, final_norm_g).reshape(nbs, SROWS, d)[:, :1]
    st = {k: jnp.stack(v) for k, v in out.items()}
    st['rows_p'] = rows_p.reshape(depth, nbp, seq, 4, NSA_KV, dh)
    st['win_p'] = win_p
    return (y_prompt, y_sample, st['rows_p'], st['rows_s'], st['win_p'], st['win_s'], st['gla_p'], st['gla_s'],
            st['ssd_p'], st['ssd_s'], st['conv_p'], st['conv_s'], st['mlc_p'], st['mlc_s'], st['mln_p'], st['mln_s'],
            st['mlm_p'], st['mlm_s'])
```

```python
import functools
import math

import jax
import jax.numpy as jnp
from jax import lax
from jax.experimental import pallas as pl
from jax.experimental.pallas import tpu as pltpu

F32 = jnp.float32
BF16 = jnp.bfloat16

PAGE_SIZE = 128
GLA_HEADS = 4
GLA_RANK = 16
GLA_TAU = 16.0
NSA_HEADS = 4
NSA_KV = 2
NSA_R = NSA_HEADS // NSA_KV
CMP_BLOCK = 32
SEL_BLOCK = 64
SEL_TOPK = 16
WINDOW = 512
FORCE_SCORE = 1e4
NEG = -1e30
SSD_HEADS = 8
SSD_GROUPS = 2
SSD_N = 128
SSD_CONV = 4
ML_HEADS = 4
CHUNK = 128
EPS = 1e-6

LANE = 128
SUBLANE = 8
VMEM_LIMIT = 56 * 1024 * 1024

SROWS = SUBLANE


def _cparams(sem):
    return pltpu.CompilerParams(dimension_semantics=sem, vmem_limit_bytes=VMEM_LIMIT)


def _layout(d_model):
    bw = d_model // 4
    gdk = (bw // GLA_HEADS) // 2
    conv_ch = bw + 2 * SSD_GROUPS * SSD_N
    splits = (('gla_q', GLA_HEADS * gdk), ('gla_k', GLA_HEADS * gdk), ('gla_v', bw), ('gla_r', bw),
              ('gla_a', GLA_RANK), ('nsa_q', bw), ('nsa_kv', 6 * NSA_KV * (bw // NSA_HEADS)),
              ('nsa_g', 3 * NSA_HEADS), ('ssd_z', bw), ('ssd_xbc', conv_ch), ('ssd_dt', SSD_HEADS),
              ('ml_q', bw), ('ml_k', bw), ('ml_v', bw), ('ml_o', bw), ('ml_i', ML_HEADS), ('ml_f', ML_HEADS),
              ('merge', 4 * d_model))
    src = {}
    off = 0
    for name, w in splits:
        src[name] = (off, w)
        off += w
    return src, off


_Z_ORDER = ('gla_q', 'gla_k', 'gla_v', 'gla_r', 'nsa_kv', 'ml_q', 'ml_k', 'ml_v', 'ml_o', 'ssd_xbc',
            'nsa_q', 'ssd_z', 'merge')
_Z_SMALL = ('gla_a', 'nsa_g', 'ssd_dt', 'ml_i', 'ml_f')


Z_TILE = 512
SMALL_TILE = LANE // 4


def _packed_offsets(d_model):
    src, d_in = _layout(d_model)
    dst = {}
    off = 0
    for name in _Z_ORDER:
        dst[name] = off
        off += src[name][1]
    starts = []
    for name in _Z_SMALL:
        o, w = src[name]
        if not starts or o + w > starts[-1] + SMALL_TILE:
            starts.append(o)
        dst[name] = (len(starts) - 1) * SMALL_TILE + o - starts[-1]
    assert len(starts) * SMALL_TILE <= LANE and starts[-1] + SMALL_TILE <= d_in
    dst['_small_starts'] = starts
    return src, dst, off, d_in


def _pack_rows_kernel(rows_ref, w_ref, o_ref):
    o_ref[...] = pltpu.einshape("nlk->lnk", w_ref[...]).astype(o_ref.dtype)


def _pack_rows(w_t, row_starts, tile):
    _, depth, d = w_t.shape
    nt = len(row_starts)
    gs = pltpu.PrefetchScalarGridSpec(
        num_scalar_prefetch=1, grid=(nt,),
        in_specs=[pl.BlockSpec((pl.Element(tile), pl.Element(depth), pl.Element(d)), lambda t, r: (r[t], 0, 0))],
        out_specs=pl.BlockSpec((depth, tile, d), lambda t, r: (0, t, 0)))
    return pl.pallas_call(
        _pack_rows_kernel, grid_spec=gs, out_shape=jax.ShapeDtypeStruct((depth, nt * tile, d), BF16),
        compiler_params=_cparams(("arbitrary",)), name="pack_w_in",
    )(jnp.asarray(row_starts, jnp.int32), w_t)


def _pack_w_in(w_in):
    depth, d_model, _ = w_in.shape
    src, dst, n_packed, d_in = _packed_offsets(d_model)
    assert w_in.shape[2] == d_in
    w_t = jnp.transpose(w_in, (2, 0, 1))
    tile = LANE

    def starts(names):
        out = []
        for n in names:
            assert src[n][1] % tile == 0
            out += [src[n][0] + i for i in range(0, src[n][1], tile)]
        return out

    mix = _pack_rows(w_t, starts(_Z_ORDER[:-1]), tile)
    gate = _pack_rows(w_t, starts(_Z_ORDER[-1:]), tile)
    small_starts = dst['_small_starts']
    small = _pack_rows(w_t, small_starts + [small_starts[-1]] * (LANE // SMALL_TILE - len(small_starts)), SMALL_TILE)
    return mix, gate, small


def _sigmoid(x):
    return 1.0 / (1.0 + jnp.exp(-x))


def _silu(x):
    return x * _sigmoid(x)


def _log_sigmoid(x):
    return jnp.minimum(x, 0.0) - jnp.log1p(jnp.exp(-jnp.abs(x)))


def _softplus(x):
    return jnp.maximum(x, 0.0) + jnp.log1p(jnp.exp(-jnp.abs(x)))


def _dot(a, b):
    return jnp.dot(a, b, preferred_element_type=F32)


def _dot_nt(a, b):
    return lax.dot_general(a, b, (((1,), (1,)), ((), ())), preferred_element_type=F32)


def _dot_tn(a, b):
    return lax.dot_general(a, b, (((0,), (0,)), ((), ())), preferred_element_type=F32)


def _bf(x):
    return x.astype(BF16)


def _split2(x):
    h = x.astype(BF16)
    return h, (x - h.astype(F32)).astype(BF16)


def _causal(c):
    r = lax.broadcasted_iota(jnp.int32, (c, c), 0)
    s = lax.broadcasted_iota(jnp.int32, (c, c), 1)
    return r >= s


def _cumsum_rows(x):
    n = x.shape[0]
    row = lax.broadcasted_iota(jnp.int32, (n, 1), 0)
    step = 1
    while step < n:
        x = x + jnp.where(row >= step, pltpu.roll(x, step, 0), 0.0)
        step *= 2
    return x


def _rms(x, g):
    return x * lax.rsqrt(jnp.mean(x * x, axis=-1, keepdims=True) + EPS) * g


def _rms_mxu(x, g):
    n = x.shape[-1]
    assert n == LANE
    hi, lo = _split2(x * x)
    avg = jnp.full((n, LANE), 1.0 / n, BF16)
    return x * lax.rsqrt(_dot(hi, avg) + _dot(lo, avg) + EPS) * g


def _softmax_rows(s):
    m = jnp.max(s, axis=-1, keepdims=True)
    e = jnp.exp(s - m)
    return e / jnp.sum(e, axis=-1, keepdims=True)


def _alibi_slopes():
    return [2.0 ** (-8.0 * (h + 1) / NSA_HEADS) for h in range(NSA_HEADS)]


def _ada_kernel(c_ref, w_ref, b_ref, o_ref):
    a = _bf(_silu(c_ref[...]))
    o_ref[...] = _dot(a, _bf(w_ref[...])) + b_ref[...]


def _ada(c_all, ada_w, ada_b):
    depth, d, n = ada_w.shape
    rows = c_all.shape[0]
    tn = 1024
    return pl.pallas_call(
        _ada_kernel, grid=(depth, n // tn),
        in_specs=[pl.BlockSpec((rows, d), lambda l, j: (0, 0)),
                  pl.BlockSpec((None, d, tn), lambda l, j: (l, 0, j)),
                  pl.BlockSpec((None, 1, tn), lambda l, j: (l, 0, j))],
        out_specs=pl.BlockSpec((None, rows, tn), lambda l, j: (l, 0, j)),
        out_shape=jax.ShapeDtypeStruct((depth, rows, n), F32),
        compiler_params=_cparams(("arbitrary", "arbitrary")), name="ada",
    )(c_all, ada_w, ada_b.reshape(depth, 1, n))


def _norm_mod_kernel(x_ref, g_ref, sc_ref, sh_ref, o_ref):
    y = _rms(x_ref[...], g_ref[...])
    o_ref[...] = (y * (1.0 + sc_ref[...]) + sh_ref[...]).astype(o_ref.dtype)


def _norm_kernel(x_ref, g_ref, o_ref):
    o_ref[...] = _rms(x_ref[...], g_ref[...]).astype(o_ref.dtype)


def _norm_mod_prompt(x, g, modp, l, j_shift, j_scale, seq):
    m, d = x.shape
    tm = 1024
    per = seq // tm
    return pl.pallas_call(
        _norm_mod_kernel, grid=(m // tm,),
        in_specs=[pl.BlockSpec((tm, d), lambda i: (i, 0)),
                  pl.BlockSpec((None, 1, d), lambda i: (l, 0, 0)),
                  pl.BlockSpec((None, None, 1, d), lambda i: (l, i // per, 0, j_scale)),
                  pl.BlockSpec((None, None, 1, d), lambda i: (l, i // per, 0, j_shift))],
        out_specs=pl.BlockSpec((tm, d), lambda i: (i, 0)),
        out_shape=jax.ShapeDtypeStruct((m, d), BF16),
        compiler_params=_cparams(("arbitrary",)), name="norm_mod_prompt",
    )(x, g, modp, modp)


def _norm_mod_sample(x, g, mods, l, j_shift, j_scale):
    m, d = x.shape
    return pl.pallas_call(
        _norm_mod_kernel, grid=(1,),
        in_specs=[pl.BlockSpec((m, d), lambda i: (0, 0)),
                  pl.BlockSpec((None, 1, d), lambda i: (l, 0, 0)),
                  pl.BlockSpec((None, m, d), lambda i: (l, 0, j_scale)),
                  pl.BlockSpec((None, m, d), lambda i: (l, 0, j_shift))],
        out_specs=pl.BlockSpec((m, d), lambda i: (0, 0)),
        out_shape=jax.ShapeDtypeStruct((m, d), BF16),
        compiler_params=_cparams(("arbitrary",)), name="norm_mod_sample",
    )(x, g, mods, mods)


def _final_norm(x, g):
    m, d = x.shape
    tm = min(m, 512)
    return pl.pallas_call(
        _norm_kernel, grid=(m // tm,),
        in_specs=[pl.BlockSpec((tm, d), lambda i: (i, 0)), pl.BlockSpec((1, d), lambda i: (0, 0))],
        out_specs=pl.BlockSpec((tm, d), lambda i: (i, 0)),
        out_shape=jax.ShapeDtypeStruct((m, d), F32),
        compiler_params=_cparams(("arbitrary",)), name="final_norm",
    )(x, g.reshape(1, d))


def _proj_nt_kernel(a_ref, as_ref, wt_ref, o_ref, os_ref):
    w = wt_ref[...]

    @pl.when(pl.program_id(1) == 0)
    def _():
        os_ref[...] = _dot_nt(as_ref[...], w).astype(os_ref.dtype)

    o_ref[...] = _dot_nt(a_ref[...], w).astype(o_ref.dtype)


def _proj_nt(a, a_s, w_t, l, tm, tn, name, out_dtype=F32):
    m, k = a.shape
    ms = a_s.shape[0]
    n = w_t.shape[1]
    return pl.pallas_call(
        _proj_nt_kernel, grid=(n // tn, m // tm),
        in_specs=[pl.BlockSpec((tm, k), lambda j, i: (i, 0)),
                  pl.BlockSpec((ms, k), lambda j, i: (0, 0)),
                  pl.BlockSpec((None, tn, k), lambda j, i: (l, j, 0))],
        out_specs=[pl.BlockSpec((tm, tn), lambda j, i: (i, j)),
                   pl.BlockSpec((ms, tn), lambda j, i: (0, j))],
        out_shape=[jax.ShapeDtypeStruct((m, n), out_dtype), jax.ShapeDtypeStruct((ms, n), out_dtype)],
        compiler_params=_cparams(("arbitrary", "arbitrary")), name=name,
    )(a, a_s, w_t)


def _mm_resid_kernel(a_ref, as_ref, w_ref, x_ref, xs_ref, gt_ref, gts_ref, o_ref, os_ref, wb_ref):
    @pl.when(pl.program_id(1) == 0)
    def _():
        wb_ref[...] = _bf(w_ref[...])
        os_ref[...] = xs_ref[...] + gts_ref[...] * _dot(as_ref[...], wb_ref[...])

    o_ref[...] = x_ref[...] + gt_ref[...] * _dot(a_ref[...], wb_ref[...])


def _mm_resid(a, a_s, w, x, x_s, modp, mods, l, j_gate, seq, tm, tn, name):
    m, k = a.shape
    ms = a_s.shape[0]
    n = w.shape[2]
    per = seq // tm
    nj = n // tn
    return pl.pallas_call(
        _mm_resid_kernel, grid=(nj, m // tm),
        in_specs=[pl.BlockSpec((tm, k), lambda j, i: (i, 0)),
                  pl.BlockSpec((ms, k), lambda j, i: (0, 0)),
                  pl.BlockSpec((None, k, tn), lambda j, i: (l, 0, j)),
                  pl.BlockSpec((tm, tn), lambda j, i: (i, j)),
                  pl.BlockSpec((ms, tn), lambda j, i: (0, j)),
                  pl.BlockSpec((None, None, 1, tn), lambda j, i: (l, i // per, 0, j_gate * nj + j)),
                  pl.BlockSpec((None, ms, tn), lambda j, i: (l, 0, j_gate * nj + j))],
        out_specs=[pl.BlockSpec((tm, tn), lambda j, i: (i, j)),
                   pl.BlockSpec((ms, tn), lambda j, i: (0, j))],
        out_shape=[jax.ShapeDtypeStruct((m, n), F32), jax.ShapeDtypeStruct((ms, n), F32)],
        scratch_shapes=[pltpu.VMEM((k, tn), BF16)],
        compiler_params=_cparams(("arbitrary", "arbitrary")), name=name,
    )(a, a_s, w, x, x_s, modp, mods)


def _mm_swiglu_kernel(a_ref, as_ref, wg_ref, wu_ref, o_ref, os_ref, wgb_ref, wub_ref):
    @pl.when(pl.program_id(1) == 0)
    def _():
        wgb_ref[...] = _bf(wg_ref[...])
        wub_ref[...] = _bf(wu_ref[...])
        a_s = as_ref[...]
        os_ref[...] = (_silu(_dot(a_s, wgb_ref[...])) * _dot(a_s, wub_ref[...])).astype(os_ref.dtype)

    a = a_ref[...]
    o_ref[...] = (_silu(_dot(a, wgb_ref[...])) * _dot(a, wub_ref[...])).astype(o_ref.dtype)


def _mm_swiglu(a, a_s, wg, wu, l, tm, tn):
    m, k = a.shape
    ms = a_s.shape[0]
    n = wg.shape[2]
    return pl.pallas_call(
        _mm_swiglu_kernel, grid=(n // tn, m // tm),
        in_specs=[pl.BlockSpec((tm, k), lambda j, i: (i, 0)),
                  pl.BlockSpec((ms, k), lambda j, i: (0, 0)),
                  pl.BlockSpec((None, k, tn), lambda j, i: (l, 0, j)),
                  pl.BlockSpec((None, k, tn), lambda j, i: (l, 0, j))],
        out_specs=[pl.BlockSpec((tm, tn), lambda j, i: (i, j)),
                   pl.BlockSpec((ms, tn), lambda j, i: (0, j))],
        out_shape=[jax.ShapeDtypeStruct((m, n), BF16), jax.ShapeDtypeStruct((ms, n), BF16)],
        scratch_shapes=[pltpu.VMEM((k, tn), BF16), pltpu.VMEM((k, tn), BF16)],
        compiler_params=_cparams(("arbitrary", "arbitrary")), name="ffn_gate_up",
    )(a, a_s, wg, wu)


def _merge_kernel(*refs):
    o_p = refs[0:4]
    o_s = refs[4:8]
    g_p = refs[8:12]
    g_s = refs[12:16]
    w_ref, out_ref, outs_ref, wb_ref = refs[16:20]

    @pl.when(pl.program_id(1) == 0)
    def _():
        wb_ref[...] = _bf(w_ref[...])
        acc = None
        for b in range(4):
            t = _sigmoid(g_s[b][...].astype(F32)) * _dot(_bf(o_s[b][...]), wb_ref[b])
            acc = t if acc is None else acc + t
        outs_ref[...] = acc.astype(outs_ref.dtype)

    acc = None
    for b in range(4):
        t = _sigmoid(g_p[b][...].astype(F32)) * _dot(_bf(o_p[b][...]), wb_ref[b])
        acc = t if acc is None else acc + t
    out_ref[...] = acc.astype(out_ref.dtype)


def _merge(o_p, o_s, z, z_s, w_branch, l, tm, tn):
    m = z.shape[0]
    ms = z_s.shape[0]
    bw = w_branch.shape[2]
    d = w_branch.shape[3]
    in_specs = ([pl.BlockSpec((tm, bw), lambda j, i: (i, 0)) for _ in range(4)]
                + [pl.BlockSpec((ms, bw), lambda j, i: (0, 0)) for _ in range(4)]
                + [pl.BlockSpec((tm, tn), lambda j, i, b=b: (i, b * (d // tn) + j)) for b in range(4)]
                + [pl.BlockSpec((ms, tn), lambda j, i, b=b: (0, b * (d // tn) + j)) for b in range(4)]
                + [pl.BlockSpec((None, 4, bw, tn), lambda j, i: (l, 0, 0, j))])
    return pl.pallas_call(
        _merge_kernel, grid=(d // tn, m // tm),
        in_specs=in_specs,
        out_specs=[pl.BlockSpec((tm, tn), lambda j, i: (i, j)), pl.BlockSpec((ms, tn), lambda j, i: (0, j))],
        out_shape=[jax.ShapeDtypeStruct((m, d), BF16), jax.ShapeDtypeStruct((ms, d), BF16)],
        scratch_shapes=[pltpu.VMEM((4, bw, tn), BF16)],
        compiler_params=_cparams(("arbitrary", "arbitrary")), name="merge",
    )(*o_p, *o_s, z, z, z, z, z_s, z_s, z_s, z_s, w_branch)


def _live_rows(c_rows, n_live):
    return lax.broadcasted_iota(jnp.int32, (c_rows, 1), 0) < n_live


def _gla_kernel(z_ref, small_ref, wa_ref, ba_ref, gn_ref, s0_ref, o_ref, sout_ref, st_scr, *, NB, C, L, H, DK, DV):
    c = pl.program_id(0)

    @pl.when(c == 0)
    def _():
        st_scr[...] = s0_ref[...]

    causal = _causal(C)
    wa = _bf(wa_ref[...])
    ba = ba_ref[...]
    gn = gn_ref[...]
    mid = C // 2

    def one_sequence(bi):
        z = z_ref[bi]
        q = z[:, 0:H * DK] * (DK ** -0.5)
        k = z[:, H * DK:2 * H * DK]
        v = z[:, 2 * H * DK:2 * H * DK + H * DV]
        r = z[:, 2 * H * DK + H * DV:]
        ga = _dot(_bf(small_ref[bi]), wa) + ba
        glog = _log_sigmoid(ga) * (1.0 / GLA_TAU)
        if L < C:
            live = _live_rows(C, L)
            glog = jnp.where(live, glog, 0.0)
            k = jnp.where(live, k, 0.0)
        b = _cumsum_rows(glog)
        bl = b[C - 1:C, :]
        bm = b[mid - 1:mid, :]
        qi = _bf(q * jnp.exp(b - bm))
        ki = _bf(k * jnp.exp(bm - b))
        qs = q * jnp.exp(b)
        kd = _bf(k * jnp.exp(bl - b))
        ebl_col = jnp.exp(b.T[:, C - 1:C])
        outs = []
        for h in range(H):
            ks = slice(h * DK, (h + 1) * DK)
            vs = slice(h * DV, (h + 1) * DV)
            vh = v[:, vs]
            a = jnp.where(causal, _dot_nt(qi[:, ks], ki[:, ks]), 0.0)
            st = st_scr[bi, h]
            o = _dot(_bf(jnp.concatenate([a, qs[:, ks]], axis=1)), _bf(jnp.concatenate([vh, st], axis=0)))
            st_scr[bi, h] = ebl_col[ks, :] * st + _dot_tn(kd[:, ks], _bf(vh))
            outs.append(_rms(o, gn[:, vs]))
        o_ref[bi] = (jnp.concatenate(outs, axis=1) * _silu(r)).astype(o_ref.dtype)

    for bi in range(NB):
        one_sequence(bi)

    @pl.when(c == pl.num_programs(0) - 1)
    def _():
        sout_ref[...] = st_scr[...]


def _gla_call(z, zsm, wa_pad, ba, gn, s0, l, nb, seq, C, L, out_dtype):
    H, DK, DV = s0.shape[1:]
    nch = seq // C
    width = 2 * H * DK + 2 * H * DV
    z3 = z.reshape(nb, seq, z.shape[1])
    zsm3 = zsm.reshape(nb, seq, LANE)
    kern = functools.partial(_gla_kernel, NB=nb, C=C, L=L, H=H, DK=DK, DV=DV)
    o, s = pl.pallas_call(
        kern, grid=(nch,),
        in_specs=[pl.BlockSpec((nb, C, width), lambda c: (0, c, 0)),
                  pl.BlockSpec((nb, C, LANE), lambda c: (0, c, 0)),
                  pl.BlockSpec((None, LANE, H * DK), lambda c: (l, 0, 0)),
                  pl.BlockSpec((None, 1, H * DK), lambda c: (l, 0, 0)),
                  pl.BlockSpec((None, 1, H * DV), lambda c: (l, 0, 0)),
                  pl.BlockSpec((nb, H, DK, DV), lambda c: (0, 0, 0, 0))],
        out_specs=[pl.BlockSpec((nb, C, H * DV), lambda c: (0, c, 0)),
                   pl.BlockSpec((nb, H, DK, DV), lambda c: (0, 0, 0, 0))],
        out_shape=[jax.ShapeDtypeStruct((nb, seq, H * DV), out_dtype),
                   jax.ShapeDtypeStruct((nb, H, DK, DV), F32)],
        scratch_shapes=[pltpu.VMEM((nb, H, DK, DV), F32)],
        compiler_params=_cparams(("arbitrary",)), name="gla",
    )(z3, zsm3, wa_pad, ba, gn, s0)
    return o.reshape(nb * seq, H * DV), s


def _ssd_kernel(xbc_ref, zg_ref, small_ref, cw_ref, cb_ref, dtb_ref, alog_ref, dvec_ref, gn_ref, s0_ref, conv0_ref,
                o_ref, sout_ref, convout_ref, s_scr, ext_scr, *, NB, C, L, H, G, N, P, dt_lane):
    c = pl.program_id(0)
    kh = SSD_CONV - 1
    base = SUBLANE - kh

    @pl.when(c == 0)
    def _():
        s_scr[...] = s0_ref[...]
        ext_scr[:, base:SUBLANE, :] = conv0_ref[...]

    cw = cw_ref[...]
    cb = cb_ref[...]
    dtb = dtb_ref[...]
    neg_a = -jnp.exp(alog_ref[...])
    dvec = dvec_ref[...]
    gn = gn_ref[...]
    causal = _causal(C)
    bw = H * P
    hpg = H // G

    def one_sequence(bi):
        ext_scr[bi, SUBLANE:SUBLANE + C, :] = xbc_ref[bi]
        conv = cb
        for j in range(SSD_CONV):
            conv = conv + ext_scr[bi, pl.ds(base + j, C), :] * cw[j:j + 1, :]
        hist = ext_scr[bi, pl.ds(base + L, kh), :]
        ext_scr[bi, base:SUBLANE, :] = hist
        xc = _silu(conv)
        sx = xc[:, 0:bw]
        bm = xc[:, bw:bw + G * N]
        cm = xc[:, bw + G * N:]
        dt = _softplus(small_ref[bi] + dtb)
        logg = dt * neg_a
        if L < C:
            live = _live_rows(C, L)
            logg = jnp.where(live, logg, 0.0)
            bm = jnp.where(live, bm, 0.0)
        b = _cumsum_rows(logg)
        b_t = b.T
        ys = []
        for g in range(G):
            cg = cm[:, g * N:(g + 1) * N]
            bg = bm[:, g * N:(g + 1) * N]
            gmat = _dot_nt(_bf(cg), _bf(bg))
            for hh in range(hpg):
                h = g * hpg + hh
                ln = dt_lane + h
                bc = b[:, ln:ln + 1]
                br = b_t[ln:ln + 1, :]
                dec = jnp.exp(jnp.where(causal, bc - br, -jnp.inf))
                xdt = sx[:, h * P:(h + 1) * P] * dt[:, ln:ln + 1]
                st = s_scr[bi, h]
                y = _dot(_bf(jnp.concatenate([gmat * dec, cg * jnp.exp(bc)], axis=1)),
                         _bf(jnp.concatenate([xdt, st], axis=0)))
                bl = bc[C - 1:C, :]
                s_scr[bi, h] = jnp.exp(bl) * st + _dot_tn(_bf(bg * jnp.exp(bl - bc)), _bf(xdt))
                ys.append(y)
        y_all = jnp.concatenate(ys, axis=1) + dvec * sx
        o_ref[bi] = _rms(y_all * _silu(zg_ref[bi]), gn).astype(o_ref.dtype)

    for bi in range(NB):
        one_sequence(bi)

    @pl.when(c == pl.num_programs(0) - 1)
    def _():
        sout_ref[...] = s_scr[...]
        convout_ref[...] = ext_scr[:, base:SUBLANE, :]


def _ssd_call(z, zsm, xbc_idx, zg_idx, cw, cb, dtb, alog, dvec, gn, s0, conv0, l, nb, seq, C, L, dt_lane,
              out_dtype):
    H, N, P = s0.shape[1:]
    G = SSD_GROUPS
    ch = conv0.shape[2]
    bw = H * P
    nch = seq // C
    z3 = z.reshape(nb, seq, z.shape[1])
    zsm3 = zsm.reshape(nb, seq, LANE)
    kern = functools.partial(_ssd_kernel, NB=nb, C=C, L=L, H=H, G=G, N=N, P=P, dt_lane=dt_lane)
    vec = lambda w: pl.BlockSpec((None, 1, w), lambda c: (l, 0, 0))
    o, s, cv = pl.pallas_call(
        kern, grid=(nch,),
        in_specs=[pl.BlockSpec((nb, C, ch), lambda c: (0, c, xbc_idx)),
                  pl.BlockSpec((nb, C, bw), lambda c: (0, c, zg_idx)),
                  pl.BlockSpec((nb, C, LANE), lambda c: (0, c, 0)),
                  pl.BlockSpec((None, SSD_CONV, ch), lambda c: (l, 0, 0)),
                  vec(ch), vec(LANE), vec(LANE), vec(bw), vec(bw),
                  pl.BlockSpec((nb, H, N, P), lambda c: (0, 0, 0, 0)),
                  pl.BlockSpec((nb, SSD_CONV - 1, ch), lambda c: (0, 0, 0))],
        out_specs=[pl.BlockSpec((nb, C, bw), lambda c: (0, c, 0)),
                   pl.BlockSpec((nb, H, N, P), lambda c: (0, 0, 0, 0)),
                   pl.BlockSpec((nb, SSD_CONV - 1, ch), lambda c: (0, 0, 0))],
        out_shape=[jax.ShapeDtypeStruct((nb, seq, bw), out_dtype),
                   jax.ShapeDtypeStruct((nb, H, N, P), F32),
                   jax.ShapeDtypeStruct((nb, SSD_CONV - 1, ch), F32)],
        scratch_shapes=[pltpu.VMEM((nb, H, N, P), F32), pltpu.VMEM((nb, SUBLANE + C, ch), F32)],
        compiler_params=_cparams(("arbitrary",)), name="ssd",
    )(z3, z3, zsm3, cw, cb, dtb, alog, dvec, gn, s0, conv0)
    return o.reshape(nb * seq, bw), s, cv


def _mlstm_kernel(qk_ref, vo_ref, small_ref, bif_ref, gn_ref, c0_ref, nm0_ref, o_ref, cout_ref, nmout_ref,
                  c_scr, nm_scr, *, NB, C, L, H, DH, i_lane, f_lane):
    c = pl.program_id(0)

    @pl.when(c == 0)
    def _():
        c_scr[...] = c0_ref[...]
        nm_scr[...] = nm0_ref[...]

    causal_t = lax.broadcasted_iota(jnp.int32, (C, C), 0) <= lax.broadcasted_iota(jnp.int32, (C, C), 1)
    lane = lax.broadcasted_iota(jnp.int32, (1, LANE), 1)
    bif = bif_ref[...]
    gn = gn_ref[...]

    def one_sequence(bi):
        qk = qk_ref[bi]
        vo = vo_ref[bi]
        q = qk[:, 0:H * DH]
        k = qk[:, H * DH:] * (DH ** -0.5)
        v = vo[:, 0:H * DH]
        og = vo[:, H * DH:]
        pre = small_ref[bi] + bif
        logf = _log_sigmoid(pre)
        ipre = pre
        if L < C:
            live = _live_rows(C, L)
            logf = jnp.where(live, logf, 0.0)
            ipre = jnp.where(live, pre, NEG)
        b = _cumsum_rows(logf)
        b_t = b.T
        nm = nm_scr[bi]
        nm_bf = _bf(nm)
        m_row = nm[H:H + 1, :]
        outs = []
        for h in range(H):
            sl = slice(h * DH, (h + 1) * DH)
            br = b_t[f_lane + h:f_lane + h + 1, :]
            uc = ipre[:, i_lane + h:i_lane + h + 1] - b[:, f_lane + h:f_lane + h + 1]
            m_prev = nm[H:H + 1, h:h + 1]
            logd = jnp.where(causal_t, br + uc, -jnp.inf)
            gg = br + m_prev
            m = jnp.maximum(gg, jnp.max(logd, axis=0, keepdims=True))
            dm = jnp.exp(logd - m)
            wi = jnp.exp(gg - m)
            qf = q[:, sl]
            qh = _bf(qf)
            kh = k[:, sl]
            vh = _bf(v[:, sl])
            sc = _dot_nt(jnp.concatenate([_bf(kh), nm_bf], axis=0), qh)
            qk_t = sc[0:C, :] * dm
            cst = c_scr[bi, h]
            nq = sc[C + h:C + h + 1, :]
            den = jnp.sum(qk_t, axis=0, keepdims=True) + wi * nq
            rden = 1.0 / jnp.maximum(jnp.abs(den), jnp.exp(-m))
            rows = jnp.concatenate([wi, rden, jnp.zeros((SUBLANE - 2, C), F32)], axis=0)
            if C < LANE:
                rows = jnp.concatenate([rows, jnp.zeros((SUBLANE, LANE - C), F32)], axis=1)
            cols = rows.T[0:C, :]
            hid = (_dot_tn(_bf(qk_t), vh) + cols[:, 0:1] * _dot(qh, _bf(cst))) * cols[:, 1:2]
            m_new = m[:, C - 1:C]
            bl = br[:, C - 1:C]
            ws = jnp.exp(uc + (bl - m_new))
            wc = jnp.exp(bl + m_prev - m_new)
            nrow = nm[h:h + 1, :]
            wk = ws * kh
            c_scr[bi, h] = wc * cst + _dot_tn(_bf(wk), vh)
            nm_scr[bi, h:h + 1, :] = wc * nrow + jnp.sum(wk, axis=0, keepdims=True)
            m_row = jnp.where(lane == h, m_new, m_row)
            outs.append(_rms_mxu(hid, gn[:, sl]))
        nm_scr[bi, H:H + 1, :] = m_row
        o_ref[bi] = (jnp.concatenate(outs, axis=1) * _sigmoid(og)).astype(o_ref.dtype)

    for bi in range(NB):
        one_sequence(bi)

    @pl.when(c == pl.num_programs(0) - 1)
    def _():
        cout_ref[...] = c_scr[...]
        nmout_ref[...] = nm_scr[...]


def _mlstm_call(z, zsm, qk_idx, vo_idx, bif, gn, c0, nm0, l, nb, seq, C, L, i_lane, f_lane, out_dtype):
    H, DH = c0.shape[1:3]
    nch = seq // C
    z3 = z.reshape(nb, seq, z.shape[1])
    zsm3 = zsm.reshape(nb, seq, LANE)
    kern = functools.partial(_mlstm_kernel, NB=nb, C=C, L=L, H=H, DH=DH, i_lane=i_lane, f_lane=f_lane)
    o, cc, nm = pl.pallas_call(
        kern, grid=(nch,),
        in_specs=[pl.BlockSpec((nb, C, 2 * H * DH), lambda c: (0, c, qk_idx)),
                  pl.BlockSpec((nb, C, 2 * H * DH), lambda c: (0, c, vo_idx)),
                  pl.BlockSpec((nb, C, LANE), lambda c: (0, c, 0)),
                  pl.BlockSpec((None, 1, LANE), lambda c: (l, 0, 0)),
                  pl.BlockSpec((None, 1, H * DH), lambda c: (l, 0, 0)),
                  pl.BlockSpec((nb, H, DH, DH), lambda c: (0, 0, 0, 0)),
                  pl.BlockSpec((nb, SUBLANE, LANE), lambda c: (0, 0, 0))],
        out_specs=[pl.BlockSpec((nb, C, H * DH), lambda c: (0, c, 0)),
                   pl.BlockSpec((nb, H, DH, DH), lambda c: (0, 0, 0, 0)),
                   pl.BlockSpec((nb, SUBLANE, LANE), lambda c: (0, 0, 0))],
        out_shape=[jax.ShapeDtypeStruct((nb, seq, H * DH), out_dtype),
                   jax.ShapeDtypeStruct((nb, H, DH, DH), F32),
                   jax.ShapeDtypeStruct((nb, SUBLANE, LANE), F32)],
        scratch_shapes=[pltpu.VMEM((nb, H, DH, DH), F32), pltpu.VMEM((nb, SUBLANE, LANE), F32)],
        compiler_params=_cparams(("arbitrary",)), name="mlstm",
    )(z3, z3, zsm3, bif, gn, c0, nm0)
    return o.reshape(nb * seq, H * DH), cc, nm


def _rank_select(score, n_blocks, keep):
    lane = lax.broadcasted_iota(jnp.int32, score.shape, 1)
    cnt = jnp.zeros(score.shape, jnp.int32)
    for j in range(n_blocks):
        cj = score[:, j:j + 1]
        ahead = jnp.where(cj > score, 1, jnp.where(cj == score, jnp.where(lane > j, 1, 0), 0))
        cnt = cnt + ahead
    return cnt < keep


def _nsa_prompt_kernel(q_ref, kv_ref, small_ref, o_ref, kvb_ref, cmp_ref, *, T, TQ, DH, g_lane):
    qi = pl.program_id(1)
    G, R = NSA_KV, NSA_R
    gw = G * DH
    ncb = T // CMP_BLOCK
    nsb = T // SEL_BLOCK
    ratio = SEL_BLOCK // CMP_BLOCK
    slopes = _alibi_slopes()

    @pl.when(qi == 0)
    def _():
        kvb_ref[...] = _bf(kv_ref[...])
        hi, lo = _split2(kv_ref[:, 0:2 * gw])
        jj = lax.broadcasted_iota(jnp.int32, (ncb, T), 0) * CMP_BLOCK
        tt = lax.broadcasted_iota(jnp.int32, (ncb, T), 1)
        pm = jnp.where((tt >= jj) & (tt < jj + CMP_BLOCK), 1.0 / CMP_BLOCK, 0.0).astype(BF16)
        cmp_ref[...] = _bf(_dot(pm, hi) + _dot(pm, lo))

    q = q_ref[...]
    gate = _sigmoid(small_ref[...])
    posl = qi * TQ + lax.broadcasted_iota(jnp.int32, (1, TQ), 1)
    cbs = lax.broadcasted_iota(jnp.int32, (ncb, 1), 0)
    cvalid = (cbs * CMP_BLOCK + (CMP_BLOCK - 1)) <= posl
    cdist = posl.astype(F32) - (cbs.astype(F32) * CMP_BLOCK + (CMP_BLOCK - 1) / 2.0)
    sbs = lax.broadcasted_iota(jnp.int32, (nsb, 1), 0)
    svalid = sbs * SEL_BLOCK <= posl
    forced = (sbs == 0) | ((sbs * SEL_BLOCK <= posl) & (posl < sbs * SEL_BLOCK + SEL_BLOCK))
    pi = lax.broadcasted_iota(jnp.int32, (nsb, ncb), 0) * ratio
    pj = lax.broadcasted_iota(jnp.int32, (nsb, ncb), 1)
    pool = jnp.where((pj >= pi) & (pj < pi + ratio), 1.0, 0.0).astype(BF16)
    brow = lax.broadcasted_iota(jnp.int32, (nsb, TQ), 0)
    row2 = lax.broadcasted_iota(jnp.int32, (R * TQ, 1), 0)
    rr = lax.broadcasted_iota(jnp.int32, (TQ, TQ), 0)
    cc = lax.broadcasted_iota(jnp.int32, (TQ, TQ), 1)
    lower = rr >= cc
    lower2 = jnp.concatenate([lower] * R, axis=0)
    col = lax.broadcasted_iota(jnp.int32, (1, TQ), 1)
    nfar = WINDOW // TQ

    def tile(carry, kt, q2, slope2, kcol, vcol, mask):
        m, l, acc = carry
        k0 = pl.multiple_of(kt * TQ, TQ)
        ks = kvb_ref[pl.ds(k0, TQ), kcol:kcol + DH]
        vs = kvb_ref[pl.ds(k0, TQ), vcol:vcol + DH]
        s = _dot_nt(q2, ks) + slope2 * (k0 + col).astype(F32)
        if mask is not None:
            s = jnp.where(mask, s, NEG)
        mn = jnp.maximum(m, jnp.max(s, axis=1, keepdims=True))
        p = jnp.exp(s - mn)
        alpha = jnp.exp(m - mn)
        return (mn, alpha * l + jnp.sum(p, axis=1, keepdims=True), alpha * acc + _dot(_bf(p), vs))

    def sel_mask(sel, kt):
        kpos = kt * TQ + col
        expand = _bf(jnp.where((kpos >= sbs * SEL_BLOCK) & (kpos < sbs * SEL_BLOCK + SEL_BLOCK), 1.0, 0.0))
        return _dot_tn(sel, expand) > 0.5

    init = (jnp.full((R * TQ, 1), NEG, F32), jnp.zeros((R * TQ, 1), F32), jnp.zeros((R * TQ, DH), F32))
    o_cmp, sels, q2s, slope2s, sel_carry, o_win = [], [], [], [], [], []
    for g in range(G):
        kc = cmp_ref[:, g * DH:(g + 1) * DH]
        vc = cmp_ref[:, gw + g * DH:gw + (g + 1) * DH]
        qs = []
        imp = jnp.zeros((nsb, TQ), F32)
        for r in range(R):
            hd = g * R + r
            qh = _bf(q[:, hd * DH:(hd + 1) * DH] * (DH ** -0.5))
            qs.append(qh)
            s = jnp.where(cvalid, _dot_nt(kc, qh) - slopes[hd] * cdist, NEG)
            e = jnp.exp(s - jnp.max(s, axis=0, keepdims=True))
            p = jnp.where(cvalid, e / jnp.sum(e, axis=0, keepdims=True), 0.0)
            o_cmp.append(_dot_tn(_bf(p), vc))
            ph, plo = _split2(p)
            imp = imp + _dot(pool, ph) + _dot(pool, plo)
        score = jnp.where(svalid, jnp.where(forced, FORCE_SCORE, imp), -1.0)
        ahead = jnp.zeros((nsb, TQ), jnp.int32)
        for j in range(nsb):
            sj = score[j:j + 1, :]
            ahead = ahead + jnp.where(sj > score, 1, jnp.where(sj == score, jnp.where(brow > j, 1, 0), 0))
        sel = _bf(jnp.where((ahead < SEL_TOPK) & (score >= 0.0), 1.0, 0.0))
        q2 = jnp.concatenate(qs, axis=0)
        slope2 = slopes[g * R]
        for r in range(1, R):
            slope2 = jnp.where(row2 >= r * TQ, slopes[g * R + r], slope2)
        sels.append(sel)
        q2s.append(q2)
        slope2s.append(slope2)
        sel_carry.append(tile(init, qi, q2, slope2, 2 * gw + g * DH, 3 * gw + g * DH,
                              jnp.concatenate([sel_mask(sel, qi) & lower] * R, axis=0)))
        kcol, vcol = 4 * gw + g * DH, 5 * gw + g * DH
        carry = tile(init, qi, q2, slope2, kcol, vcol, lower2)
        for back in range(1, nfar + 1):
            kt = jnp.maximum(qi - back, 0)
            live = qi >= back
            mask = jnp.logical_and(jnp.logical_not(lower2), live) if back == nfar else live
            carry = tile(carry, kt, q2, slope2, kcol, vcol, mask)
        o_win.append(carry[2] / carry[1])

    def sel_tiles(kt, carries):
        return tuple(tile(carries[g], kt, q2s[g], slope2s[g], 2 * gw + g * DH, 3 * gw + g * DH,
                          jnp.concatenate([sel_mask(sels[g], kt)] * R, axis=0)) for g in range(G))

    sel_carry = lax.fori_loop(0, qi, sel_tiles, tuple(sel_carry))
    outs = [None] * (G * R)
    for g in range(G):
        o_slc = sel_carry[g][2] / sel_carry[g][1]
        for r in range(R):
            hd = g * R + r
            gc = gate[:, g_lane + hd:g_lane + hd + 1]
            gs = gate[:, g_lane + NSA_HEADS + hd:g_lane + NSA_HEADS + hd + 1]
            gwn = gate[:, g_lane + 2 * NSA_HEADS + hd:g_lane + 2 * NSA_HEADS + hd + 1]
            outs[hd] = gc * o_cmp[hd] + gs * o_slc[r * TQ:(r + 1) * TQ] + gwn * o_win[g][r * TQ:(r + 1) * TQ]
    o_ref[...] = jnp.concatenate(outs, axis=1).astype(o_ref.dtype)


def _nsa_prompt_call(z, zsm, q_idx, kv_idx, nb, seq, g_lane, tq=256):
    dh = LANE
    kvw = 6 * NSA_KV * dh
    qw = NSA_HEADS * dh
    nq = seq // tq
    assert WINDOW % tq == 0 and seq % tq == 0
    kern = functools.partial(_nsa_prompt_kernel, T=seq, TQ=tq, DH=dh, g_lane=g_lane)
    return pl.pallas_call(
        kern, grid=(nb, nq),
        in_specs=[pl.BlockSpec((tq, qw), lambda b, i: (b * nq + i, q_idx)),
                  pl.BlockSpec((seq, kvw), lambda b, i: (b, kv_idx)),
                  pl.BlockSpec((tq, LANE), lambda b, i: (b * nq + i, 0))],
        out_specs=pl.BlockSpec((tq, qw), lambda b, i: (b * nq + i, 0)),
        out_shape=jax.ShapeDtypeStruct((nb * seq, qw), BF16),
        scratch_shapes=[pltpu.VMEM((seq, kvw), BF16), pltpu.VMEM((seq // CMP_BLOCK, 2 * NSA_KV * dh), BF16)],
        compiler_params=_cparams(("arbitrary", "arbitrary")), name="nsa_prompt",
    )(z, z, zsm)


def _nsa_means_kernel(pt_ref, *refs, PP, DH):
    ins = refs[:PP]
    o_ref = refs[PP]
    nper = PAGE_SIZE // CMP_BLOCK
    for i in range(PP):
        x = ins[i][...].reshape(nper, CMP_BLOCK, 2, NSA_KV, DH)
        o_ref[i * nper:(i + 1) * nper] = jnp.sum(x, axis=1) * (1.0 / CMP_BLOCK)


def _nsa_means_call(cache, page_table, l, pp=32):
    nb, n_pages = page_table.shape
    dh = cache.shape[5]
    nper = PAGE_SIZE // CMP_BLOCK
    assert cache.shape[2] == PAGE_SIZE and cache.shape[4] == NSA_KV and n_pages % pp == 0
    in_specs = [pl.BlockSpec((None, None, PAGE_SIZE, 2, NSA_KV, dh),
                             lambda b, s, pt, i=i: (l, pt[b, s * pp + i], 0, 0, 0, 0)) for i in range(pp)]
    gs = pltpu.PrefetchScalarGridSpec(
        num_scalar_prefetch=1, grid=(nb, n_pages // pp), in_specs=in_specs,
        out_specs=pl.BlockSpec((None, nper * pp, 2, NSA_KV, dh), lambda b, s, pt: (b, s, 0, 0, 0)))
    means = pl.pallas_call(
        functools.partial(_nsa_means_kernel, PP=pp, DH=dh), grid_spec=gs,
        out_shape=jax.ShapeDtypeStruct((nb, n_pages * nper, 2, NSA_KV, dh), F32),
        compiler_params=_cparams(("arbitrary", "arbitrary")), name="nsa_cmp_means",
    )(page_table, *([cache] * pp))
    return means.reshape(nb, n_pages * nper, 2 * NSA_KV * dh)


def _nsa_score_kernel(q_ref, kvc_ref, ocmp_ref, ids_ref, *, DH, qpos):
    G, R = NSA_KV, NSA_R
    gw = G * DH
    ncb = kvc_ref.shape[0]
    ratio = SEL_BLOCK // CMP_BLOCK
    nsb = ncb // ratio
    slopes = _alibi_slopes()
    q = q_ref[...]
    kvc = _bf(kvc_ref[...])
    rows = q.shape[0]
    cb = lax.broadcasted_iota(jnp.int32, (1, ncb), 1)
    cvalid = (cb * CMP_BLOCK + (CMP_BLOCK - 1)) <= qpos
    cdist = float(qpos) - (cb.astype(F32) * CMP_BLOCK + (CMP_BLOCK - 1) / 2.0)
    pi = lax.broadcasted_iota(jnp.int32, (ncb, nsb), 0)
    pj = lax.broadcasted_iota(jnp.int32, (ncb, nsb), 1) * ratio
    pool = jnp.where((pi >= pj) & (pi < pj + ratio), 1.0, 0.0).astype(BF16)
    lane = lax.broadcasted_iota(jnp.int32, (rows, nsb), 1).astype(F32)
    lane_out = lax.broadcasted_iota(jnp.int32, (rows, LANE), 1)
    o_cmp = []
    id_rows = []
    for g in range(G):
        kc = kvc[:, g * DH:(g + 1) * DH]
        vc = kvc[:, gw + g * DH:gw + (g + 1) * DH]
        imp = jnp.zeros((rows, nsb), F32)
        for r in range(R):
            hd = g * R + r
            qh = _bf(q[:, hd * DH:(hd + 1) * DH] * (DH ** -0.5))
            s = jnp.where(cvalid, _dot_nt(qh, kc) - slopes[hd] * cdist, NEG)
            p = jnp.where(cvalid, _softmax_rows(s), 0.0)
            o_cmp.append(_dot(_bf(p), vc))
            ph, plo = _split2(p)
            imp = imp + _dot(ph, pool) + _dot(plo, pool)
        val = jnp.where(lane == 0.0, -1.0, imp)
        ids = jnp.zeros((rows, LANE), F32)
        for it in range(SEL_TOPK - 2):
            mx = jnp.max(val, axis=1, keepdims=True)
            ix = jnp.min(jnp.where(val == mx, lane, float(nsb)), axis=1, keepdims=True)
            ids = jnp.where(lane_out == it + 1, ix, ids)
            val = jnp.where(lane == ix, -2.0, val)
        id_rows.append(ids[0:1, :])
    ocmp_ref[...] = jnp.concatenate(o_cmp, axis=1)
    pad = jnp.zeros((ids_ref.shape[0] - G, LANE), F32)
    ids_ref[...] = jnp.concatenate(id_rows + [pad], axis=0).astype(jnp.int32)


def _nsa_score_call(z_s, q_idx, kvc, qpos):
    nb, ncb, w = kvc.shape
    dh = w // (2 * NSA_KV)
    qw = NSA_HEADS * dh
    return pl.pallas_call(
        functools.partial(_nsa_score_kernel, DH=dh, qpos=qpos), grid=(nb,),
        in_specs=[pl.BlockSpec((SROWS, qw), lambda b: (b, q_idx)),
                  pl.BlockSpec((None, ncb, w), lambda b: (b, 0, 0))],
        out_specs=[pl.BlockSpec((SROWS, qw), lambda b: (b, 0)),
                   pl.BlockSpec((None, SUBLANE, LANE), lambda b: (b, 0, 0))],
        out_shape=[jax.ShapeDtypeStruct((nb * SROWS, qw), F32),
                   jax.ShapeDtypeStruct((nb, SUBLANE, LANE), jnp.int32)],
        compiler_params=_cparams(("arbitrary",)), name="nsa_cmp_score",
    )(z_s, kvc)


def _nsa_gather_kernel(sel_ref, q_ref, new_ref, small_ref, ocmp_ref, win_ref, *rest,
                       DH, NSEL, qpos, g_lane):
    kv = rest[:NSA_KV * NSEL]
    o_ref = rest[NSA_KV * NSEL]
    b = pl.program_id(0)
    G, R = NSA_KV, NSA_R
    gw = G * DH
    slopes = _alibi_slopes()
    q = q_ref[...]
    new = new_ref[...]
    gate = _sigmoid(small_ref[...])
    ocmp = ocmp_ref[...]
    wlen = win_ref.shape[0]
    jw = lax.broadcasted_iota(jnp.int32, (1, wlen), 1)
    distw = wlen - jw
    validw = (distw < WINDOW) & (qpos - distw >= 0)
    off = lax.broadcasted_iota(jnp.int32, (1, SEL_BLOCK), 1)
    outs = []
    for g in range(G):
        ksn = _bf(new[:, 2 * gw + g * DH:2 * gw + (g + 1) * DH]).astype(F32)
        vsn = _bf(new[:, 3 * gw + g * DH:3 * gw + (g + 1) * DH]).astype(F32)
        kwn = _bf(new[:, 4 * gw + g * DH:4 * gw + (g + 1) * DH]).astype(F32)
        vwn = _bf(new[:, 5 * gw + g * DH:5 * gw + (g + 1) * DH]).astype(F32)
        kw = _bf(win_ref[:, 0, g, :])
        vw = _bf(win_ref[:, 1, g, :])
        kblk = [_bf(kv[g * NSEL + i][:, 0, g, :]) for i in range(NSEL)]
        vblk = [_bf(kv[g * NSEL + i][:, 1, g, :]) for i in range(NSEL)]
        nslot = NSA_KV * SEL_TOPK
        dists = [qpos - (sel_ref[b, 2 * nslot + g * SEL_TOPK + i] * SEL_BLOCK + off) for i in range(NSEL)]
        for r in range(R):
            hd = g * R + r
            slope = slopes[hd]
            qh = _bf(q[:, hd * DH:(hd + 1) * DH] * (DH ** -0.5))
            qf = qh.astype(F32)
            s_list = [jnp.where(dists[i] >= 0, _dot_nt(qh, kblk[i]) - slope * dists[i].astype(F32), NEG)
                      for i in range(NSEL)]
            s_new = jnp.sum(qf * ksn, axis=1, keepdims=True)
            mx = s_new
            for s in s_list:
                mx = jnp.maximum(mx, jnp.max(s, axis=1, keepdims=True))
            p_new = jnp.exp(s_new - mx)
            den = p_new
            acc = p_new * vsn
            for i, s in enumerate(s_list):
                p = jnp.exp(s - mx)
                den = den + jnp.sum(p, axis=1, keepdims=True)
                acc = acc + _dot(_bf(p), vblk[i])
            o_slc = acc / den
            s = jnp.where(validw, _dot_nt(qh, kw) - slope * distw.astype(F32), NEG)
            s_new = jnp.sum(qf * kwn, axis=1, keepdims=True)
            mx = jnp.maximum(s_new, jnp.max(s, axis=1, keepdims=True))
            p = jnp.exp(s - mx)
            p_new = jnp.exp(s_new - mx)
            o_win = (_dot(_bf(p), vw) + p_new * vwn) / (jnp.sum(p, axis=1, keepdims=True) + p_new)
            gc, gs, gwn = [gate[:, g_lane + c * NSA_HEADS + hd:g_lane + c * NSA_HEADS + hd + 1] for c in range(3)]
            outs.append(gc * ocmp[:, hd * DH:(hd + 1) * DH] + gs * o_slc + gwn * o_win)
    o_ref[...] = jnp.concatenate(outs, axis=1)


def _nsa_gather_call(z_s, zsm_s, q_idx, kv_idx, ocmp, win, cache, page_table, ids, l, qpos, g_lane):
    nb = page_table.shape[0]
    dh = cache.shape[5]
    G = NSA_KV
    qw = NSA_HEADS * dh
    nsel = SEL_TOPK - 1
    bpp = PAGE_SIZE // SEL_BLOCK
    wlen = win.shape[2]
    nslot = G * SEL_TOPK
    sel = jnp.concatenate([jnp.take_along_axis(page_table, ids // bpp, axis=1), ids % bpp, ids], axis=1)
    in_specs = [pl.BlockSpec((SROWS, qw), lambda b, s: (b, q_idx)),
                pl.BlockSpec((SROWS, 6 * G * dh), lambda b, s: (b, kv_idx)),
                pl.BlockSpec((SROWS, LANE), lambda b, s: (b, 0)),
                pl.BlockSpec((SROWS, qw), lambda b, s: (b, 0)),
                pl.BlockSpec((None, None, wlen, 2, G, dh), lambda b, s: (l, b, 0, 0, 0, 0))]
    for g in range(G):
        for i in range(nsel):
            in_specs.append(pl.BlockSpec(
                (None, None, SEL_BLOCK, 2, G, dh),
                lambda b, s, j=g * SEL_TOPK + i: (l, s[b, j], s[b, nslot + j], 1, 0, 0)))
    gs = pltpu.PrefetchScalarGridSpec(
        num_scalar_prefetch=1, grid=(nb,), in_specs=in_specs,
        out_specs=pl.BlockSpec((SROWS, qw), lambda b, s: (b, 0)))
    kern = functools.partial(_nsa_gather_kernel, DH=dh, NSEL=nsel, qpos=qpos, g_lane=g_lane)
    return pl.pallas_call(
        kern, grid_spec=gs,
        out_shape=jax.ShapeDtypeStruct((nb * SROWS, qw), F32),
        compiler_params=_cparams(("arbitrary",)), name="nsa_sel_win",
    )(sel, z_s, z_s, zsm_s, ocmp, win, *([cache] * (G * nsel)))


def _split_cols_kernel(*refs, n_in, G, DH):
    z_refs = refs[:n_in]
    o_ref = refs[-1]
    per = z_refs[0].shape[1] // DH
    for idx in range(n_in * per):
        o_ref[:, idx // G, idx % G, :] = z_refs[idx // per][:, (idx % per) * DH:(idx % per + 1) * DH]


def _split_cols(z, col0, ncomp, prev, l, depth, nblk, rows, row_block, G, dh):
    width = ncomp * G * dh
    n_in = max(1, width // Z_TILE)
    wblk = width // n_in
    assert col0 % wblk == 0
    kern = functools.partial(_split_cols_kernel, n_in=n_in, G=G, DH=dh)
    in_specs = [pl.BlockSpec((rows, wblk), lambda j, k=k: (row_block(j), col0 // wblk + k)) for k in range(n_in)]
    args = [z] * n_in
    aliases = {}
    if prev is not None:
        in_specs.append(pl.BlockSpec(memory_space=pl.ANY))
        args.append(prev)
        aliases = {n_in: 0}
    return pl.pallas_call(
        kern, grid=(nblk,), in_specs=in_specs,
        out_specs=pl.BlockSpec((None, None, rows, ncomp, G, dh), lambda j: (l, j, 0, 0, 0, 0)),
        out_shape=jax.ShapeDtypeStruct((depth, nblk, rows, ncomp, G, dh), z.dtype),
        input_output_aliases=aliases,
        compiler_params=_cparams(("arbitrary",)), name="nsa_rows_out",
    )(*args)


def _lane_vec(depth, lane, values):
    n = values.shape[1]
    return jnp.zeros((depth, 1, LANE), F32).at[:, 0, lane:lane + n].set(values.astype(F32))


def kernel(x_prompt, x_sample, cache_nsa_kv, cache_nsa_win, state_gla, state_ssd, state_ssd_conv, state_mlstm_c, state_mlstm_n, state_mlstm_m, page_table, c_prompt, c_sample, ada_w, ada_b, norm_mix_g, norm_ffn_g, w_in, gla_w_a, gla_b_a, gla_norm_g, ssd_conv_w, ssd_conv_b, ssd_dt_bias, ssd_a_log, ssd_d, ssd_norm_g, ml_b_i, ml_b_f, ml_norm_g, w_branch, w_out, ffn_w_gate, ffn_w_up, ffn_w_down, final_norm_g):
    nbp, seq, d = x_prompt.shape
    nbs, dec_seq, _ = x_sample.shape
    assert dec_seq == 1 and seq % CHUNK == 0
    depth = w_in.shape[0]
    past_len = page_table.shape[1] * PAGE_SIZE
    bw = d // 4
    dh = bw // NSA_HEADS
    gw = NSA_KV * dh
    src, dst, n_packed, _ = _packed_offsets(d)
    kv_off = dst['nsa_kv']
    q_idx = dst['nsa_q'] // bw
    kv_idx = kv_off // (6 * gw)
    xbc_idx = dst['ssd_xbc'] // src['ssd_xbc'][1]
    zg_idx = dst['ssd_z'] // bw
    mlqk_idx = dst['ml_q'] // (2 * bw)
    mlvo_idx = dst['ml_v'] // (2 * bw)
    for name, width in (('gla_q', 3 * bw), ('nsa_q', bw), ('nsa_kv', 6 * gw), ('ssd_xbc', src['ssd_xbc'][1]),
                        ('ssd_z', bw), ('ml_q', 2 * bw), ('ml_v', 2 * bw), ('merge', Z_TILE)):
        assert dst[name] % width == 0
    assert dst['gla_q'] == 0
    g_lane, dt_lane, i_lane, f_lane = dst['nsa_g'], dst['ssd_dt'], dst['ml_i'], dst['ml_f']

    crow = 2 * SUBLANE
    c_all = jnp.concatenate([c_prompt, c_sample, jnp.zeros((crow - nbp - nbs, d), F32)], axis=0)
    mod = _ada(c_all, ada_w, ada_b)
    modp = mod[:, :nbp].reshape(depth, nbp, 1, 6 * d)
    mods = jnp.repeat(mod[:, nbp:nbp + nbs], SROWS, axis=1)

    w_mix, w_gate, w_small = _pack_w_in(w_in)
    n_mix = dst['merge']
    assert n_mix % (4 * LANE) == 0
    r3 = lambda a: a.reshape(depth, 1, a.shape[-1])
    wa_pad = jnp.zeros((depth, LANE, gla_w_a.shape[2]), F32).at[:, dst['gla_a']:dst['gla_a'] + GLA_RANK].set(gla_w_a)
    dtb = _lane_vec(depth, dt_lane, ssd_dt_bias)
    alog = _lane_vec(depth, dt_lane, ssd_a_log)
    dvec = jnp.repeat(ssd_d, bw // SSD_HEADS, axis=1).reshape(depth, 1, bw)
    bif = _lane_vec(depth, i_lane, ml_b_i) + _lane_vec(depth, f_lane, ml_b_f)

    xp = x_prompt.reshape(nbp * seq, d)
    xs = jnp.zeros((nbs, SROWS, d), F32).at[:, 0].set(x_sample[:, 0]).reshape(nbs * SROWS, d)

    zeros_like_p = lambda a: jnp.zeros((nbp,) + a.shape[2:], F32)
    gla0, ssd0, conv0, mlc0 = (zeros_like_p(a) for a in (state_gla, state_ssd, state_ssd_conv, state_mlstm_c))
    nm0 = jnp.zeros((nbp, SUBLANE, LANE), F32)
    nm_s = (jnp.zeros((depth, nbs, SUBLANE, LANE), F32).at[:, :, :ML_HEADS, :].set(state_mlstm_n)
            .at[:, :, ML_HEADS, :ML_HEADS].set(state_mlstm_m))

    out = {k: [] for k in ('rows_s', 'win_s', 'gla_p', 'gla_s', 'ssd_p', 'ssd_s', 'conv_p',
                           'conv_s', 'mlc_p', 'mlc_s', 'mln_p', 'mln_s', 'mlm_p', 'mlm_s')}
    rows_p = win_p = None
    rows_tile = 1024
    keep = min(WINDOW, seq)
    assert seq % keep == 0 and (nbp * seq) % rows_tile == 0
    for l in range(depth):
        hp = _norm_mod_prompt(xp, r3(norm_mix_g), modp, l, 0, 1, seq)
        hs = _norm_mod_sample(xs, r3(norm_mix_g), mods, l, 0, 1)
        zp, zs = _proj_nt(hp, hs, w_mix, l, 1024, n_mix // 4, "proj_in")
        zgp, zgs = _proj_nt(hp, hs, w_small, l, 2048, LANE, "proj_small")
        gtp, gts = _proj_nt(hp, hs, w_gate, l, 1024, d, "proj_gate", BF16)

        o_gla_p, gla_p = _gla_call(zp, zgp, wa_pad, r3(gla_b_a), r3(gla_norm_g), gla0, l, nbp, seq, CHUNK,
                                   CHUNK, BF16)
        o_gla_s, gla_s = _gla_call(zs, zgs, wa_pad, r3(gla_b_a), r3(gla_norm_g), state_gla[l], l, nbs, SROWS,
                                   SROWS, 1, F32)
        ssd_args = (ssd_conv_w, r3(ssd_conv_b), dtb, alog, dvec, r3(ssd_norm_g))
        o_ssd_p, ssd_p, conv_p = _ssd_call(zp, zgp, xbc_idx, zg_idx, *ssd_args, ssd0, conv0, l, nbp, seq,
                                           CHUNK, CHUNK, dt_lane, BF16)
        o_ssd_s, ssd_s, conv_s = _ssd_call(zs, zgs, xbc_idx, zg_idx, *ssd_args, state_ssd[l],
                                           state_ssd_conv[l], l, nbs, SROWS, SROWS, 1, dt_lane, F32)
        o_ml_p, mlc_p, mlnm_p = _mlstm_call(zp, zgp, mlqk_idx, mlvo_idx, bif, r3(ml_norm_g), mlc0, nm0, l,
                                            nbp, seq, CHUNK, CHUNK, i_lane, f_lane, BF16)
        o_ml_s, mlc_s, mlnm_s = _mlstm_call(zs, zgs, mlqk_idx, mlvo_idx, bif, r3(ml_norm_g), state_mlstm_c[l],
                                            nm_s[l], l, nbs, SROWS, SROWS, 1, i_lane, f_lane, F32)
        o_nsa_p = _nsa_prompt_call(zp, zgp, q_idx, kv_idx, nbp, seq, g_lane)
        kvc = _nsa_means_call(cache_nsa_kv, page_table, l)
        ocmp, ids = _nsa_score_call(zs, q_idx, kvc, past_len)
        ids2 = ids[:, :NSA_KV, :SEL_TOPK].reshape(nbs, NSA_KV * SEL_TOPK)
        o_nsa_s = _nsa_gather_call(zs, zgs, q_idx, kv_idx, ocmp, cache_nsa_win, cache_nsa_kv, page_table, ids2, l,
                                   past_len, g_lane)

        mp, ms = _merge((o_gla_p, o_nsa_p, o_ssd_p, o_ml_p), (o_gla_s, o_nsa_s, o_ssd_s, o_ml_s), gtp, gts, w_branch,
                        l, 1024, 512)
        xp, xs = _mm_resid(mp, ms, w_out, xp, xs, modp, mods, l, 2, seq, 1024, 512, "proj_out")
        hp = _norm_mod_prompt(xp, r3(norm_ffn_g), modp, l, 3, 4, seq)
        hs = _norm_mod_sample(xs, r3(norm_ffn_g), mods, l, 3, 4)
        ap, a_s = _mm_swiglu(hp, hs, ffn_w_gate, ffn_w_up, l, 1024, 512)
        xp, xs = _mm_resid(ap, a_s, ffn_w_down, xp, xs, modp, mods, l, 5, seq, 512, 512, "ffn_down")

        zs3 = zs.reshape(nbs, SROWS, n_mix)[:, :1]
        rows_p = _split_cols(zp, kv_off, 4, rows_p, l, depth, nbp * seq // rows_tile, rows_tile, lambda j: j,
                             NSA_KV, dh)
        win_p = _split_cols(zp, kv_off + 4 * gw, 2, win_p, l, depth, nbp, keep,
                            lambda j: (j + 1) * (seq // keep) - 1, NSA_KV, dh)
        out['rows_s'].append(zs3[:, :, kv_off:kv_off + 4 * gw].reshape(nbs, 1, 4, NSA_KV, dh))
        win_all = jnp.concatenate([cache_nsa_win[l], zs3[:, :, kv_off + 4 * gw:kv_off + 6 * gw]
                                   .reshape(nbs, 1, 2, NSA_KV, dh)], axis=1)
        out['win_s'].append(win_all[:, win_all.shape[1] - min(WINDOW, win_all.shape[1]):])
        for key, val in (('gla_p', gla_p), ('gla_s', gla_s), ('ssd_p', ssd_p), ('ssd_s', ssd_s), ('conv_p', conv_p),
                         ('conv_s', conv_s), ('mlc_p', mlc_p), ('mlc_s', mlc_s)):
            out[key].append(val)
        for tag, nm in (('p', mlnm_p), ('s', mlnm_s)):
            out['mln_' + tag].append(nm[:, :ML_HEADS, :])
            out['mlm_' + tag].append(nm[:, ML_HEADS, :ML_HEADS])

    y_prompt = _final_norm(xp, final_norm_g).reshape(nbp, seq, d)
    y_sample = _final_norm(xs, final_norm_g).reshape(nbs, SROWS, d)[:, :1]
    st = {k: jnp.stack(v) for k, v in out.items()}
    st['rows_p'] = rows_p.reshape(depth, nbp, seq, 4, NSA_KV, dh)
    st['win_p'] = win_p
    return (y_prompt, y_sample, st['rows_p'], st['rows_s'], st['win_p'], st['win_s'], st['gla_p'], st['gla_s'],
            st['ssd_p'], st['ssd_s'], st['conv_p'], st['conv_s'], st['mlc_p'], st['mlc_s'], st['mln_p'], st['mln_s'],
            st['mlm_p'], st['mlm_s'])
```

```python
import functools
import math

import jax
import jax.numpy as jnp
from jax import lax
from jax.experimental import pallas as pl
from jax.experimental.pallas import tpu as pltpu

F32 = jnp.float32
BF16 = jnp.bfloat16

PAGE_SIZE = 128
GLA_HEADS = 4
GLA_RANK = 16
GLA_TAU = 16.0
NSA_HEADS = 4
NSA_KV = 2
NSA_R = NSA_HEADS // NSA_KV
CMP_BLOCK = 32
SEL_BLOCK = 64
SEL_TOPK = 16
WINDOW = 512
FORCE_SCORE = 1e4
NEG = -1e30
SSD_HEADS = 8
SSD_GROUPS = 2
SSD_N = 128
SSD_CONV = 4
ML_HEADS = 4
CHUNK = 128
EPS = 1e-6

LANE = 128
SUBLANE = 8
VMEM_LIMIT = 56 * 1024 * 1024

SROWS = SUBLANE


def _cparams(sem):
    return pltpu.CompilerParams(dimension_semantics=sem, vmem_limit_bytes=VMEM_LIMIT)


def _layout(d_model):
    bw = d_model // 4
    gdk = (bw // GLA_HEADS) // 2
    conv_ch = bw + 2 * SSD_GROUPS * SSD_N
    splits = (('gla_q', GLA_HEADS * gdk), ('gla_k', GLA_HEADS * gdk), ('gla_v', bw), ('gla_r', bw),
              ('gla_a', GLA_RANK), ('nsa_q', bw), ('nsa_kv', 6 * NSA_KV * (bw // NSA_HEADS)),
              ('nsa_g', 3 * NSA_HEADS), ('ssd_z', bw), ('ssd_xbc', conv_ch), ('ssd_dt', SSD_HEADS),
              ('ml_q', bw), ('ml_k', bw), ('ml_v', bw), ('ml_o', bw), ('ml_i', ML_HEADS), ('ml_f', ML_HEADS),
              ('merge', 4 * d_model))
    src = {}
    off = 0
    for name, w in splits:
        src[name] = (off, w)
        off += w
    return src, off


_Z_ORDER = ('gla_q', 'gla_k', 'gla_v', 'gla_r', 'nsa_kv', 'ml_q', 'ml_k', 'ml_v', 'ml_o', 'ssd_xbc',
            'nsa_q', 'ssd_z', 'merge')
_Z_SMALL = ('gla_a', 'nsa_g', 'ssd_dt', 'ml_i', 'ml_f')


Z_TILE = 512
SMALL_TILE = LANE // 4


def _packed_offsets(d_model):
    src, d_in = _layout(d_model)
    dst = {}
    off = 0
    for name in _Z_ORDER:
        dst[name] = off
        off += src[name][1]
    starts = []
    for name in _Z_SMALL:
        o, w = src[name]
        if not starts or o + w > starts[-1] + SMALL_TILE:
            starts.append(o)
        dst[name] = (len(starts) - 1) * SMALL_TILE + o - starts[-1]
    assert len(starts) * SMALL_TILE <= LANE and starts[-1] + SMALL_TILE <= d_in
    dst['_small_starts'] = starts
    return src, dst, off, d_in


def _pack_rows_kernel(rows_ref, w_ref, o_ref):
    o_ref[...] = pltpu.einshape("nlk->lnk", w_ref[...]).astype(o_ref.dtype)


def _pack_rows(w_t, row_starts, tile):
    _, depth, d = w_t.shape
    nt = len(row_starts)
    gs = pltpu.PrefetchScalarGridSpec(
        num_scalar_prefetch=1, grid=(nt,),
        in_specs=[pl.BlockSpec((pl.Element(tile), pl.Element(depth), pl.Element(d)), lambda t, r: (r[t], 0, 0))],
        out_specs=pl.BlockSpec((depth, tile, d), lambda t, r: (0, t, 0)))
    return pl.pallas_call(
        _pack_rows_kernel, grid_spec=gs, out_shape=jax.ShapeDtypeStruct((depth, nt * tile, d), BF16),
        compiler_params=_cparams(("arbitrary",)), name="pack_w_in",
    )(jnp.asarray(row_starts, jnp.int32), w_t)


def _pack_w_in(w_in):
    depth, d_model, _ = w_in.shape
    src, dst, n_packed, d_in = _packed_offsets(d_model)
    assert w_in.shape[2] == d_in
    w_t = jnp.transpose(w_in, (2, 0, 1))
    tile = LANE

    def starts(names):
        out = []
        for n in names:
            assert src[n][1] % tile == 0
            out += [src[n][0] + i for i in range(0, src[n][1], tile)]
        return out

    mix = _pack_rows(w_t, starts(_Z_ORDER[:-1]), tile)
    gate = _pack_rows(w_t, starts(_Z_ORDER[-1:]), tile)
    small_starts = dst['_small_starts']
    small = _pack_rows(w_t, small_starts + [small_starts[-1]] * (LANE // SMALL_TILE - len(small_starts)), SMALL_TILE)
    return mix, gate, small


def _sigmoid(x):
    return 1.0 / (1.0 + jnp.exp(-x))


def _silu(x):
    return x * _sigmoid(x)


def _log_sigmoid(x):
    return jnp.minimum(x, 0.0) - jnp.log1p(jnp.exp(-jnp.abs(x)))


def _softplus(x):
    return jnp.maximum(x, 0.0) + jnp.log1p(jnp.exp(-jnp.abs(x)))


def _dot(a, b):
    return jnp.dot(a, b, preferred_element_type=F32)


def _dot_nt(a, b):
    return lax.dot_general(a, b, (((1,), (1,)), ((), ())), preferred_element_type=F32)


def _dot_tn(a, b):
    return lax.dot_general(a, b, (((0,), (0,)), ((), ())), preferred_element_type=F32)


def _bf(x):
    return x.astype(BF16)


def _split2(x):
    h = x.astype(BF16)
    return h, (x - h.astype(F32)).astype(BF16)


def _causal(c):
    r = lax.broadcasted_iota(jnp.int32, (c, c), 0)
    s = lax.broadcasted_iota(jnp.int32, (c, c), 1)
    return r >= s


def _cumsum_rows(x):
    n = x.shape[0]
    row = lax.broadcasted_iota(jnp.int32, (n, 1), 0)
    step = 1
    while step < n:
        x = x + jnp.where(row >= step, pltpu.roll(x, step, 0), 0.0)
        step *= 2
    return x


def _rms(x, g):
    return x * lax.rsqrt(jnp.mean(x * x, axis=-1, keepdims=True) + EPS) * g


def _rms_mxu(x, g):
    n = x.shape[-1]
    assert n == LANE
    hi, lo = _split2(x * x)
    avg = jnp.full((n, LANE), 1.0 / n, BF16)
    return x * lax.rsqrt(_dot(hi, avg) + _dot(lo, avg) + EPS) * g


def _softmax_rows(s):
    m = jnp.max(s, axis=-1, keepdims=True)
    e = jnp.exp(s - m)
    return e / jnp.sum(e, axis=-1, keepdims=True)


def _alibi_slopes():
    return [2.0 ** (-8.0 * (h + 1) / NSA_HEADS) for h in range(NSA_HEADS)]


def _ada_kernel(c_ref, w_ref, b_ref, o_ref):
    a = _bf(_silu(c_ref[...]))
    o_ref[...] = _dot(a, _bf(w_ref[...])) + b_ref[...]


def _ada(c_all, ada_w, ada_b):
    depth, d, n = ada_w.shape
    rows = c_all.shape[0]
    tn = 1024
    return pl.pallas_call(
        _ada_kernel, grid=(depth, n // tn),
        in_specs=[pl.BlockSpec((rows, d), lambda l, j: (0, 0)),
                  pl.BlockSpec((None, d, tn), lambda l, j: (l, 0, j)),
                  pl.BlockSpec((None, 1, tn), lambda l, j: (l, 0, j))],
        out_specs=pl.BlockSpec((None, rows, tn), lambda l, j: (l, 0, j)),
        out_shape=jax.ShapeDtypeStruct((depth, rows, n), F32),
        compiler_params=_cparams(("arbitrary", "arbitrary")), name="ada",
    )(c_all, ada_w, ada_b.reshape(depth, 1, n))


def _norm_mod_kernel(x_ref, g_ref, sc_ref, sh_ref, o_ref):
    y = _rms(x_ref[...], g_ref[...])
    o_ref[...] = (y * (1.0 + sc_ref[...]) + sh_ref[...]).astype(o_ref.dtype)


def _norm_kernel(x_ref, g_ref, o_ref):
    o_ref[...] = _rms(x_ref[...], g_ref[...]).astype(o_ref.dtype)


def _norm_mod_prompt(x, g, modp, l, j_shift, j_scale, seq):
    m, d = x.shape
    tm = 1024
    per = seq // tm
    return pl.pallas_call(
        _norm_mod_kernel, grid=(m // tm,),
        in_specs=[pl.BlockSpec((tm, d), lambda i: (i, 0)),
                  pl.BlockSpec((None, 1, d), lambda i: (l, 0, 0)),
                  pl.BlockSpec((None, None, 1, d), lambda i: (l, i // per, 0, j_scale)),
                  pl.BlockSpec((None, None, 1, d), lambda i: (l, i // per, 0, j_shift))],
        out_specs=pl.BlockSpec((tm, d), lambda i: (i, 0)),
        out_shape=jax.ShapeDtypeStruct((m, d), BF16),
        compiler_params=_cparams(("arbitrary",)), name="norm_mod_prompt",
    )(x, g, modp, modp)


def _norm_mod_sample(x, g, mods, l, j_shift, j_scale):
    m, d = x.shape
    return pl.pallas_call(
        _norm_mod_kernel, grid=(1,),
        in_specs=[pl.BlockSpec((m, d), lambda i: (0, 0)),
                  pl.BlockSpec((None, 1, d), lambda i: (l, 0, 0)),
                  pl.BlockSpec((None, m, d), lambda i: (l, 0, j_scale)),
                  pl.BlockSpec((None, m, d), lambda i: (l, 0, j_shift))],
        out_specs=pl.BlockSpec((m, d), lambda i: (0, 0)),
        out_shape=jax.ShapeDtypeStruct((m, d), BF16),
        compiler_params=_cparams(("arbitrary",)), name="norm_mod_sample",
    )(x, g, mods, mods)


def _final_norm(x, g):
    m, d = x.shape
    tm = min(m, 512)
    return pl.pallas_call(
        _norm_kernel, grid=(m // tm,),
        in_specs=[pl.BlockSpec((tm, d), lambda i: (i, 0)), pl.BlockSpec((1, d), lambda i: (0, 0))],
        out_specs=pl.BlockSpec((tm, d), lambda i: (i, 0)),
        out_shape=jax.ShapeDtypeStruct((m, d), F32),
        compiler_params=_cparams(("arbitrary",)), name="final_norm",
    )(x, g.reshape(1, d))


def _proj_nt_kernel(a_ref, as_ref, wt_ref, o_ref, os_ref):
    w = wt_ref[...]

    @pl.when(pl.program_id(1) == 0)
    def _():
        os_ref[...] = _dot_nt(as_ref[...], w).astype(os_ref.dtype)

    o_ref[...] = _dot_nt(a_ref[...], w).astype(o_ref.dtype)


def _proj_nt(a, a_s, w_t, l, tm, tn, name, out_dtype=F32):
    m, k = a.shape
    ms = a_s.shape[0]
    n = w_t.shape[1]
    return pl.pallas_call(
        _proj_nt_kernel, grid=(n // tn, m // tm),
        in_specs=[pl.BlockSpec((tm, k), lambda j, i: (i, 0)),
                  pl.BlockSpec((ms, k), lambda j, i: (0, 0)),
                  pl.BlockSpec((None, tn, k), lambda j, i: (l, j, 0))],
        out_specs=[pl.BlockSpec((tm, tn), lambda j, i: (i, j)),
                   pl.BlockSpec((ms, tn), lambda j, i: (0, j))],
        out_shape=[jax.ShapeDtypeStruct((m, n), out_dtype), jax.ShapeDtypeStruct((ms, n), out_dtype)],
        compiler_params=_cparams(("arbitrary", "arbitrary")), name=name,
    )(a, a_s, w_t)


def _mm_resid_kernel(a_ref, as_ref, w_ref, x_ref, xs_ref, gt_ref, gts_ref, o_ref, os_ref, wb_ref):
    @pl.when(pl.program_id(1) == 0)
    def _():
        wb_ref[...] = _bf(w_ref[...])
        os_ref[...] = xs_ref[...] + gts_ref[...] * _dot(as_ref[...], wb_ref[...])

    o_ref[...] = x_ref[...] + gt_ref[...] * _dot(a_ref[...], wb_ref[...])


def _mm_resid(a, a_s, w, x, x_s, modp, mods, l, j_gate, seq, tm, tn, name):
    m, k = a.shape
    ms = a_s.shape[0]
    n = w.shape[2]
    per = seq // tm
    nj = n // tn
    return pl.pallas_call(
        _mm_resid_kernel, grid=(nj, m // tm),
        in_specs=[pl.BlockSpec((tm, k), lambda j, i: (i, 0)),
                  pl.BlockSpec((ms, k), lambda j, i: (0, 0)),
                  pl.BlockSpec((None, k, tn), lambda j, i: (l, 0, j)),
                  pl.BlockSpec((tm, tn), lambda j, i: (i, j)),
                  pl.BlockSpec((ms, tn), lambda j, i: (0, j)),
                  pl.BlockSpec((None, None, 1, tn), lambda j, i: (l, i // per, 0, j_gate * nj + j)),
                  pl.BlockSpec((None, ms, tn), lambda j, i: (l, 0, j_gate * nj + j))],
        out_specs=[pl.BlockSpec((tm, tn), lambda j, i: (i, j)),
                   pl.BlockSpec((ms, tn), lambda j, i: (0, j))],
        out_shape=[jax.ShapeDtypeStruct((m, n), F32), jax.ShapeDtypeStruct((ms, n), F32)],
        scratch_shapes=[pltpu.VMEM((k, tn), BF16)],
        compiler_params=_cparams(("arbitrary", "arbitrary")), name=name,
    )(a, a_s, w, x, x_s, modp, mods)


def _mm_swiglu_kernel(a_ref, as_ref, wg_ref, wu_ref, o_ref, os_ref, wgb_ref, wub_ref):
    @pl.when(pl.program_id(1) == 0)
    def _():
        wgb_ref[...] = _bf(wg_ref[...])
        wub_ref[...] = _bf(wu_ref[...])
        a_s = as_ref[...]
        os_ref[...] = (_silu(_dot(a_s, wgb_ref[...])) * _dot(a_s, wub_ref[...])).astype(os_ref.dtype)

    a = a_ref[...]
    o_ref[...] = (_silu(_dot(a, wgb_ref[...])) * _dot(a, wub_ref[...])).astype(o_ref.dtype)


def _mm_swiglu(a, a_s, wg, wu, l, tm, tn):
    m, k = a.shape
    ms = a_s.shape[0]
    n = wg.shape[2]
    return pl.pallas_call(
        _mm_swiglu_kernel, grid=(n // tn, m // tm),
        in_specs=[pl.BlockSpec((tm, k), lambda j, i: (i, 0)),
                  pl.BlockSpec((ms, k), lambda j, i: (0, 0)),
                  pl.BlockSpec((None, k, tn), lambda j, i: (l, 0, j)),
                  pl.BlockSpec((None, k, tn), lambda j, i: (l, 0, j))],
        out_specs=[pl.BlockSpec((tm, tn), lambda j, i: (i, j)),
                   pl.BlockSpec((ms, tn), lambda j, i: (0, j))],
        out_shape=[jax.ShapeDtypeStruct((m, n), BF16), jax.ShapeDtypeStruct((ms, n), BF16)],
        scratch_shapes=[pltpu.VMEM((k, tn), BF16), pltpu.VMEM((k, tn), BF16)],
        compiler_params=_cparams(("arbitrary", "arbitrary")), name="ffn_gate_up",
    )(a, a_s, wg, wu)


def _merge_kernel(*refs):
    o_p = refs[0:4]
    o_s = refs[4:8]
    g_p = refs[8:12]
    g_s = refs[12:16]
    w_ref, out_ref, outs_ref, wb_ref = refs[16:20]

    @pl.when(pl.program_id(1) == 0)
    def _():
        wb_ref[...] = _bf(w_ref[...])
        acc = None
        for b in range(4):
            t = _sigmoid(g_s[b][...].astype(F32)) * _dot(_bf(o_s[b][...]), wb_ref[b])
            acc = t if acc is None else acc + t
        outs_ref[...] = acc.astype(outs_ref.dtype)

    acc = None
    for b in range(4):
        t = _sigmoid(g_p[b][...].astype(F32)) * _dot(_bf(o_p[b][...]), wb_ref[b])
        acc = t if acc is None else acc + t
    out_ref[...] = acc.astype(out_ref.dtype)


def _merge(o_p, o_s, z, z_s, w_branch, l, tm, tn):
    m = z.shape[0]
    ms = z_s.shape[0]
    bw = w_branch.shape[2]
    d = w_branch.shape[3]
    in_specs = ([pl.BlockSpec((tm, bw), lambda j, i: (i, 0)) for _ in range(4)]
                + [pl.BlockSpec((ms, bw), lambda j, i: (0, 0)) for _ in range(4)]
                + [pl.BlockSpec((tm, tn), lambda j, i, b=b: (i, b * (d // tn) + j)) for b in range(4)]
                + [pl.BlockSpec((ms, tn), lambda j, i, b=b: (0, b * (d // tn) + j)) for b in range(4)]
                + [pl.BlockSpec((None, 4, bw, tn), lambda j, i: (l, 0, 0, j))])
    return pl.pallas_call(
        _merge_kernel, grid=(d // tn, m // tm),
        in_specs=in_specs,
        out_specs=[pl.BlockSpec((tm, tn), lambda j, i: (i, j)), pl.BlockSpec((ms, tn), lambda j, i: (0, j))],
        out_shape=[jax.ShapeDtypeStruct((m, d), BF16), jax.ShapeDtypeStruct((ms, d), BF16)],
        scratch_shapes=[pltpu.VMEM((4, bw, tn), BF16)],
        compiler_params=_cparams(("arbitrary", "arbitrary")), name="merge",
    )(*o_p, *o_s, z, z, z, z, z_s, z_s, z_s, z_s, w_branch)


def _live_rows(c_rows, n_live):
    return lax.broadcasted_iota(jnp.int32, (c_rows, 1), 0) < n_live


def _gla_kernel(z_ref, small_ref, wa_ref, ba_ref, gn_ref, s0_ref, o_ref, sout_ref, st_scr, *, NB, C, L, H, DK, DV):
    c = pl.program_id(0)

    @pl.when(c == 0)
    def _():
        st_scr[...] = s0_ref[...]

    causal = _causal(C)
    wa = _bf(wa_ref[...])
    ba = ba_ref[...]
    gn = gn_ref[...]
    mid = C // 2

    def one_sequence(bi):
        z = z_ref[bi]
        q = z[:, 0:H * DK] * (DK ** -0.5)
        k = z[:, H * DK:2 * H * DK]
        v = z[:, 2 * H * DK:2 * H * DK + H * DV]
        r = z[:, 2 * H * DK + H * DV:]
        ga = _dot(_bf(small_ref[bi]), wa) + ba
        glog = _log_sigmoid(ga) * (1.0 / GLA_TAU)
        if L < C:
            live = _live_rows(C, L)
            glog = jnp.where(live, glog, 0.0)
            k = jnp.where(live, k, 0.0)
        b = _cumsum_rows(glog)
        bl = b[C - 1:C, :]
        bm = b[mid - 1:mid, :]
        qi = _bf(q * jnp.exp(b - bm))
        ki = _bf(k * jnp.exp(bm - b))
        qs = q * jnp.exp(b)
        kd = _bf(k * jnp.exp(bl - b))
        ebl_col = jnp.exp(b.T[:, C - 1:C])
        outs = []
        for h in range(H):
            ks = slice(h * DK, (h + 1) * DK)
            vs = slice(h * DV, (h + 1) * DV)
            vh = v[:, vs]
            a = jnp.where(causal, _dot_nt(qi[:, ks], ki[:, ks]), 0.0)
            st = st_scr[bi, h]
            o = _dot(_bf(jnp.concatenate([a, qs[:, ks]], axis=1)), _bf(jnp.concatenate([vh, st], axis=0)))
            st_scr[bi, h] = ebl_col[ks, :] * st + _dot_tn(kd[:, ks], _bf(vh))
            outs.append(_rms(o, gn[:, vs]))
        o_ref[bi] = (jnp.concatenate(outs, axis=1) * _silu(r)).astype(o_ref.dtype)

    for bi in range(NB):
        one_sequence(bi)

    @pl.when(c == pl.num_programs(0) - 1)
    def _():
        sout_ref[...] = st_scr[...]


def _gla_call(z, zsm, wa_pad, ba, gn, s0, l, nb, seq, C, L, out_dtype):
    H, DK, DV = s0.shape[1:]
    nch = seq // C
    width = 2 * H * DK + 2 * H * DV
    z3 = z.reshape(nb, seq, z.shape[1])
    zsm3 = zsm.reshape(nb, seq, LANE)
    kern = functools.partial(_gla_kernel, NB=nb, C=C, L=L, H=H, DK=DK, DV=DV)
    o, s = pl.pallas_call(
        kern, grid=(nch,),
        in_specs=[pl.BlockSpec((nb, C, width), lambda c: (0, c, 0)),
                  pl.BlockSpec((nb, C, LANE), lambda c: (0, c, 0)),
                  pl.BlockSpec((None, LANE, H * DK), lambda c: (l, 0, 0)),
                  pl.BlockSpec((None, 1, H * DK), lambda c: (l, 0, 0)),
                  pl.BlockSpec((None, 1, H * DV), lambda c: (l, 0, 0)),
                  pl.BlockSpec((nb, H, DK, DV), lambda c: (0, 0, 0, 0))],
        out_specs=[pl.BlockSpec((nb, C, H * DV), lambda c: (0, c, 0)),
                   pl.BlockSpec((nb, H, DK, DV), lambda c: (0, 0, 0, 0))],
        out_shape=[jax.ShapeDtypeStruct((nb, seq, H * DV), out_dtype),
                   jax.ShapeDtypeStruct((nb, H, DK, DV), F32)],
        scratch_shapes=[pltpu.VMEM((nb, H, DK, DV), F32)],
        compiler_params=_cparams(("arbitrary",)), name="gla",
    )(z3, zsm3, wa_pad, ba, gn, s0)
    return o.reshape(nb * seq, H * DV), s


def _ssd_kernel(xbc_ref, zg_ref, small_ref, cw_ref, cb_ref, dtb_ref, alog_ref, dvec_ref, gn_ref, s0_ref, conv0_ref,
                o_ref, sout_ref, convout_ref, s_scr, ext_scr, *, NB, C, L, H, G, N, P, dt_lane):
    c = pl.program_id(0)
    kh = SSD_CONV - 1
    base = SUBLANE - kh

    @pl.when(c == 0)
    def _():
        s_scr[...] = s0_ref[...]
        ext_scr[:, base:SUBLANE, :] = conv0_ref[...]

    cw = cw_ref[...]
    cb = cb_ref[...]
    dtb = dtb_ref[...]
    neg_a = -jnp.exp(alog_ref[...])
    dvec = dvec_ref[...]
    gn = gn_ref[...]
    causal = _causal(C)
    bw = H * P
    hpg = H // G

    def one_sequence(bi):
        ext_scr[bi, SUBLANE:SUBLANE + C, :] = xbc_ref[bi]
        conv = cb
        for j in range(SSD_CONV):
            conv = conv + ext_scr[bi, pl.ds(base + j, C), :] * cw[j:j + 1, :]
        hist = ext_scr[bi, pl.ds(base + L, kh), :]
        ext_scr[bi, base:SUBLANE, :] = hist
        xc = _silu(conv)
        sx = xc[:, 0:bw]
        bm = xc[:, bw:bw + G * N]
        cm = xc[:, bw + G * N:]
        dt = _softplus(small_ref[bi] + dtb)
        logg = dt * neg_a
        if L < C:
            live = _live_rows(C, L)
            logg = jnp.where(live, logg, 0.0)
            bm = jnp.where(live, bm, 0.0)
        b = _cumsum_rows(logg)
        b_t = b.T
        ys = []
        for g in range(G):
            cg = cm[:, g * N:(g + 1) * N]
            bg = bm[:, g * N:(g + 1) * N]
            gmat = _dot_nt(_bf(cg), _bf(bg))
            for hh in range(hpg):
                h = g * hpg + hh
                ln = dt_lane + h
                bc = b[:, ln:ln + 1]
                br = b_t[ln:ln + 1, :]
                dec = jnp.exp(jnp.where(causal, bc - br, -jnp.inf))
                xdt = sx[:, h * P:(h + 1) * P] * dt[:, ln:ln + 1]
                st = s_scr[bi, h]
                y = _dot(_bf(jnp.concatenate([gmat * dec, cg * jnp.exp(bc)], axis=1)),
                         _bf(jnp.concatenate([xdt, st], axis=0)))
                bl = bc[C - 1:C, :]
                s_scr[bi, h] = jnp.exp(bl) * st + _dot_tn(_bf(bg * jnp.exp(bl - bc)), _bf(xdt))
                ys.append(y)
        y_all = jnp.concatenate(ys, axis=1) + dvec * sx
        o_ref[bi] = _rms(y_all * _silu(zg_ref[bi]), gn).astype(o_ref.dtype)

    for bi in range(NB):
        one_sequence(bi)

    @pl.when(c == pl.num_programs(0) - 1)
    def _():
        sout_ref[...] = s_scr[...]
        convout_ref[...] = ext_scr[:, base:SUBLANE, :]


def _ssd_call(z, zsm, xbc_idx, zg_idx, cw, cb, dtb, alog, dvec, gn, s0, conv0, l, nb, seq, C, L, dt_lane,
              out_dtype):
    H, N, P = s0.shape[1:]
    G = SSD_GROUPS
    ch = conv0.shape[2]
    bw = H * P
    nch = seq // C
    z3 = z.reshape(nb, seq, z.shape[1])
    zsm3 = zsm.reshape(nb, seq, LANE)
    kern = functools.partial(_ssd_kernel, NB=nb, C=C, L=L, H=H, G=G, N=N, P=P, dt_lane=dt_lane)
    vec = lambda w: pl.BlockSpec((None, 1, w), lambda c: (l, 0, 0))
    o, s, cv = pl.pallas_call(
        kern, grid=(nch,),
        in_specs=[pl.BlockSpec((nb, C, ch), lambda c: (0, c, xbc_idx)),
                  pl.BlockSpec((nb, C, bw), lambda c: (0, c, zg_idx)),
                  pl.BlockSpec((nb, C, LANE), lambda c: (0, c, 0)),
                  pl.BlockSpec((None, SSD_CONV, ch), lambda c: (l, 0, 0)),
                  vec(ch), vec(LANE), vec(LANE), vec(bw), vec(bw),
                  pl.BlockSpec((nb, H, N, P), lambda c: (0, 0, 0, 0)),
                  pl.BlockSpec((nb, SSD_CONV - 1, ch), lambda c: (0, 0, 0))],
        out_specs=[pl.BlockSpec((nb, C, bw), lambda c: (0, c, 0)),
                   pl.BlockSpec((nb, H, N, P), lambda c: (0, 0, 0, 0)),
                   pl.BlockSpec((nb, SSD_CONV - 1, ch), lambda c: (0, 0, 0))],
        out_shape=[jax.ShapeDtypeStruct((nb, seq, bw), out_dtype),
                   jax.ShapeDtypeStruct((nb, H, N, P), F32),
                   jax.ShapeDtypeStruct((nb, SSD_CONV - 1, ch), F32)],
        scratch_shapes=[pltpu.VMEM((nb, H, N, P), F32), pltpu.VMEM((nb, SUBLANE + C, ch), F32)],
        compiler_params=_cparams(("arbitrary",)), name="ssd",
    )(z3, z3, zsm3, cw, cb, dtb, alog, dvec, gn, s0, conv0)
    return o.reshape(nb * seq, bw), s, cv


def _mlstm_kernel(qk_ref, vo_ref, small_ref, bif_ref, gn_ref, c0_ref, nm0_ref, o_ref, cout_ref, nmout_ref,
                  c_scr, nm_scr, *, NB, C, L, H, DH, i_lane, f_lane):
    c = pl.program_id(0)

    @pl.when(c == 0)
    def _():
        c_scr[...] = c0_ref[...]
        nm_scr[...] = nm0_ref[...]

    causal_t = lax.broadcasted_iota(jnp.int32, (C, C), 0) <= lax.broadcasted_iota(jnp.int32, (C, C), 1)
    lane = lax.broadcasted_iota(jnp.int32, (1, LANE), 1)
    bif = bif_ref[...]
    gn = gn_ref[...]

    def one_sequence(bi):
        qk = qk_ref[bi]
        vo = vo_ref[bi]
        q = qk[:, 0:H * DH]
        k = qk[:, H * DH:] * (DH ** -0.5)
        v = vo[:, 0:H * DH]
        og = vo[:, H * DH:]
        pre = small_ref[bi] + bif
        logf = _log_sigmoid(pre)
        ipre = pre
        if L < C:
            live = _live_rows(C, L)
            logf = jnp.where(live, logf, 0.0)
            ipre = jnp.where(live, pre, NEG)
        b = _cumsum_rows(logf)
        b_t = b.T
        nm = nm_scr[bi]
        nm_bf = _bf(nm)
        m_row = nm[H:H + 1, :]
        outs = []
        for h in range(H):
            sl = slice(h * DH, (h + 1) * DH)
            br = b_t[f_lane + h:f_lane + h + 1, :]
            uc = ipre[:, i_lane + h:i_lane + h + 1] - b[:, f_lane + h:f_lane + h + 1]
            m_prev = nm[H:H + 1, h:h + 1]
            logd = jnp.where(causal_t, br + uc, -jnp.inf)
            gg = br + m_prev
            m = jnp.maximum(gg, jnp.max(logd, axis=0, keepdims=True))
            dm = jnp.exp(logd - m)
            wi = jnp.exp(gg - m)
            qf = q[:, sl]
            qh = _bf(qf)
            kh = k[:, sl]
            vh = _bf(v[:, sl])
            sc = _dot_nt(jnp.concatenate([_bf(kh), nm_bf], axis=0), qh)
            qk_t = sc[0:C, :] * dm
            cst = c_scr[bi, h]
            nq = sc[C + h:C + h + 1, :]
            den = jnp.sum(qk_t, axis=0, keepdims=True) + wi * nq
            rden = 1.0 / jnp.maximum(jnp.abs(den), jnp.exp(-m))
            rows = jnp.concatenate([wi, rden, jnp.zeros((SUBLANE - 2, C), F32)], axis=0)
            if C < LANE:
                rows = jnp.concatenate([rows, jnp.zeros((SUBLANE, LANE - C), F32)], axis=1)
            cols = rows.T[0:C, :]
            hid = (_dot_tn(_bf(qk_t), vh) + cols[:, 0:1] * _dot(qh, _bf(cst))) * cols[:, 1:2]
            m_new = m[:, C - 1:C]
            bl = br[:, C - 1:C]
            ws = jnp.exp(uc + (bl - m_new))
            wc = jnp.exp(bl + m_prev - m_new)
            nrow = nm[h:h + 1, :]
            wk = ws * kh
            c_scr[bi, h] = wc * cst + _dot_tn(_bf(wk), vh)
            nm_scr[bi, h:h + 1, :] = wc * nrow + jnp.sum(wk, axis=0, keepdims=True)
            m_row = jnp.where(lane == h, m_new, m_row)
            outs.append(_rms_mxu(hid, gn[:, sl]))
        nm_scr[bi, H:H + 1, :] = m_row
        o_ref[bi] = (jnp.concatenate(outs, axis=1) * _sigmoid(og)).astype(o_ref.dtype)

    for bi in range(NB):
        one_sequence(bi)

    @pl.when(c == pl.num_programs(0) - 1)
    def _():
        cout_ref[...] = c_scr[...]
        nmout_ref[...] = nm_scr[...]


def _mlstm_call(z, zsm, qk_idx, vo_idx, bif, gn, c0, nm0, l, nb, seq, C, L, i_lane, f_lane, out_dtype):
    H, DH = c0.shape[1:3]
    nch = seq // C
    z3 = z.reshape(nb, seq, z.shape[1])
    zsm3 = zsm.reshape(nb, seq, LANE)
    kern = functools.partial(_mlstm_kernel, NB=nb, C=C, L=L, H=H, DH=DH, i_lane=i_lane, f_lane=f_lane)
    o, cc, nm = pl.pallas_call(
        kern, grid=(nch,),
        in_specs=[pl.BlockSpec((nb, C, 2 * H * DH), lambda c: (0, c, qk_idx)),
                  pl.BlockSpec((nb, C, 2 * H * DH), lambda c: (0, c, vo_idx)),
                  pl.BlockSpec((nb, C, LANE), lambda c: (0, c, 0)),
                  pl.BlockSpec((None, 1, LANE), lambda c: (l, 0, 0)),
                  pl.BlockSpec((None, 1, H * DH), lambda c: (l, 0, 0)),
                  pl.BlockSpec((nb, H, DH, DH), lambda c: (0, 0, 0, 0)),
                  pl.BlockSpec((nb, SUBLANE, LANE), lambda c: (0, 0, 0))],
        out_specs=[pl.BlockSpec((nb, C, H * DH), lambda c: (0, c, 0)),
                   pl.BlockSpec((nb, H, DH, DH), lambda c: (0, 0, 0, 0)),
                   pl.BlockSpec((nb, SUBLANE, LANE), lambda c: (0, 0, 0))],
        out_shape=[jax.ShapeDtypeStruct((nb, seq, H * DH), out_dtype),
                   jax.ShapeDtypeStruct((nb, H, DH, DH), F32),
                   jax.ShapeDtypeStruct((nb, SUBLANE, LANE), F32)],
        scratch_shapes=[pltpu.VMEM((nb, H, DH, DH), F32), pltpu.VMEM((nb, SUBLANE, LANE), F32)],
        compiler_params=_cparams(("arbitrary",)), name="mlstm",
    )(z3, z3, zsm3, bif, gn, c0, nm0)
    return o.reshape(nb * seq, H * DH), cc, nm


def _nsa_prompt_kernel(q_ref, kv_ref, small_ref, o_ref, kvb_ref, cmp_ref, *, T, TQ, DH, g_lane):
    qi = pl.program_id(1)
    G, R = NSA_KV, NSA_R
    gw = G * DH
    ncb = T // CMP_BLOCK
    nsb = T // SEL_BLOCK
    ratio = SEL_BLOCK // CMP_BLOCK
    slopes = _alibi_slopes()

    @pl.when(qi == 0)
    def _():
        kvb_ref[...] = _bf(kv_ref[...])
        hi, lo = _split2(kv_ref[:, 0:2 * gw])
        jj = lax.broadcasted_iota(jnp.int32, (ncb, T), 0) * CMP_BLOCK
        tt = lax.broadcasted_iota(jnp.int32, (ncb, T), 1)
        pm = jnp.where((tt >= jj) & (tt < jj + CMP_BLOCK), 1.0 / CMP_BLOCK, 0.0).astype(BF16)
        cmp_ref[...] = _bf(_dot(pm, hi) + _dot(pm, lo))

    q = q_ref[...]
    gate = _sigmoid(small_ref[...])
    posl = qi * TQ + lax.broadcasted_iota(jnp.int32, (1, TQ), 1)
    cbs = lax.broadcasted_iota(jnp.int32, (ncb, 1), 0)
    cvalid = (cbs * CMP_BLOCK + (CMP_BLOCK - 1)) <= posl
    cdist = posl.astype(F32) - (cbs.astype(F32) * CMP_BLOCK + (CMP_BLOCK - 1) / 2.0)
    sbs = lax.broadcasted_iota(jnp.int32, (nsb, 1), 0)
    svalid = sbs * SEL_BLOCK <= posl
    forced = (sbs == 0) | ((sbs * SEL_BLOCK <= posl) & (posl < sbs * SEL_BLOCK + SEL_BLOCK))
    pi = lax.broadcasted_iota(jnp.int32, (nsb, ncb), 0) * ratio
    pj = lax.broadcasted_iota(jnp.int32, (nsb, ncb), 1)
    pool = jnp.where((pj >= pi) & (pj < pi + ratio), 1.0, 0.0).astype(BF16)
    brow = lax.broadcasted_iota(jnp.int32, (nsb, TQ), 0)
    row2 = lax.broadcasted_iota(jnp.int32, (R * TQ, 1), 0)
    rr = lax.broadcasted_iota(jnp.int32, (TQ, TQ), 0)
    cc = lax.broadcasted_iota(jnp.int32, (TQ, TQ), 1)
    lower = rr >= cc
    lower2 = jnp.concatenate([lower] * R, axis=0)
    col = lax.broadcasted_iota(jnp.int32, (1, TQ), 1)
    nfar = WINDOW // TQ

    def tile(carry, kt, q2, slope2, kcol, vcol, mask):
        m, l, acc = carry
        k0 = pl.multiple_of(kt * TQ, TQ)
        ks = kvb_ref[pl.ds(k0, TQ), kcol:kcol + DH]
        vs = kvb_ref[pl.ds(k0, TQ), vcol:vcol + DH]
        s = _dot_nt(q2, ks) + slope2 * (k0 + col).astype(F32)
        if mask is not None:
            s = jnp.where(mask, s, NEG)
        mn = jnp.maximum(m, jnp.max(s, axis=1, keepdims=True))
        p = jnp.exp(s - mn)
        alpha = jnp.exp(m - mn)
        return (mn, alpha * l + jnp.sum(p, axis=1, keepdims=True), alpha * acc + _dot(_bf(p), vs))

    def sel_mask(sel, kt):
        kpos = kt * TQ + col
        expand = _bf(jnp.where((kpos >= sbs * SEL_BLOCK) & (kpos < sbs * SEL_BLOCK + SEL_BLOCK), 1.0, 0.0))
        return _dot_tn(sel, expand) > 0.5

    init = (jnp.full((R * TQ, 1), NEG, F32), jnp.zeros((R * TQ, 1), F32), jnp.zeros((R * TQ, DH), F32))
    o_cmp, sels, q2s, slope2s, sel_carry, o_win = [], [], [], [], [], []
    for g in range(G):
        kc = cmp_ref[:, g * DH:(g + 1) * DH]
        vc = cmp_ref[:, gw + g * DH:gw + (g + 1) * DH]
        qs = []
        imp = jnp.zeros((nsb, TQ), F32)
        for r in range(R):
            hd = g * R + r
            qh = _bf(q[:, hd * DH:(hd + 1) * DH] * (DH ** -0.5))
            qs.append(qh)
            s = jnp.where(cvalid, _dot_nt(kc, qh) - slopes[hd] * cdist, NEG)
            e = jnp.exp(s - jnp.max(s, axis=0, keepdims=True))
            p = jnp.where(cvalid, e / jnp.sum(e, axis=0, keepdims=True), 0.0)
            o_cmp.append(_dot_tn(_bf(p), vc))
            ph, plo = _split2(p)
            imp = imp + _dot(pool, ph) + _dot(pool, plo)
        score = jnp.where(svalid, jnp.where(forced, FORCE_SCORE, imp), -1.0)
        ahead = jnp.zeros((nsb, TQ), jnp.int32)
        for j in range(nsb):
            sj = score[j:j + 1, :]
            ahead = ahead + jnp.where(sj > score, 1, jnp.where(sj == score, jnp.where(brow > j, 1, 0), 0))
        sel = _bf(jnp.where((ahead < SEL_TOPK) & (score >= 0.0), 1.0, 0.0))
        q2 = jnp.concatenate(qs, axis=0)
        slope2 = slopes[g * R]
        for r in range(1, R):
            slope2 = jnp.where(row2 >= r * TQ, slopes[g * R + r], slope2)
        sels.append(sel)
        q2s.append(q2)
        slope2s.append(slope2)
        sel_carry.append(tile(init, qi, q2, slope2, 2 * gw + g * DH, 3 * gw + g * DH,
                              jnp.concatenate([sel_mask(sel, qi) & lower] * R, axis=0)))
        kcol, vcol = 4 * gw + g * DH, 5 * gw + g * DH
        carry = tile(init, qi, q2, slope2, kcol, vcol, lower2)
        for back in range(1, nfar + 1):
            kt = jnp.maximum(qi - back, 0)
            live = qi >= back
            mask = jnp.logical_and(jnp.logical_not(lower2), live) if back == nfar else live
            carry = tile(carry, kt, q2, slope2, kcol, vcol, mask)
        o_win.append(carry[2] / carry[1])

    def sel_tiles(kt, carries):
        return tuple(tile(carries[g], kt, q2s[g], slope2s[g], 2 * gw + g * DH, 3 * gw + g * DH,
                          jnp.concatenate([sel_mask(sels[g], kt)] * R, axis=0)) for g in range(G))

    sel_carry = lax.fori_loop(0, qi, sel_tiles, tuple(sel_carry))
    outs = [None] * (G * R)
    for g in range(G):
        o_slc = sel_carry[g][2] / sel_carry[g][1]
        for r in range(R):
            hd = g * R + r
            gc = gate[:, g_lane + hd:g_lane + hd + 1]
            gs = gate[:, g_lane + NSA_HEADS + hd:g_lane + NSA_HEADS + hd + 1]
            gwn = gate[:, g_lane + 2 * NSA_HEADS + hd:g_lane + 2 * NSA_HEADS + hd + 1]
            outs[hd] = gc * o_cmp[hd] + gs * o_slc[r * TQ:(r + 1) * TQ] + gwn * o_win[g][r * TQ:(r + 1) * TQ]
    o_ref[...] = jnp.concatenate(outs, axis=1).astype(o_ref.dtype)


def _nsa_prompt_call(z, zsm, q_idx, kv_idx, nb, seq, g_lane, tq=256):
    dh = LANE
    kvw = 6 * NSA_KV * dh
    qw = NSA_HEADS * dh
    nq = seq // tq
    assert WINDOW % tq == 0 and seq % tq == 0
    kern = functools.partial(_nsa_prompt_kernel, T=seq, TQ=tq, DH=dh, g_lane=g_lane)
    return pl.pallas_call(
        kern, grid=(nb, nq),
        in_specs=[pl.BlockSpec((tq, qw), lambda b, i: (b * nq + i, q_idx)),
                  pl.BlockSpec((seq, kvw), lambda b, i: (b, kv_idx)),
                  pl.BlockSpec((tq, LANE), lambda b, i: (b * nq + i, 0))],
        out_specs=pl.BlockSpec((tq, qw), lambda b, i: (b * nq + i, 0)),
        out_shape=jax.ShapeDtypeStruct((nb * seq, qw), BF16),
        scratch_shapes=[pltpu.VMEM((seq, kvw), BF16), pltpu.VMEM((seq // CMP_BLOCK, 2 * NSA_KV * dh), BF16)],
        compiler_params=_cparams(("arbitrary", "arbitrary")), name="nsa_prompt",
    )(z, z, zsm)


def _nsa_means_kernel(pt_ref, *refs, PP, DH):
    ins = refs[:PP]
    o_ref = refs[PP]
    nper = PAGE_SIZE // CMP_BLOCK
    for i in range(PP):
        x = ins[i][...].reshape(nper, CMP_BLOCK, 2, NSA_KV, DH)
        o_ref[i * nper:(i + 1) * nper] = jnp.sum(x, axis=1) * (1.0 / CMP_BLOCK)


def _nsa_means_call(cache, page_table, l, pp=32):
    nb, n_pages = page_table.shape
    dh = cache.shape[5]
    nper = PAGE_SIZE // CMP_BLOCK
    assert cache.shape[2] == PAGE_SIZE and cache.shape[4] == NSA_KV and n_pages % pp == 0
    in_specs = [pl.BlockSpec((None, None, PAGE_SIZE, 2, NSA_KV, dh),
                             lambda b, s, pt, i=i: (l, pt[b, s * pp + i], 0, 0, 0, 0)) for i in range(pp)]
    gs = pltpu.PrefetchScalarGridSpec(
        num_scalar_prefetch=1, grid=(nb, n_pages // pp), in_specs=in_specs,
        out_specs=pl.BlockSpec((None, nper * pp, 2, NSA_KV, dh), lambda b, s, pt: (b, s, 0, 0, 0)))
    means = pl.pallas_call(
        functools.partial(_nsa_means_kernel, PP=pp, DH=dh), grid_spec=gs,
        out_shape=jax.ShapeDtypeStruct((nb, n_pages * nper, 2, NSA_KV, dh), F32),
        compiler_params=_cparams(("arbitrary", "arbitrary")), name="nsa_cmp_means",
    )(page_table, *([cache] * pp))
    return means.reshape(nb, n_pages * nper, 2 * NSA_KV * dh)


def _nsa_score_kernel(q_ref, kvc_ref, ocmp_ref, ids_ref, *, DH, qpos):
    G, R = NSA_KV, NSA_R
    gw = G * DH
    ncb = kvc_ref.shape[0]
    ratio = SEL_BLOCK // CMP_BLOCK
    nsb = ncb // ratio
    slopes = _alibi_slopes()
    q = q_ref[...]
    kvc = _bf(kvc_ref[...])
    rows = q.shape[0]
    cb = lax.broadcasted_iota(jnp.int32, (1, ncb), 1)
    cvalid = (cb * CMP_BLOCK + (CMP_BLOCK - 1)) <= qpos
    cdist = float(qpos) - (cb.astype(F32) * CMP_BLOCK + (CMP_BLOCK - 1) / 2.0)
    pi = lax.broadcasted_iota(jnp.int32, (ncb, nsb), 0)
    pj = lax.broadcasted_iota(jnp.int32, (ncb, nsb), 1) * ratio
    pool = jnp.where((pi >= pj) & (pi < pj + ratio), 1.0, 0.0).astype(BF16)
    lane = lax.broadcasted_iota(jnp.int32, (rows, nsb), 1).astype(F32)
    lane_out = lax.broadcasted_iota(jnp.int32, (rows, LANE), 1)
    rowi = lax.broadcasted_iota(jnp.int32, (rows, 1), 0)
    o_cmp = []
    id_rows = []
    for g in range(G):
        kc = kvc[:, g * DH:(g + 1) * DH]
        vc = kvc[:, gw + g * DH:gw + (g + 1) * DH]
        qg = jnp.concatenate([q[0:1, (g * R + r) * DH:(g * R + r + 1) * DH] for r in range(R)]
                             + [jnp.zeros((rows - R, DH), F32)], axis=0)
        slope = slopes[g * R]
        for r in range(1, R):
            slope = jnp.where(rowi == r, slopes[g * R + r], slope)
        s = jnp.where(cvalid, _dot_nt(_bf(qg * (DH ** -0.5)), kc) - slope * cdist, NEG)
        p = jnp.where(cvalid, _softmax_rows(s), 0.0)
        og = _dot(_bf(p), vc)
        psum = p[0:1, :]
        for r in range(R):
            o_cmp.append(og[r:r + 1, :])
            if r:
                psum = psum + p[r:r + 1, :]
        ph, plo = _split2(jnp.concatenate([psum, jnp.zeros((rows - 1, ncb), F32)], axis=0))
        imp = _dot(ph, pool) + _dot(plo, pool)
        val = jnp.where(lane == 0.0, -1.0, imp)
        ids = jnp.zeros((rows, LANE), F32)
        for it in range(SEL_TOPK - 2):
            mx = jnp.max(val, axis=1, keepdims=True)
            ix = jnp.min(jnp.where(val == mx, lane, float(nsb)), axis=1, keepdims=True)
            ids = jnp.where(lane_out == it + 1, ix, ids)
            val = jnp.where(lane == ix, -2.0, val)
        id_rows.append(ids[0:1, :])
    ocmp_ref[...] = jnp.concatenate([jnp.concatenate(o_cmp, axis=1),
                                     jnp.zeros((rows - 1, len(o_cmp) * DH), F32)], axis=0)
    pad = jnp.zeros((ids_ref.shape[0] - G, LANE), F32)
    ids_ref[...] = jnp.concatenate(id_rows + [pad], axis=0).astype(jnp.int32)


def _nsa_score_call(z_s, q_idx, kvc, qpos):
    nb, ncb, w = kvc.shape
    dh = w // (2 * NSA_KV)
    qw = NSA_HEADS * dh
    return pl.pallas_call(
        functools.partial(_nsa_score_kernel, DH=dh, qpos=qpos), grid=(nb,),
        in_specs=[pl.BlockSpec((SROWS, qw), lambda b: (b, q_idx)),
                  pl.BlockSpec((None, ncb, w), lambda b: (b, 0, 0))],
        out_specs=[pl.BlockSpec((SROWS, qw), lambda b: (b, 0)),
                   pl.BlockSpec((None, SUBLANE, LANE), lambda b: (b, 0, 0))],
        out_shape=[jax.ShapeDtypeStruct((nb * SROWS, qw), F32),
                   jax.ShapeDtypeStruct((nb, SUBLANE, LANE), jnp.int32)],
        compiler_params=_cparams(("arbitrary",)), name="nsa_cmp_score",
    )(z_s, kvc)


def _nsa_gather_kernel(sel_ref, q_ref, new_ref, small_ref, ocmp_ref, win_ref, *rest,
                       DH, NSEL, qpos, g_lane):
    kv = rest[:NSA_KV * NSEL]
    o_ref = rest[NSA_KV * NSEL]
    b = pl.program_id(0)
    G, R = NSA_KV, NSA_R
    gw = G * DH
    slopes = _alibi_slopes()
    q = q_ref[...]
    new = new_ref[...]
    gate = _sigmoid(small_ref[...])
    ocmp = ocmp_ref[...]
    wlen = win_ref.shape[0]
    jw = lax.broadcasted_iota(jnp.int32, (1, wlen), 1)
    distw = wlen - jw
    validw = (distw < WINDOW) & (qpos - distw >= 0)
    off = lax.broadcasted_iota(jnp.int32, (1, SEL_BLOCK), 1)
    rows = q.shape[0]
    rowi = lax.broadcasted_iota(jnp.int32, (rows, 1), 0)
    fill = jnp.zeros((rows - R, DH), F32)

    def per_head(vals):
        out = vals[0]
        for r in range(1, R):
            out = jnp.where(rowi == r, vals[r], out)
        return out

    outs = []
    for g in range(G):
        ksn = _bf(new[0:1, 2 * gw + g * DH:2 * gw + (g + 1) * DH]).astype(F32)
        vsn = _bf(new[0:1, 3 * gw + g * DH:3 * gw + (g + 1) * DH]).astype(F32)
        kwn = _bf(new[0:1, 4 * gw + g * DH:4 * gw + (g + 1) * DH]).astype(F32)
        vwn = _bf(new[0:1, 5 * gw + g * DH:5 * gw + (g + 1) * DH]).astype(F32)
        kw = _bf(win_ref[:, 0, g, :])
        vw = _bf(win_ref[:, 1, g, :])
        kblk = [_bf(kv[g * NSEL + i][:, 0, g, :]) for i in range(NSEL)]
        vblk = [_bf(kv[g * NSEL + i][:, 1, g, :]) for i in range(NSEL)]
        nslot = NSA_KV * SEL_TOPK
        dists = [qpos - (sel_ref[b, 2 * nslot + g * SEL_TOPK + i] * SEL_BLOCK + off) for i in range(NSEL)]
        heads = [g * R + r for r in range(R)]
        qh = _bf(jnp.concatenate([q[0:1, hd * DH:(hd + 1) * DH] for hd in heads] + [fill], axis=0) * (DH ** -0.5))
        qf = qh.astype(F32)
        slope = per_head([slopes[hd] for hd in heads])
        s_list = [jnp.where(dists[i] >= 0, _dot_nt(qh, kblk[i]) - slope * dists[i].astype(F32), NEG)
                  for i in range(NSEL)]
        s_new = jnp.sum(qf * ksn, axis=1, keepdims=True)
        mx = s_new
        for s in s_list:
            mx = jnp.maximum(mx, jnp.max(s, axis=1, keepdims=True))
        p_new = jnp.exp(s_new - mx)
        den = p_new
        acc = p_new * vsn
        for i, s in enumerate(s_list):
            p = jnp.exp(s - mx)
            den = den + jnp.sum(p, axis=1, keepdims=True)
            acc = acc + _dot(_bf(p), vblk[i])
        o_slc = acc / den
        s = jnp.where(validw, _dot_nt(qh, kw) - slope * distw.astype(F32), NEG)
        s_new = jnp.sum(qf * kwn, axis=1, keepdims=True)
        mx = jnp.maximum(s_new, jnp.max(s, axis=1, keepdims=True))
        p = jnp.exp(s - mx)
        p_new = jnp.exp(s_new - mx)
        o_win = (_dot(_bf(p), vw) + p_new * vwn) / (jnp.sum(p, axis=1, keepdims=True) + p_new)
        gc, gs, gwn = [per_head([gate[0:1, g_lane + c * NSA_HEADS + hd:g_lane + c * NSA_HEADS + hd + 1]
                                 for hd in heads]) for c in range(3)]
        o_c = jnp.concatenate([ocmp[0:1, hd * DH:(hd + 1) * DH] for hd in heads] + [fill], axis=0)
        res = gc * o_c + gs * o_slc + gwn * o_win
        outs += [res[r:r + 1, :] for r in range(R)]
    o_ref[...] = jnp.concatenate([jnp.concatenate(outs, axis=1), jnp.zeros((rows - 1, G * R * DH), F32)], axis=0)


def _nsa_gather_call(z_s, zsm_s, q_idx, kv_idx, ocmp, win, cache, page_table, ids, l, qpos, g_lane):
    nb = page_table.shape[0]
    dh = cache.shape[5]
    G = NSA_KV
    qw = NSA_HEADS * dh
    nsel = SEL_TOPK - 1
    bpp = PAGE_SIZE // SEL_BLOCK
    wlen = win.shape[2]
    nslot = G * SEL_TOPK
    sel = jnp.concatenate([jnp.take_along_axis(page_table, ids // bpp, axis=1), ids % bpp, ids], axis=1)
    in_specs = [pl.BlockSpec((SROWS, qw), lambda b, s: (b, q_idx)),
                pl.BlockSpec((SROWS, 6 * G * dh), lambda b, s: (b, kv_idx)),
                pl.BlockSpec((SROWS, LANE), lambda b, s: (b, 0)),
                pl.BlockSpec((SROWS, qw), lambda b, s: (b, 0)),
                pl.BlockSpec((None, None, wlen, 2, G, dh), lambda b, s: (l, b, 0, 0, 0, 0))]
    for g in range(G):
        for i in range(nsel):
            in_specs.append(pl.BlockSpec(
                (None, None, SEL_BLOCK, 2, G, dh),
                lambda b, s, j=g * SEL_TOPK + i: (l, s[b, j], s[b, nslot + j], 1, 0, 0)))
    gs = pltpu.PrefetchScalarGridSpec(
        num_scalar_prefetch=1, grid=(nb,), in_specs=in_specs,
        out_specs=pl.BlockSpec((SROWS, qw), lambda b, s: (b, 0)))
    kern = functools.partial(_nsa_gather_kernel, DH=dh, NSEL=nsel, qpos=qpos, g_lane=g_lane)
    return pl.pallas_call(
        kern, grid_spec=gs,
        out_shape=jax.ShapeDtypeStruct((nb * SROWS, qw), F32),
        compiler_params=_cparams(("arbitrary",)), name="nsa_sel_win",
    )(sel, z_s, z_s, zsm_s, ocmp, win, *([cache] * (G * nsel)))


def _split_cols_kernel(*refs, n_in, G, DH):
    z_refs = refs[:n_in]
    o_ref = refs[-1]
    per = z_refs[0].shape[1] // DH
    for idx in range(n_in * per):
        o_ref[:, idx // G, idx % G, :] = z_refs[idx // per][:, (idx % per) * DH:(idx % per + 1) * DH]


def _split_cols(z, col0, ncomp, prev, l, depth, nblk, rows, row_block, G, dh):
    width = ncomp * G * dh
    n_in = max(1, width // Z_TILE)
    wblk = width // n_in
    assert col0 % wblk == 0
    kern = functools.partial(_split_cols_kernel, n_in=n_in, G=G, DH=dh)
    in_specs = [pl.BlockSpec((rows, wblk), lambda j, k=k: (row_block(j), col0 // wblk + k)) for k in range(n_in)]
    args = [z] * n_in
    aliases = {}
    if prev is not None:
        in_specs.append(pl.BlockSpec(memory_space=pl.ANY))
        args.append(prev)
        aliases = {n_in: 0}
    return pl.pallas_call(
        kern, grid=(nblk,), in_specs=in_specs,
        out_specs=pl.BlockSpec((None, None, rows, ncomp, G, dh), lambda j: (l, j, 0, 0, 0, 0)),
        out_shape=jax.ShapeDtypeStruct((depth, nblk, rows, ncomp, G, dh), z.dtype),
        input_output_aliases=aliases,
        compiler_params=_cparams(("arbitrary",)), name="nsa_rows_out",
    )(*args)


def _lane_vec(depth, lane, values):
    n = values.shape[1]
    return jnp.zeros((depth, 1, LANE), F32).at[:, 0, lane:lane + n].set(values.astype(F32))


def kernel(x_prompt, x_sample, cache_nsa_kv, cache_nsa_win, state_gla, state_ssd, state_ssd_conv, state_mlstm_c, state_mlstm_n, state_mlstm_m, page_table, c_prompt, c_sample, ada_w, ada_b, norm_mix_g, norm_ffn_g, w_in, gla_w_a, gla_b_a, gla_norm_g, ssd_conv_w, ssd_conv_b, ssd_dt_bias, ssd_a_log, ssd_d, ssd_norm_g, ml_b_i, ml_b_f, ml_norm_g, w_branch, w_out, ffn_w_gate, ffn_w_up, ffn_w_down, final_norm_g):
    nbp, seq, d = x_prompt.shape
    nbs, dec_seq, _ = x_sample.shape
    assert dec_seq == 1 and seq % CHUNK == 0
    depth = w_in.shape[0]
    past_len = page_table.shape[1] * PAGE_SIZE
    bw = d // 4
    dh = bw // NSA_HEADS
    gw = NSA_KV * dh
    src, dst, n_packed, _ = _packed_offsets(d)
    kv_off = dst['nsa_kv']
    q_idx = dst['nsa_q'] // bw
    kv_idx = kv_off // (6 * gw)
    xbc_idx = dst['ssd_xbc'] // src['ssd_xbc'][1]
    zg_idx = dst['ssd_z'] // bw
    mlqk_idx = dst['ml_q'] // (2 * bw)
    mlvo_idx = dst['ml_v'] // (2 * bw)
    for name, width in (('gla_q', 3 * bw), ('nsa_q', bw), ('nsa_kv', 6 * gw), ('ssd_xbc', src['ssd_xbc'][1]),
                        ('ssd_z', bw), ('ml_q', 2 * bw), ('ml_v', 2 * bw), ('merge', Z_TILE)):
        assert dst[name] % width == 0
    assert dst['gla_q'] == 0
    g_lane, dt_lane, i_lane, f_lane = dst['nsa_g'], dst['ssd_dt'], dst['ml_i'], dst['ml_f']

    crow = 2 * SUBLANE
    c_all = jnp.concatenate([c_prompt, c_sample, jnp.zeros((crow - nbp - nbs, d), F32)], axis=0)
    mod = _ada(c_all, ada_w, ada_b)
    modp = mod[:, :nbp].reshape(depth, nbp, 1, 6 * d)
    mods = jnp.repeat(mod[:, nbp:nbp + nbs], SROWS, axis=1)

    w_mix, w_gate, w_small = _pack_w_in(w_in)
    n_mix = dst['merge']
    assert n_mix % (4 * LANE) == 0
    r3 = lambda a: a.reshape(depth, 1, a.shape[-1])
    wa_pad = jnp.zeros((depth, LANE, gla_w_a.shape[2]), F32).at[:, dst['gla_a']:dst['gla_a'] + GLA_RANK].set(gla_w_a)
    dtb = _lane_vec(depth, dt_lane, ssd_dt_bias)
    alog = _lane_vec(depth, dt_lane, ssd_a_log)
    dvec = jnp.repeat(ssd_d, bw // SSD_HEADS, axis=1).reshape(depth, 1, bw)
    bif = _lane_vec(depth, i_lane, ml_b_i) + _lane_vec(depth, f_lane, ml_b_f)

    xp = x_prompt.reshape(nbp * seq, d)
    xs = jnp.zeros((nbs, SROWS, d), F32).at[:, 0].set(x_sample[:, 0]).reshape(nbs * SROWS, d)

    zeros_like_p = lambda a: jnp.zeros((nbp,) + a.shape[2:], F32)
    gla0, ssd0, conv0, mlc0 = (zeros_like_p(a) for a in (state_gla, state_ssd, state_ssd_conv, state_mlstm_c))
    nm0 = jnp.zeros((nbp, SUBLANE, LANE), F32)
    nm_s = (jnp.zeros((depth, nbs, SUBLANE, LANE), F32).at[:, :, :ML_HEADS, :].set(state_mlstm_n)
            .at[:, :, ML_HEADS, :ML_HEADS].set(state_mlstm_m))

    out = {k: [] for k in ('rows_s', 'win_s', 'gla_p', 'gla_s', 'ssd_p', 'ssd_s', 'conv_p',
                           'conv_s', 'mlc_p', 'mlc_s', 'mln_p', 'mln_s', 'mlm_p', 'mlm_s')}
    rows_p = win_p = None
    rows_tile = 1024
    keep = min(WINDOW, seq)
    assert seq % keep == 0 and (nbp * seq) % rows_tile == 0
    for l in range(depth):
        hp = _norm_mod_prompt(xp, r3(norm_mix_g), modp, l, 0, 1, seq)
        hs = _norm_mod_sample(xs, r3(norm_mix_g), mods, l, 0, 1)
        zp, zs = _proj_nt(hp, hs, w_mix, l, 1024, n_mix // 4, "proj_in")
        zgp, zgs = _proj_nt(hp, hs, w_small, l, 2048, LANE, "proj_small")
        gtp, gts = _proj_nt(hp, hs, w_gate, l, 1024, d, "proj_gate", BF16)

        o_gla_p, gla_p = _gla_call(zp, zgp, wa_pad, r3(gla_b_a), r3(gla_norm_g), gla0, l, nbp, seq, CHUNK,
                                   CHUNK, BF16)
        o_gla_s, gla_s = _gla_call(zs, zgs, wa_pad, r3(gla_b_a), r3(gla_norm_g), state_gla[l], l, nbs, SROWS,
                                   SROWS, 1, F32)
        ssd_args = (ssd_conv_w, r3(ssd_conv_b), dtb, alog, dvec, r3(ssd_norm_g))
        o_ssd_p, ssd_p, conv_p = _ssd_call(zp, zgp, xbc_idx, zg_idx, *ssd_args, ssd0, conv0, l, nbp, seq,
                                           CHUNK, CHUNK, dt_lane, BF16)
        o_ssd_s, ssd_s, conv_s = _ssd_call(zs, zgs, xbc_idx, zg_idx, *ssd_args, state_ssd[l],
                                           state_ssd_conv[l], l, nbs, SROWS, SROWS, 1, dt_lane, F32)
        o_ml_p, mlc_p, mlnm_p = _mlstm_call(zp, zgp, mlqk_idx, mlvo_idx, bif, r3(ml_norm_g), mlc0, nm0, l,
                                            nbp, seq, CHUNK, CHUNK, i_lane, f_lane, BF16)
        o_ml_s, mlc_s, mlnm_s = _mlstm_call(zs, zgs, mlqk_idx, mlvo_idx, bif, r3(ml_norm_g), state_mlstm_c[l],
                                            nm_s[l], l, nbs, SROWS, SROWS, 1, i_lane, f_lane, F32)
        o_nsa_p = _nsa_prompt_call(zp, zgp, q_idx, kv_idx, nbp, seq, g_lane)
        kvc = _nsa_means_call(cache_nsa_kv, page_table, l)
        ocmp, ids = _nsa_score_call(zs, q_idx, kvc, past_len)
        ids2 = ids[:, :NSA_KV, :SEL_TOPK].reshape(nbs, NSA_KV * SEL_TOPK)
        o_nsa_s = _nsa_gather_call(zs, zgs, q_idx, kv_idx, ocmp, cache_nsa_win, cache_nsa_kv, page_table, ids2, l,
                                   past_len, g_lane)

        mp, ms = _merge((o_gla_p, o_nsa_p, o_ssd_p, o_ml_p), (o_gla_s, o_nsa_s, o_ssd_s, o_ml_s), gtp, gts, w_branch,
                        l, 1024, 512)
        xp, xs = _mm_resid(mp, ms, w_out, xp, xs, modp, mods, l, 2, seq, 1024, 512, "proj_out")
        hp = _norm_mod_prompt(xp, r3(norm_ffn_g), modp, l, 3, 4, seq)
        hs = _norm_mod_sample(xs, r3(norm_ffn_g), mods, l, 3, 4)
        ap, a_s = _mm_swiglu(hp, hs, ffn_w_gate, ffn_w_up, l, 1024, 512)
        xp, xs = _mm_resid(ap, a_s, ffn_w_down, xp, xs, modp, mods, l, 5, seq, 512, 512, "ffn_down")

        zs3 = zs.reshape(nbs, SROWS, n_mix)[:, :1]
        rows_p = _split_cols(zp, kv_off, 4, rows_p, l, depth, nbp * seq // rows_tile, rows_tile, lambda j: j,
                             NSA_KV, dh)
        win_p = _split_cols(zp, kv_off + 4 * gw, 2, win_p, l, depth, nbp, keep,
                            lambda j: (j + 1) * (seq // keep) - 1, NSA_KV, dh)
        out['rows_s'].append(zs3[:, :, kv_off:kv_off + 4 * gw].reshape(nbs, 1, 4, NSA_KV, dh))
        win_all = jnp.concatenate([cache_nsa_win[l], zs3[:, :, kv_off + 4 * gw:kv_off + 6 * gw]
                                   .reshape(nbs, 1, 2, NSA_KV, dh)], axis=1)
        out['win_s'].append(win_all[:, win_all.shape[1] - min(WINDOW, win_all.shape[1]):])
        for key, val in (('gla_p', gla_p), ('gla_s', gla_s), ('ssd_p', ssd_p), ('ssd_s', ssd_s), ('conv_p', conv_p),
                         ('conv_s', conv_s), ('mlc_p', mlc_p), ('mlc_s', mlc_s)):
            out[key].append(val)
        for tag, nm in (('p', mlnm_p), ('s', mlnm_s)):
            out['mln_' + tag].append(nm[:, :ML_HEADS, :])
            out['mlm_' + tag].append(nm[:, ML_HEADS, :ML_HEADS])

    y_prompt = _final_norm(xp, final_norm_g).reshape(nbp, seq, d)
    y_sample = _final_norm(xs, final_norm_g).reshape(nbs, SROWS, d)[:, :1]
    st = {k: jnp.stack(v) for k, v in out.items()}
    st['rows_p'] = rows_p.reshape(depth, nbp, seq, 4, NSA_KV, dh)
    st['win_p'] = win_p
    return (y_prompt, y_sample, st['rows_p'], st['rows_s'], st['win_p'], st['win_s'], st['gla_p'], st['gla_s'],
            st['ssd_p'], st['ssd_s'], st['conv_p'], st['conv_s'], st['mlc_p'], st['mlc_s'], st['mln_p'], st['mln_s'],
            st['mlm_p'], st['mlm_s'])
```

```python
import functools
import math

import jax
import jax.numpy as jnp
from jax import lax
from jax.experimental import pallas as pl
from jax.experimental.pallas import tpu as pltpu

F32 = jnp.float32
BF16 = jnp.bfloat16

PAGE_SIZE = 128
GLA_HEADS = 4
GLA_RANK = 16
GLA_TAU = 16.0
NSA_HEADS = 4
NSA_KV = 2
NSA_R = NSA_HEADS // NSA_KV
CMP_BLOCK = 32
SEL_BLOCK = 64
SEL_TOPK = 16
WINDOW = 512
FORCE_SCORE = 1e4
NEG = -1e30
SSD_HEADS = 8
SSD_GROUPS = 2
SSD_N = 128
SSD_CONV = 4
ML_HEADS = 4
CHUNK = 128
EPS = 1e-6

LANE = 128
SUBLANE = 8
VMEM_LIMIT = 56 * 1024 * 1024

SROWS = SUBLANE


def _cparams(sem):
    return pltpu.CompilerParams(dimension_semantics=sem, vmem_limit_bytes=VMEM_LIMIT)


def _layout(d_model):
    bw = d_model // 4
    gdk = (bw // GLA_HEADS) // 2
    conv_ch = bw + 2 * SSD_GROUPS * SSD_N
    splits = (('gla_q', GLA_HEADS * gdk), ('gla_k', GLA_HEADS * gdk), ('gla_v', bw), ('gla_r', bw),
              ('gla_a', GLA_RANK), ('nsa_q', bw), ('nsa_kv', 6 * NSA_KV * (bw // NSA_HEADS)),
              ('nsa_g', 3 * NSA_HEADS), ('ssd_z', bw), ('ssd_xbc', conv_ch), ('ssd_dt', SSD_HEADS),
              ('ml_q', bw), ('ml_k', bw), ('ml_v', bw), ('ml_o', bw), ('ml_i', ML_HEADS), ('ml_f', ML_HEADS),
              ('merge', 4 * d_model))
    src = {}
    off = 0
    for name, w in splits:
        src[name] = (off, w)
        off += w
    return src, off


_Z_ORDER = ('gla_q', 'gla_k', 'gla_v', 'gla_r', 'nsa_kv', 'ml_q', 'ml_k', 'ml_v', 'ml_o', 'ssd_xbc',
            'nsa_q', 'ssd_z', 'merge')
_Z_SMALL = ('gla_a', 'nsa_g', 'ssd_dt', 'ml_i', 'ml_f')


Z_TILE = 512
SMALL_TILE = LANE // 4


def _packed_offsets(d_model):
    src, d_in = _layout(d_model)
    dst = {}
    off = 0
    for name in _Z_ORDER:
        dst[name] = off
        off += src[name][1]
    starts = []
    for name in _Z_SMALL:
        o, w = src[name]
        if not starts or o + w > starts[-1] + SMALL_TILE:
            starts.append(o)
        dst[name] = (len(starts) - 1) * SMALL_TILE + o - starts[-1]
    assert len(starts) * SMALL_TILE <= LANE and starts[-1] + SMALL_TILE <= d_in
    dst['_small_starts'] = starts
    return src, dst, off, d_in


def _pack_rows_kernel(rows_ref, w_ref, o_ref):
    o_ref[...] = pltpu.einshape("nlk->lnk", w_ref[...]).astype(o_ref.dtype)


def _pack_rows(w_t, row_starts, tile):
    _, depth, d = w_t.shape
    nt = len(row_starts)
    gs = pltpu.PrefetchScalarGridSpec(
        num_scalar_prefetch=1, grid=(nt,),
        in_specs=[pl.BlockSpec((pl.Element(tile), pl.Element(depth), pl.Element(d)), lambda t, r: (r[t], 0, 0))],
        out_specs=pl.BlockSpec((depth, tile, d), lambda t, r: (0, t, 0)))
    return pl.pallas_call(
        _pack_rows_kernel, grid_spec=gs, out_shape=jax.ShapeDtypeStruct((depth, nt * tile, d), BF16),
        compiler_params=_cparams(("arbitrary",)), name="pack_w_in",
    )(jnp.asarray(row_starts, jnp.int32), w_t)


def _pack_w_in(w_in):
    depth, d_model, _ = w_in.shape
    src, dst, n_packed, d_in = _packed_offsets(d_model)
    assert w_in.shape[2] == d_in
    w_t = jnp.transpose(w_in, (2, 0, 1))
    tile = LANE

    def starts(names):
        out = []
        for n in names:
            assert src[n][1] % tile == 0
            out += [src[n][0] + i for i in range(0, src[n][1], tile)]
        return out

    mix = _pack_rows(w_t, starts(_Z_ORDER[:-1]), tile)
    gate = _pack_rows(w_t, starts(_Z_ORDER[-1:]), tile)
    small_starts = dst['_small_starts']
    small = _pack_rows(w_t, small_starts + [small_starts[-1]] * (LANE // SMALL_TILE - len(small_starts)), SMALL_TILE)
    return mix, gate, small


def _sigmoid(x):
    return 1.0 / (1.0 + jnp.exp(-x))


def _silu(x):
    return x * _sigmoid(x)


def _log_sigmoid(x):
    return jnp.minimum(x, 0.0) - jnp.log1p(jnp.exp(-jnp.abs(x)))


def _softplus(x):
    return jnp.maximum(x, 0.0) + jnp.log1p(jnp.exp(-jnp.abs(x)))


def _dot(a, b):
    return jnp.dot(a, b, preferred_element_type=F32)


def _dot_nt(a, b):
    return lax.dot_general(a, b, (((1,), (1,)), ((), ())), preferred_element_type=F32)


def _dot_tn(a, b):
    return lax.dot_general(a, b, (((0,), (0,)), ((), ())), preferred_element_type=F32)


def _bf(x):
    return x.astype(BF16)


def _split2(x):
    h = x.astype(BF16)
    return h, (x - h.astype(F32)).astype(BF16)


def _causal(c):
    r = lax.broadcasted_iota(jnp.int32, (c, c), 0)
    s = lax.broadcasted_iota(jnp.int32, (c, c), 1)
    return r >= s


def _cumsum_rows(x):
    n = x.shape[0]
    row = lax.broadcasted_iota(jnp.int32, (n, 1), 0)
    step = 1
    while step < n:
        x = x + jnp.where(row >= step, pltpu.roll(x, step, 0), 0.0)
        step *= 2
    return x


def _rms(x, g):
    return x * lax.rsqrt(jnp.mean(x * x, axis=-1, keepdims=True) + EPS) * g


def _rms_mxu(x, g):
    n = x.shape[-1]
    assert n == LANE
    hi, lo = _split2(x * x)
    avg = jnp.full((n, LANE), 1.0 / n, BF16)
    return x * lax.rsqrt(_dot(hi, avg) + _dot(lo, avg) + EPS) * g


def _softmax_rows(s):
    m = jnp.max(s, axis=-1, keepdims=True)
    e = jnp.exp(s - m)
    return e / jnp.sum(e, axis=-1, keepdims=True)


def _alibi_slopes():
    return [2.0 ** (-8.0 * (h + 1) / NSA_HEADS) for h in range(NSA_HEADS)]


def _ada_kernel(c_ref, w_ref, b_ref, o_ref):
    a = _bf(_silu(c_ref[...]))
    o_ref[...] = _dot(a, _bf(w_ref[...])) + b_ref[...]


def _ada(c_all, ada_w, ada_b):
    depth, d, n = ada_w.shape
    rows = c_all.shape[0]
    tn = 1024
    return pl.pallas_call(
        _ada_kernel, grid=(depth, n // tn),
        in_specs=[pl.BlockSpec((rows, d), lambda l, j: (0, 0)),
                  pl.BlockSpec((None, d, tn), lambda l, j: (l, 0, j)),
                  pl.BlockSpec((None, 1, tn), lambda l, j: (l, 0, j))],
        out_specs=pl.BlockSpec((None, rows, tn), lambda l, j: (l, 0, j)),
        out_shape=jax.ShapeDtypeStruct((depth, rows, n), F32),
        compiler_params=_cparams(("arbitrary", "arbitrary")), name="ada",
    )(c_all, ada_w, ada_b.reshape(depth, 1, n))


def _norm_mod_kernel(x_ref, g_ref, sc_ref, sh_ref, o_ref):
    y = _rms(x_ref[...], g_ref[...])
    o_ref[...] = (y * (1.0 + sc_ref[...]) + sh_ref[...]).astype(o_ref.dtype)


def _norm_kernel(x_ref, g_ref, o_ref):
    o_ref[...] = _rms(x_ref[...], g_ref[...]).astype(o_ref.dtype)


def _norm_mod_prompt(x, g, modp, l, j_shift, j_scale, seq):
    m, d = x.shape
    tm = 1024
    per = seq // tm
    return pl.pallas_call(
        _norm_mod_kernel, grid=(m // tm,),
        in_specs=[pl.BlockSpec((tm, d), lambda i: (i, 0)),
                  pl.BlockSpec((None, 1, d), lambda i: (l, 0, 0)),
                  pl.BlockSpec((None, None, 1, d), lambda i: (l, i // per, 0, j_scale)),
                  pl.BlockSpec((None, None, 1, d), lambda i: (l, i // per, 0, j_shift))],
        out_specs=pl.BlockSpec((tm, d), lambda i: (i, 0)),
        out_shape=jax.ShapeDtypeStruct((m, d), BF16),
        compiler_params=_cparams(("arbitrary",)), name="norm_mod_prompt",
    )(x, g, modp, modp)


def _norm_mod_sample(x, g, mods, l, j_shift, j_scale):
    m, d = x.shape
    return pl.pallas_call(
        _norm_mod_kernel, grid=(1,),
        in_specs=[pl.BlockSpec((m, d), lambda i: (0, 0)),
                  pl.BlockSpec((None, 1, d), lambda i: (l, 0, 0)),
                  pl.BlockSpec((None, m, d), lambda i: (l, 0, j_scale)),
                  pl.BlockSpec((None, m, d), lambda i: (l, 0, j_shift))],
        out_specs=pl.BlockSpec((m, d), lambda i: (0, 0)),
        out_shape=jax.ShapeDtypeStruct((m, d), BF16),
        compiler_params=_cparams(("arbitrary",)), name="norm_mod_sample",
    )(x, g, mods, mods)


def _final_norm(x, g):
    m, d = x.shape
    tm = min(m, 512)
    return pl.pallas_call(
        _norm_kernel, grid=(m // tm,),
        in_specs=[pl.BlockSpec((tm, d), lambda i: (i, 0)), pl.BlockSpec((1, d), lambda i: (0, 0))],
        out_specs=pl.BlockSpec((tm, d), lambda i: (i, 0)),
        out_shape=jax.ShapeDtypeStruct((m, d), F32),
        compiler_params=_cparams(("arbitrary",)), name="final_norm",
    )(x, g.reshape(1, d))


def _proj_nt_kernel(a_ref, as_ref, wt_ref, o_ref, os_ref):
    w = wt_ref[...]

    @pl.when(pl.program_id(1) == 0)
    def _():
        os_ref[...] = _dot_nt(as_ref[...], w).astype(os_ref.dtype)

    o_ref[...] = _dot_nt(a_ref[...], w).astype(o_ref.dtype)


def _proj_nt(a, a_s, w_t, l, tm, tn, name, out_dtype=F32):
    m, k = a.shape
    ms = a_s.shape[0]
    n = w_t.shape[1]
    return pl.pallas_call(
        _proj_nt_kernel, grid=(n // tn, m // tm),
        in_specs=[pl.BlockSpec((tm, k), lambda j, i: (i, 0)),
                  pl.BlockSpec((ms, k), lambda j, i: (0, 0)),
                  pl.BlockSpec((None, tn, k), lambda j, i: (l, j, 0))],
        out_specs=[pl.BlockSpec((tm, tn), lambda j, i: (i, j)),
                   pl.BlockSpec((ms, tn), lambda j, i: (0, j))],
        out_shape=[jax.ShapeDtypeStruct((m, n), out_dtype), jax.ShapeDtypeStruct((ms, n), out_dtype)],
        compiler_params=_cparams(("arbitrary", "arbitrary")), name=name,
    )(a, a_s, w_t)


def _mm_resid_kernel(a_ref, as_ref, w_ref, x_ref, xs_ref, gt_ref, gts_ref, o_ref, os_ref, wb_ref):
    @pl.when(pl.program_id(1) == 0)
    def _():
        wb_ref[...] = _bf(w_ref[...])
        os_ref[...] = xs_ref[...] + gts_ref[...] * _dot(as_ref[...], wb_ref[...])

    o_ref[...] = x_ref[...] + gt_ref[...] * _dot(a_ref[...], wb_ref[...])


def _mm_resid(a, a_s, w, x, x_s, modp, mods, l, j_gate, seq, tm, tn, name):
    m, k = a.shape
    ms = a_s.shape[0]
    n = w.shape[2]
    per = seq // tm
    nj = n // tn
    return pl.pallas_call(
        _mm_resid_kernel, grid=(nj, m // tm),
        in_specs=[pl.BlockSpec((tm, k), lambda j, i: (i, 0)),
                  pl.BlockSpec((ms, k), lambda j, i: (0, 0)),
                  pl.BlockSpec((None, k, tn), lambda j, i: (l, 0, j)),
                  pl.BlockSpec((tm, tn), lambda j, i: (i, j)),
                  pl.BlockSpec((ms, tn), lambda j, i: (0, j)),
                  pl.BlockSpec((None, None, 1, tn), lambda j, i: (l, i // per, 0, j_gate * nj + j)),
                  pl.BlockSpec((None, ms, tn), lambda j, i: (l, 0, j_gate * nj + j))],
        out_specs=[pl.BlockSpec((tm, tn), lambda j, i: (i, j)),
                   pl.BlockSpec((ms, tn), lambda j, i: (0, j))],
        out_shape=[jax.ShapeDtypeStruct((m, n), F32), jax.ShapeDtypeStruct((ms, n), F32)],
        scratch_shapes=[pltpu.VMEM((k, tn), BF16)],
        compiler_params=_cparams(("arbitrary", "arbitrary")), name=name,
    )(a, a_s, w, x, x_s, modp, mods)


def _mm_swiglu_kernel(a_ref, as_ref, wg_ref, wu_ref, o_ref, os_ref, wgb_ref, wub_ref):
    @pl.when(pl.program_id(1) == 0)
    def _():
        wgb_ref[...] = _bf(wg_ref[...])
        wub_ref[...] = _bf(wu_ref[...])
        a_s = as_ref[...]
        os_ref[...] = (_silu(_dot(a_s, wgb_ref[...])) * _dot(a_s, wub_ref[...])).astype(os_ref.dtype)

    a = a_ref[...]
    o_ref[...] = (_silu(_dot(a, wgb_ref[...])) * _dot(a, wub_ref[...])).astype(o_ref.dtype)


def _mm_swiglu(a, a_s, wg, wu, l, tm, tn):
    m, k = a.shape
    ms = a_s.shape[0]
    n = wg.shape[2]
    return pl.pallas_call(
        _mm_swiglu_kernel, grid=(n // tn, m // tm),
        in_specs=[pl.BlockSpec((tm, k), lambda j, i: (i, 0)),
                  pl.BlockSpec((ms, k), lambda j, i: (0, 0)),
                  pl.BlockSpec((None, k, tn), lambda j, i: (l, 0, j)),
                  pl.BlockSpec((None, k, tn), lambda j, i: (l, 0, j))],
        out_specs=[pl.BlockSpec((tm, tn), lambda j, i: (i, j)),
                   pl.BlockSpec((ms, tn), lambda j, i: (0, j))],
        out_shape=[jax.ShapeDtypeStruct((m, n), BF16), jax.ShapeDtypeStruct((ms, n), BF16)],
        scratch_shapes=[pltpu.VMEM((k, tn), BF16), pltpu.VMEM((k, tn), BF16)],
        compiler_params=_cparams(("arbitrary", "arbitrary")), name="ffn_gate_up",
    )(a, a_s, wg, wu)


def _merge_kernel(*refs):
    o_p = refs[0:4]
    o_s = refs[4:8]
    g_p = refs[8:12]
    g_s = refs[12:16]
    w_ref, out_ref, outs_ref, wb_ref = refs[16:20]

    @pl.when(pl.program_id(1) == 0)
    def _():
        wb_ref[...] = _bf(w_ref[...])
        acc = None
        for b in range(4):
            t = _sigmoid(g_s[b][...].astype(F32)) * _dot(_bf(o_s[b][...]), wb_ref[b])
            acc = t if acc is None else acc + t
        outs_ref[...] = acc.astype(outs_ref.dtype)

    acc = None
    for b in range(4):
        t = _sigmoid(g_p[b][...].astype(F32)) * _dot(_bf(o_p[b][...]), wb_ref[b])
        acc = t if acc is None else acc + t
    out_ref[...] = acc.astype(out_ref.dtype)


def _merge(o_p, o_s, z, z_s, w_branch, l, tm, tn):
    m = z.shape[0]
    ms = z_s.shape[0]
    bw = w_branch.shape[2]
    d = w_branch.shape[3]
    in_specs = ([pl.BlockSpec((tm, bw), lambda j, i: (i, 0)) for _ in range(4)]
                + [pl.BlockSpec((ms, bw), lambda j, i: (0, 0)) for _ in range(4)]
                + [pl.BlockSpec((tm, tn), lambda j, i, b=b: (i, b * (d // tn) + j)) for b in range(4)]
                + [pl.BlockSpec((ms, tn), lambda j, i, b=b: (0, b * (d // tn) + j)) for b in range(4)]
                + [pl.BlockSpec((None, 4, bw, tn), lambda j, i: (l, 0, 0, j))])
    return pl.pallas_call(
        _merge_kernel, grid=(d // tn, m // tm),
        in_specs=in_specs,
        out_specs=[pl.BlockSpec((tm, tn), lambda j, i: (i, j)), pl.BlockSpec((ms, tn), lambda j, i: (0, j))],
        out_shape=[jax.ShapeDtypeStruct((m, d), BF16), jax.ShapeDtypeStruct((ms, d), BF16)],
        scratch_shapes=[pltpu.VMEM((4, bw, tn), BF16)],
        compiler_params=_cparams(("arbitrary", "arbitrary")), name="merge",
    )(*o_p, *o_s, z, z, z, z, z_s, z_s, z_s, z_s, w_branch)


def _live_rows(c_rows, n_live):
    return lax.broadcasted_iota(jnp.int32, (c_rows, 1), 0) < n_live


def _gla_kernel(z_ref, small_ref, wa_ref, ba_ref, gn_ref, s0_ref, o_ref, sout_ref, st_scr, *, NB, C, L, H, DK, DV):
    c = pl.program_id(0)

    @pl.when(c == 0)
    def _():
        st_scr[...] = s0_ref[...]

    causal = _causal(C)
    wa = _bf(wa_ref[...])
    ba = ba_ref[...]
    gn = gn_ref[...]
    mid = C // 2

    def one_sequence(bi):
        z = z_ref[bi]
        q = z[:, 0:H * DK] * (DK ** -0.5)
        k = z[:, H * DK:2 * H * DK]
        v = z[:, 2 * H * DK:2 * H * DK + H * DV]
        r = z[:, 2 * H * DK + H * DV:]
        ga = _dot(_bf(small_ref[bi]), wa) + ba
        glog = _log_sigmoid(ga) * (1.0 / GLA_TAU)
        if L < C:
            live = _live_rows(C, L)
            glog = jnp.where(live, glog, 0.0)
            k = jnp.where(live, k, 0.0)
        b = _cumsum_rows(glog)
        bl = b[C - 1:C, :]
        bm = b[mid - 1:mid, :]
        qi = _bf(q * jnp.exp(b - bm))
        ki = _bf(k * jnp.exp(bm - b))
        qs = q * jnp.exp(b)
        kd = _bf(k * jnp.exp(bl - b))
        ebl_col = jnp.exp(b.T[:, C - 1:C])
        outs = []
        for h in range(H):
            ks = slice(h * DK, (h + 1) * DK)
            vs = slice(h * DV, (h + 1) * DV)
            vh = v[:, vs]
            a = jnp.where(causal, _dot_nt(qi[:, ks], ki[:, ks]), 0.0)
            st = st_scr[bi, h]
            o = _dot(_bf(jnp.concatenate([a, qs[:, ks]], axis=1)), _bf(jnp.concatenate([vh, st], axis=0)))
            st_scr[bi, h] = ebl_col[ks, :] * st + _dot_tn(kd[:, ks], _bf(vh))
            outs.append(_rms(o, gn[:, vs]))
        o_ref[bi] = (jnp.concatenate(outs, axis=1) * _silu(r)).astype(o_ref.dtype)

    for bi in range(NB):
        one_sequence(bi)

    @pl.when(c == pl.num_programs(0) - 1)
    def _():
        sout_ref[...] = st_scr[...]


def _gla_call(z, zsm, wa_pad, ba, gn, s0, l, nb, seq, C, L, out_dtype):
    H, DK, DV = s0.shape[1:]
    nch = seq // C
    width = 2 * H * DK + 2 * H * DV
    z3 = z.reshape(nb, seq, z.shape[1])
    zsm3 = zsm.reshape(nb, seq, LANE)
    kern = functools.partial(_gla_kernel, NB=nb, C=C, L=L, H=H, DK=DK, DV=DV)
    o, s = pl.pallas_call(
        kern, grid=(nch,),
        in_specs=[pl.BlockSpec((nb, C, width), lambda c: (0, c, 0)),
                  pl.BlockSpec((nb, C, LANE), lambda c: (0, c, 0)),
                  pl.BlockSpec((None, LANE, H * DK), lambda c: (l, 0, 0)),
                  pl.BlockSpec((None, 1, H * DK), lambda c: (l, 0, 0)),
                  pl.BlockSpec((None, 1, H * DV), lambda c: (l, 0, 0)),
                  pl.BlockSpec((nb, H, DK, DV), lambda c: (0, 0, 0, 0))],
        out_specs=[pl.BlockSpec((nb, C, H * DV), lambda c: (0, c, 0)),
                   pl.BlockSpec((nb, H, DK, DV), lambda c: (0, 0, 0, 0))],
        out_shape=[jax.ShapeDtypeStruct((nb, seq, H * DV), out_dtype),
                   jax.ShapeDtypeStruct((nb, H, DK, DV), F32)],
        scratch_shapes=[pltpu.VMEM((nb, H, DK, DV), F32)],
        compiler_params=_cparams(("arbitrary",)), name="gla",
    )(z3, zsm3, wa_pad, ba, gn, s0)
    return o.reshape(nb * seq, H * DV), s


def _ssd_kernel(xbc_ref, zg_ref, small_ref, cw_ref, cb_ref, dtb_ref, alog_ref, dvec_ref, gn_ref, s0_ref, conv0_ref,
                o_ref, sout_ref, convout_ref, s_scr, ext_scr, *, NB, C, L, H, G, N, P, dt_lane):
    c = pl.program_id(0)
    kh = SSD_CONV - 1
    base = SUBLANE - kh

    @pl.when(c == 0)
    def _():
        s_scr[...] = s0_ref[...]
        ext_scr[:, base:SUBLANE, :] = conv0_ref[...]

    cw = cw_ref[...]
    cb = cb_ref[...]
    dtb = dtb_ref[...]
    neg_a = -jnp.exp(alog_ref[...])
    dvec = dvec_ref[...]
    gn = gn_ref[...]
    causal = _causal(C)
    bw = H * P
    hpg = H // G

    def one_sequence(bi):
        ext_scr[bi, SUBLANE:SUBLANE + C, :] = xbc_ref[bi]
        conv = cb
        for j in range(SSD_CONV):
            conv = conv + ext_scr[bi, pl.ds(base + j, C), :] * cw[j:j + 1, :]
        hist = ext_scr[bi, pl.ds(base + L, kh), :]
        ext_scr[bi, base:SUBLANE, :] = hist
        xc = _silu(conv)
        sx = xc[:, 0:bw]
        bm = xc[:, bw:bw + G * N]
        cm = xc[:, bw + G * N:]
        dt = _softplus(small_ref[bi] + dtb)
        logg = dt * neg_a
        if L < C:
            live = _live_rows(C, L)
            logg = jnp.where(live, logg, 0.0)
            bm = jnp.where(live, bm, 0.0)
        b = _cumsum_rows(logg)
        b_t = b.T
        ys = []
        for g in range(G):
            cg = cm[:, g * N:(g + 1) * N]
            bg = bm[:, g * N:(g + 1) * N]
            gmat = _dot_nt(_bf(cg), _bf(bg))
            for hh in range(hpg):
                h = g * hpg + hh
                ln = dt_lane + h
                bc = b[:, ln:ln + 1]
                br = b_t[ln:ln + 1, :]
                dec = jnp.exp(jnp.where(causal, bc - br, -jnp.inf))
                xdt = sx[:, h * P:(h + 1) * P] * dt[:, ln:ln + 1]
                st = s_scr[bi, h]
                y = _dot(_bf(jnp.concatenate([gmat * dec, cg * jnp.exp(bc)], axis=1)),
                         _bf(jnp.concatenate([xdt, st], axis=0)))
                bl = bc[C - 1:C, :]
                s_scr[bi, h] = jnp.exp(bl) * st + _dot_tn(_bf(bg * jnp.exp(bl - bc)), _bf(xdt))
                ys.append(y)
        y_all = jnp.concatenate(ys, axis=1) + dvec * sx
        o_ref[bi] = _rms(y_all * _silu(zg_ref[bi]), gn).astype(o_ref.dtype)

    for bi in range(NB):
        one_sequence(bi)

    @pl.when(c == pl.num_programs(0) - 1)
    def _():
        sout_ref[...] = s_scr[...]
        convout_ref[...] = ext_scr[:, base:SUBLANE, :]


def _ssd_call(z, zsm, xbc_idx, zg_idx, cw, cb, dtb, alog, dvec, gn, s0, conv0, l, nb, seq, C, L, dt_lane,
              out_dtype):
    H, N, P = s0.shape[1:]
    G = SSD_GROUPS
    ch = conv0.shape[2]
    bw = H * P
    nch = seq // C
    z3 = z.reshape(nb, seq, z.shape[1])
    zsm3 = zsm.reshape(nb, seq, LANE)
    kern = functools.partial(_ssd_kernel, NB=nb, C=C, L=L, H=H, G=G, N=N, P=P, dt_lane=dt_lane)
    vec = lambda w: pl.BlockSpec((None, 1, w), lambda c: (l, 0, 0))
    o, s, cv = pl.pallas_call(
        kern, grid=(nch,),
        in_specs=[pl.BlockSpec((nb, C, ch), lambda c: (0, c, xbc_idx)),
                  pl.BlockSpec((nb, C, bw), lambda c: (0, c, zg_idx)),
                  pl.BlockSpec((nb, C, LANE), lambda c: (0, c, 0)),
                  pl.BlockSpec((None, SSD_CONV, ch), lambda c: (l, 0, 0)),
                  vec(ch), vec(LANE), vec(LANE), vec(bw), vec(bw),
                  pl.BlockSpec((nb, H, N, P), lambda c: (0, 0, 0, 0)),
                  pl.BlockSpec((nb, SSD_CONV - 1, ch), lambda c: (0, 0, 0))],
        out_specs=[pl.BlockSpec((nb, C, bw), lambda c: (0, c, 0)),
                   pl.BlockSpec((nb, H, N, P), lambda c: (0, 0, 0, 0)),
                   pl.BlockSpec((nb, SSD_CONV - 1, ch), lambda c: (0, 0, 0))],
        out_shape=[jax.ShapeDtypeStruct((nb, seq, bw), out_dtype),
                   jax.ShapeDtypeStruct((nb, H, N, P), F32),
                   jax.ShapeDtypeStruct((nb, SSD_CONV - 1, ch), F32)],
        scratch_shapes=[pltpu.VMEM((nb, H, N, P), F32), pltpu.VMEM((nb, SUBLANE + C, ch), F32)],
        compiler_params=_cparams(("arbitrary",)), name="ssd",
    )(z3, z3, zsm3, cw, cb, dtb, alog, dvec, gn, s0, conv0)
    return o.reshape(nb * seq, bw), s, cv


def _mlstm_kernel(qk_ref, vo_ref, small_ref, bif_ref, gn_ref, c0_ref, nm0_ref, o_ref, cout_ref, nmout_ref,
                  c_scr, nm_scr, *, NB, C, L, H, DH, i_lane, f_lane):
    c = pl.program_id(0)

    @pl.when(c == 0)
    def _():
        c_scr[...] = c0_ref[...]
        nm_scr[...] = nm0_ref[...]

    causal_t = lax.broadcasted_iota(jnp.int32, (C, C), 0) <= lax.broadcasted_iota(jnp.int32, (C, C), 1)
    lane = lax.broadcasted_iota(jnp.int32, (1, LANE), 1)
    bif = bif_ref[...]
    gn = gn_ref[...]

    def one_sequence(bi):
        qk = qk_ref[bi]
        vo = vo_ref[bi]
        q = qk[:, 0:H * DH]
        k = qk[:, H * DH:] * (DH ** -0.5)
        v = vo[:, 0:H * DH]
        og = vo[:, H * DH:]
        pre = small_ref[bi] + bif
        logf = _log_sigmoid(pre)
        ipre = pre
        if L < C:
            live = _live_rows(C, L)
            logf = jnp.where(live, logf, 0.0)
            ipre = jnp.where(live, pre, NEG)
        b = _cumsum_rows(logf)
        b_t = b.T
        nm = nm_scr[bi]
        nm_bf = _bf(nm)
        m_row = nm[H:H + 1, :]
        outs = []
        for h in range(H):
            sl = slice(h * DH, (h + 1) * DH)
            br = b_t[f_lane + h:f_lane + h + 1, :]
            uc = ipre[:, i_lane + h:i_lane + h + 1] - b[:, f_lane + h:f_lane + h + 1]
            m_prev = nm[H:H + 1, h:h + 1]
            logd = jnp.where(causal_t, br + uc, -jnp.inf)
            gg = br + m_prev
            m = jnp.maximum(gg, jnp.max(logd, axis=0, keepdims=True))
            dm = jnp.exp(logd - m)
            wi = jnp.exp(gg - m)
            qf = q[:, sl]
            qh = _bf(qf)
            kh = k[:, sl]
            vh = _bf(v[:, sl])
            sc = _dot_nt(jnp.concatenate([_bf(kh), nm_bf], axis=0), qh)
            qk_t = sc[0:C, :] * dm
            cst = c_scr[bi, h]
            nq = sc[C + h:C + h + 1, :]
            den = jnp.sum(qk_t, axis=0, keepdims=True) + wi * nq
            rden = 1.0 / jnp.maximum(jnp.abs(den), jnp.exp(-m))
            rows = jnp.concatenate([wi, rden, jnp.zeros((SUBLANE - 2, C), F32)], axis=0)
            if C < LANE:
                rows = jnp.concatenate([rows, jnp.zeros((SUBLANE, LANE - C), F32)], axis=1)
            cols = rows.T[0:C, :]
            hid = (_dot_tn(_bf(qk_t), vh) + cols[:, 0:1] * _dot(qh, _bf(cst))) * cols[:, 1:2]
            m_new = m[:, C - 1:C]
            bl = br[:, C - 1:C]
            ws = jnp.exp(uc + (bl - m_new))
            wc = jnp.exp(bl + m_prev - m_new)
            nrow = nm[h:h + 1, :]
            wk = ws * kh
            c_scr[bi, h] = wc * cst + _dot_tn(_bf(wk), vh)
            nm_scr[bi, h:h + 1, :] = wc * nrow + jnp.sum(wk, axis=0, keepdims=True)
            m_row = jnp.where(lane == h, m_new, m_row)
            outs.append(_rms_mxu(hid, gn[:, sl]))
        nm_scr[bi, H:H + 1, :] = m_row
        o_ref[bi] = (jnp.concatenate(outs, axis=1) * _sigmoid(og)).astype(o_ref.dtype)

    for bi in range(NB):
        one_sequence(bi)

    @pl.when(c == pl.num_programs(0) - 1)
    def _():
        cout_ref[...] = c_scr[...]
        nmout_ref[...] = nm_scr[...]


def _mlstm_call(z, zsm, qk_idx, vo_idx, bif, gn, c0, nm0, l, nb, seq, C, L, i_lane, f_lane, out_dtype):
    H, DH = c0.shape[1:3]
    nch = seq // C
    z3 = z.reshape(nb, seq, z.shape[1])
    zsm3 = zsm.reshape(nb, seq, LANE)
    kern = functools.partial(_mlstm_kernel, NB=nb, C=C, L=L, H=H, DH=DH, i_lane=i_lane, f_lane=f_lane)
    o, cc, nm = pl.pallas_call(
        kern, grid=(nch,),
        in_specs=[pl.BlockSpec((nb, C, 2 * H * DH), lambda c: (0, c, qk_idx)),
                  pl.BlockSpec((nb, C, 2 * H * DH), lambda c: (0, c, vo_idx)),
                  pl.BlockSpec((nb, C, LANE), lambda c: (0, c, 0)),
                  pl.BlockSpec((None, 1, LANE), lambda c: (l, 0, 0)),
                  pl.BlockSpec((None, 1, H * DH), lambda c: (l, 0, 0)),
                  pl.BlockSpec((nb, H, DH, DH), lambda c: (0, 0, 0, 0)),
                  pl.BlockSpec((nb, SUBLANE, LANE), lambda c: (0, 0, 0))],
        out_specs=[pl.BlockSpec((nb, C, H * DH), lambda c: (0, c, 0)),
                   pl.BlockSpec((nb, H, DH, DH), lambda c: (0, 0, 0, 0)),
                   pl.BlockSpec((nb, SUBLANE, LANE), lambda c: (0, 0, 0))],
        out_shape=[jax.ShapeDtypeStruct((nb, seq, H * DH), out_dtype),
                   jax.ShapeDtypeStruct((nb, H, DH, DH), F32),
                   jax.ShapeDtypeStruct((nb, SUBLANE, LANE), F32)],
        scratch_shapes=[pltpu.VMEM((nb, H, DH, DH), F32), pltpu.VMEM((nb, SUBLANE, LANE), F32)],
        compiler_params=_cparams(("arbitrary",)), name="mlstm",
    )(z3, z3, zsm3, bif, gn, c0, nm0)
    return o.reshape(nb * seq, H * DH), cc, nm


def _nsa_prompt_kernel(q_ref, kv_ref, small_ref, o_ref, kvb_ref, cmp_ref, *, T, TQ, DH, g_lane):
    qi = pl.program_id(1)
    G, R = NSA_KV, NSA_R
    gw = G * DH
    ncb = T // CMP_BLOCK
    nsb = T // SEL_BLOCK
    ratio = SEL_BLOCK // CMP_BLOCK
    slopes = _alibi_slopes()

    @pl.when(qi == 0)
    def _():
        kvb_ref[...] = _bf(kv_ref[...])
        hi, lo = _split2(kv_ref[:, 0:2 * gw])
        jj = lax.broadcasted_iota(jnp.int32, (ncb, T), 0) * CMP_BLOCK
        tt = lax.broadcasted_iota(jnp.int32, (ncb, T), 1)
        pm = jnp.where((tt >= jj) & (tt < jj + CMP_BLOCK), 1.0 / CMP_BLOCK, 0.0).astype(BF16)
        cmp_ref[...] = _bf(_dot(pm, hi) + _dot(pm, lo))

    q = q_ref[...]
    gate = _sigmoid(small_ref[...])
    posl = qi * TQ + lax.broadcasted_iota(jnp.int32, (1, TQ), 1)
    cbs = lax.broadcasted_iota(jnp.int32, (ncb, 1), 0)
    cvalid = (cbs * CMP_BLOCK + (CMP_BLOCK - 1)) <= posl
    cdist = posl.astype(F32) - (cbs.astype(F32) * CMP_BLOCK + (CMP_BLOCK - 1) / 2.0)
    sbs = lax.broadcasted_iota(jnp.int32, (nsb, 1), 0)
    svalid = sbs * SEL_BLOCK <= posl
    forced = (sbs == 0) | ((sbs * SEL_BLOCK <= posl) & (posl < sbs * SEL_BLOCK + SEL_BLOCK))
    pi = lax.broadcasted_iota(jnp.int32, (nsb, ncb), 0) * ratio
    pj = lax.broadcasted_iota(jnp.int32, (nsb, ncb), 1)
    pool = jnp.where((pj >= pi) & (pj < pi + ratio), 1.0, 0.0).astype(BF16)
    brow = lax.broadcasted_iota(jnp.int32, (nsb, TQ), 0)
    row2 = lax.broadcasted_iota(jnp.int32, (R * TQ, 1), 0)
    rr = lax.broadcasted_iota(jnp.int32, (TQ, TQ), 0)
    cc = lax.broadcasted_iota(jnp.int32, (TQ, TQ), 1)
    lower = rr >= cc
    lower2 = jnp.concatenate([lower] * R, axis=0)
    col = lax.broadcasted_iota(jnp.int32, (1, TQ), 1)
    nfar = WINDOW // TQ

    def tile(carry, kt, q2, slope2, kcol, vcol, mask):
        m, l, acc = carry
        k0 = pl.multiple_of(kt * TQ, TQ)
        ks = kvb_ref[pl.ds(k0, TQ), kcol:kcol + DH]
        vs = kvb_ref[pl.ds(k0, TQ), vcol:vcol + DH]
        s = _dot_nt(q2, ks) + slope2 * (k0 + col).astype(F32)
        if mask is not None:
            s = jnp.where(mask, s, NEG)
        mn = jnp.maximum(m, jnp.max(s, axis=1, keepdims=True))
        p = jnp.exp(s - mn)
        alpha = jnp.exp(m - mn)
        return (mn, alpha * l + jnp.sum(p, axis=1, keepdims=True), alpha * acc + _dot(_bf(p), vs))

    def sel_mask(sel, kt):
        kpos = kt * TQ + col
        expand = _bf(jnp.where((kpos >= sbs * SEL_BLOCK) & (kpos < sbs * SEL_BLOCK + SEL_BLOCK), 1.0, 0.0))
        return _dot_tn(sel, expand) > 0.5

    init = (jnp.full((R * TQ, 1), NEG, F32), jnp.zeros((R * TQ, 1), F32), jnp.zeros((R * TQ, DH), F32))
    o_cmp, sels, q2s, slope2s, sel_carry, o_win = [], [], [], [], [], []
    for g in range(G):
        kc = cmp_ref[:, g * DH:(g + 1) * DH]
        vc = cmp_ref[:, gw + g * DH:gw + (g + 1) * DH]
        qs = []
        imp = jnp.zeros((nsb, TQ), F32)
        for r in range(R):
            hd = g * R + r
            qh = _bf(q[:, hd * DH:(hd + 1) * DH] * (DH ** -0.5))
            qs.append(qh)
            s = jnp.where(cvalid, _dot_nt(kc, qh) - slopes[hd] * cdist, NEG)
            e = jnp.exp(s - jnp.max(s, axis=0, keepdims=True))
            p = jnp.where(cvalid, e / jnp.sum(e, axis=0, keepdims=True), 0.0)
            o_cmp.append(_dot_tn(_bf(p), vc))
            ph, plo = _split2(p)
            imp = imp + _dot(pool, ph) + _dot(pool, plo)
        score = jnp.where(svalid, jnp.where(forced, FORCE_SCORE, imp), -1.0)
        ahead = jnp.zeros((nsb, TQ), jnp.int32)
        for j in range(nsb):
            sj = score[j:j + 1, :]
            ahead = ahead + jnp.where(sj > score, 1, jnp.where(sj == score, jnp.where(brow > j, 1, 0), 0))
        sel = _bf(jnp.where((ahead < SEL_TOPK) & (score >= 0.0), 1.0, 0.0))
        q2 = jnp.concatenate(qs, axis=0)
        slope2 = slopes[g * R]
        for r in range(1, R):
            slope2 = jnp.where(row2 >= r * TQ, slopes[g * R + r], slope2)
        sels.append(sel)
        q2s.append(q2)
        slope2s.append(slope2)
        sel_carry.append(tile(init, qi, q2, slope2, 2 * gw + g * DH, 3 * gw + g * DH,
                              jnp.concatenate([sel_mask(sel, qi) & lower] * R, axis=0)))
        kcol, vcol = 4 * gw + g * DH, 5 * gw + g * DH
        carry = tile(init, qi, q2, slope2, kcol, vcol, lower2)
        for back in range(1, nfar + 1):
            kt = jnp.maximum(qi - back, 0)
            live = qi >= back
            mask = jnp.logical_and(jnp.logical_not(lower2), live) if back == nfar else live
            carry = tile(carry, kt, q2, slope2, kcol, vcol, mask)
        o_win.append(carry[2] / carry[1])

    def sel_tiles(kt, carries):
        return tuple(tile(carries[g], kt, q2s[g], slope2s[g], 2 * gw + g * DH, 3 * gw + g * DH,
                          jnp.concatenate([sel_mask(sels[g], kt)] * R, axis=0)) for g in range(G))

    sel_carry = lax.fori_loop(0, qi, sel_tiles, tuple(sel_carry))
    outs = [None] * (G * R)
    for g in range(G):
        o_slc = sel_carry[g][2] / sel_carry[g][1]
        for r in range(R):
            hd = g * R + r
            gc = gate[:, g_lane + hd:g_lane + hd + 1]
            gs = gate[:, g_lane + NSA_HEADS + hd:g_lane + NSA_HEADS + hd + 1]
            gwn = gate[:, g_lane + 2 * NSA_HEADS + hd:g_lane + 2 * NSA_HEADS + hd + 1]
            outs[hd] = gc * o_cmp[hd] + gs * o_slc[r * TQ:(r + 1) * TQ] + gwn * o_win[g][r * TQ:(r + 1) * TQ]
    o_ref[...] = jnp.concatenate(outs, axis=1).astype(o_ref.dtype)


def _nsa_prompt_call(z, zsm, q_idx, kv_idx, nb, seq, g_lane, tq=256):
    dh = LANE
    kvw = 6 * NSA_KV * dh
    qw = NSA_HEADS * dh
    nq = seq // tq
    assert WINDOW % tq == 0 and seq % tq == 0
    kern = functools.partial(_nsa_prompt_kernel, T=seq, TQ=tq, DH=dh, g_lane=g_lane)
    return pl.pallas_call(
        kern, grid=(nb, nq),
        in_specs=[pl.BlockSpec((tq, qw), lambda b, i: (b * nq + i, q_idx)),
                  pl.BlockSpec((seq, kvw), lambda b, i: (b, kv_idx)),
                  pl.BlockSpec((tq, LANE), lambda b, i: (b * nq + i, 0))],
        out_specs=pl.BlockSpec((tq, qw), lambda b, i: (b * nq + i, 0)),
        out_shape=jax.ShapeDtypeStruct((nb * seq, qw), BF16),
        scratch_shapes=[pltpu.VMEM((seq, kvw), BF16), pltpu.VMEM((seq // CMP_BLOCK, 2 * NSA_KV * dh), BF16)],
        compiler_params=_cparams(("arbitrary", "arbitrary")), name="nsa_prompt",
    )(z, z, zsm)


def _nsa_means_kernel(pt_ref, *refs, PP, DH):
    ins = refs[:PP]
    o_ref = refs[PP]
    nper = PAGE_SIZE // CMP_BLOCK
    for i in range(PP):
        x = ins[i][...].reshape(nper, CMP_BLOCK, 2, NSA_KV, DH)
        o_ref[i * nper:(i + 1) * nper] = jnp.sum(x, axis=1) * (1.0 / CMP_BLOCK)


def _nsa_means_call(cache, page_table, l, pp=32):
    nb, n_pages = page_table.shape
    dh = cache.shape[5]
    nper = PAGE_SIZE // CMP_BLOCK
    assert cache.shape[2] == PAGE_SIZE and cache.shape[4] == NSA_KV and n_pages % pp == 0
    in_specs = [pl.BlockSpec((None, None, PAGE_SIZE, 2, NSA_KV, dh),
                             lambda b, s, pt, i=i: (l, pt[b, s * pp + i], 0, 0, 0, 0)) for i in range(pp)]
    gs = pltpu.PrefetchScalarGridSpec(
        num_scalar_prefetch=1, grid=(nb, n_pages // pp), in_specs=in_specs,
        out_specs=pl.BlockSpec((None, nper * pp, 2, NSA_KV, dh), lambda b, s, pt: (b, s, 0, 0, 0)))
    means = pl.pallas_call(
        functools.partial(_nsa_means_kernel, PP=pp, DH=dh), grid_spec=gs,
        out_shape=jax.ShapeDtypeStruct((nb, n_pages * nper, 2, NSA_KV, dh), F32),
        compiler_params=_cparams(("arbitrary", "arbitrary")), name="nsa_cmp_means",
    )(page_table, *([cache] * pp))
    return means.reshape(nb, n_pages * nper, 2 * NSA_KV * dh)


def _nsa_score_kernel(q_ref, kvc_ref, ocmp_ref, ids_ref, *, DH, qpos):
    G, R = NSA_KV, NSA_R
    gw = G * DH
    ncb = kvc_ref.shape[0]
    ratio = SEL_BLOCK // CMP_BLOCK
    nsb = ncb // ratio
    slopes = _alibi_slopes()
    q = q_ref[...]
    kvc = _bf(kvc_ref[...])
    rows = q.shape[0]
    cb = lax.broadcasted_iota(jnp.int32, (1, ncb), 1)
    cvalid = (cb * CMP_BLOCK + (CMP_BLOCK - 1)) <= qpos
    cdist = float(qpos) - (cb.astype(F32) * CMP_BLOCK + (CMP_BLOCK - 1) / 2.0)
    pi = lax.broadcasted_iota(jnp.int32, (ncb, nsb), 0)
    pj = lax.broadcasted_iota(jnp.int32, (ncb, nsb), 1) * ratio
    pool = jnp.where((pi >= pj) & (pi < pj + ratio), 1.0, 0.0).astype(BF16)
    lane = lax.broadcasted_iota(jnp.int32, (rows, nsb), 1).astype(F32)
    lane_out = lax.broadcasted_iota(jnp.int32, (rows, LANE), 1)
    rowi = lax.broadcasted_iota(jnp.int32, (rows, 1), 0)
    o_cmp = []
    id_rows = []
    for g in range(G):
        kc = kvc[:, g * DH:(g + 1) * DH]
        vc = kvc[:, gw + g * DH:gw + (g + 1) * DH]
        qg = jnp.concatenate([q[0:1, (g * R + r) * DH:(g * R + r + 1) * DH] for r in range(R)]
                             + [jnp.zeros((rows - R, DH), F32)], axis=0)
        slope = slopes[g * R]
        for r in range(1, R):
            slope = jnp.where(rowi == r, slopes[g * R + r], slope)
        s = jnp.where(cvalid, _dot_nt(_bf(qg * (DH ** -0.5)), kc) - slope * cdist, NEG)
        p = jnp.where(cvalid, _softmax_rows(s), 0.0)
        og = _dot(_bf(p), vc)
        psum = p[0:1, :]
        for r in range(R):
            o_cmp.append(og[r:r + 1, :])
            if r:
                psum = psum + p[r:r + 1, :]
        ph, plo = _split2(jnp.concatenate([psum, jnp.zeros((rows - 1, ncb), F32)], axis=0))
        imp = _dot(ph, pool) + _dot(plo, pool)
        val = jnp.where(lane == 0.0, -1.0, imp)
        ids = jnp.zeros((rows, LANE), F32)
        for it in range(SEL_TOPK - 2):
            mx = jnp.max(val, axis=1, keepdims=True)
            ix = jnp.min(jnp.where(val == mx, lane, float(nsb)), axis=1, keepdims=True)
            ids = jnp.where(lane_out == it + 1, ix, ids)
            val = jnp.where(lane == ix, -2.0, val)
        id_rows.append(ids[0:1, :])
    ocmp_ref[...] = jnp.concatenate([jnp.concatenate(o_cmp, axis=1),
                                     jnp.zeros((rows - 1, len(o_cmp) * DH), F32)], axis=0)
    pad = jnp.zeros((ids_ref.shape[0] - G, LANE), F32)
    ids_ref[...] = jnp.concatenate(id_rows + [pad], axis=0).astype(jnp.int32)


def _nsa_score_call(z_s, q_idx, kvc, qpos):
    nb, ncb, w = kvc.shape
    dh = w // (2 * NSA_KV)
    qw = NSA_HEADS * dh
    return pl.pallas_call(
        functools.partial(_nsa_score_kernel, DH=dh, qpos=qpos), grid=(nb,),
        in_specs=[pl.BlockSpec((SROWS, qw), lambda b: (b, q_idx)),
                  pl.BlockSpec((None, ncb, w), lambda b: (b, 0, 0))],
        out_specs=[pl.BlockSpec((SROWS, qw), lambda b: (b, 0)),
                   pl.BlockSpec((None, SUBLANE, LANE), lambda b: (b, 0, 0))],
        out_shape=[jax.ShapeDtypeStruct((nb * SROWS, qw), F32),
                   jax.ShapeDtypeStruct((nb, SUBLANE, LANE), jnp.int32)],
        compiler_params=_cparams(("arbitrary",)), name="nsa_cmp_score",
    )(z_s, kvc)


def _nsa_gather_kernel(sel_ref, q_ref, new_ref, small_ref, ocmp_ref, win_ref, *rest,
                       DH, NSEL, qpos, g_lane):
    kv = rest[:NSA_KV * NSEL]
    o_ref = rest[NSA_KV * NSEL]
    b = pl.program_id(0)
    G, R = NSA_KV, NSA_R
    gw = G * DH
    slopes = _alibi_slopes()
    q = q_ref[...]
    new = new_ref[...]
    gate = _sigmoid(small_ref[...])
    ocmp = ocmp_ref[...]
    wlen = win_ref.shape[0]
    jw = lax.broadcasted_iota(jnp.int32, (1, wlen), 1)
    distw = wlen - jw
    validw = (distw < WINDOW) & (qpos - distw >= 0)
    off = lax.broadcasted_iota(jnp.int32, (1, SEL_BLOCK), 1)
    rows = q.shape[0]
    rowi = lax.broadcasted_iota(jnp.int32, (rows, 1), 0)
    fill = jnp.zeros((rows - R, DH), F32)

    def per_head(vals):
        out = vals[0]
        for r in range(1, R):
            out = jnp.where(rowi == r, vals[r], out)
        return out

    outs = []
    for g in range(G):
        ksn = _bf(new[0:1, 2 * gw + g * DH:2 * gw + (g + 1) * DH]).astype(F32)
        vsn = _bf(new[0:1, 3 * gw + g * DH:3 * gw + (g + 1) * DH]).astype(F32)
        kwn = _bf(new[0:1, 4 * gw + g * DH:4 * gw + (g + 1) * DH]).astype(F32)
        vwn = _bf(new[0:1, 5 * gw + g * DH:5 * gw + (g + 1) * DH]).astype(F32)
        kw = _bf(win_ref[:, 0, g, :])
        vw = _bf(win_ref[:, 1, g, :])
        kblk = [_bf(kv[g * NSEL + i][:, 0, g, :]) for i in range(NSEL)]
        vblk = [_bf(kv[g * NSEL + i][:, 1, g, :]) for i in range(NSEL)]
        nslot = NSA_KV * SEL_TOPK
        dists = [qpos - (sel_ref[b, 2 * nslot + g * SEL_TOPK + i] * SEL_BLOCK + off) for i in range(NSEL)]
        heads = [g * R + r for r in range(R)]
        qh = _bf(jnp.concatenate([q[0:1, hd * DH:(hd + 1) * DH] for hd in heads] + [fill], axis=0) * (DH ** -0.5))
        qf = qh.astype(F32)
        slope = per_head([slopes[hd] for hd in heads])
        s_list = [jnp.where(dists[i] >= 0, _dot_nt(qh, kblk[i]) - slope * dists[i].astype(F32), NEG)
                  for i in range(NSEL)]
        s_new = jnp.sum(qf * ksn, axis=1, keepdims=True)
        mx = s_new
        for s in s_list:
            mx = jnp.maximum(mx, jnp.max(s, axis=1, keepdims=True))
        p_new = jnp.exp(s_new - mx)
        den = p_new
        acc = p_new * vsn
        for i, s in enumerate(s_list):
            p = jnp.exp(s - mx)
            den = den + jnp.sum(p, axis=1, keepdims=True)
            acc = acc + _dot(_bf(p), vblk[i])
        o_slc = acc / den
        s = jnp.where(validw, _dot_nt(qh, kw) - slope * distw.astype(F32), NEG)
        s_new = jnp.sum(qf * kwn, axis=1, keepdims=True)
        mx = jnp.maximum(s_new, jnp.max(s, axis=1, keepdims=True))
        p = jnp.exp(s - mx)
        p_new = jnp.exp(s_new - mx)
        o_win = (_dot(_bf(p), vw) + p_new * vwn) / (jnp.sum(p, axis=1, keepdims=True) + p_new)
        gc, gs, gwn = [per_head([gate[0:1, g_lane + c * NSA_HEADS + hd:g_lane + c * NSA_HEADS + hd + 1]
                                 for hd in heads]) for c in range(3)]
        o_c = jnp.concatenate([ocmp[0:1, hd * DH:(hd + 1) * DH] for hd in heads] + [fill], axis=0)
        res = gc * o_c + gs * o_slc + gwn * o_win
        outs += [res[r:r + 1, :] for r in range(R)]
    o_ref[...] = jnp.concatenate([jnp.concatenate(outs, axis=1), jnp.zeros((rows - 1, G * R * DH), F32)], axis=0)


def _nsa_gather_call(z_s, zsm_s, q_idx, kv_idx, ocmp, win, cache, page_table, ids, l, qpos, g_lane):
    nb = page_table.shape[0]
    dh = cache.shape[5]
    G = NSA_KV
    qw = NSA_HEADS * dh
    nsel = SEL_TOPK - 1
    bpp = PAGE_SIZE // SEL_BLOCK
    wlen = win.shape[2]
    nslot = G * SEL_TOPK
    sel = jnp.concatenate([jnp.take_along_axis(page_table, ids // bpp, axis=1), ids % bpp, ids], axis=1)
    in_specs = [pl.BlockSpec((SROWS, qw), lambda b, s: (b, q_idx)),
                pl.BlockSpec((SROWS, 6 * G * dh), lambda b, s: (b, kv_idx)),
                pl.BlockSpec((SROWS, LANE), lambda b, s: (b, 0)),
                pl.BlockSpec((SROWS, qw), lambda b, s: (b, 0)),
                pl.BlockSpec((None, None, wlen, 2, G, dh), lambda b, s: (l, b, 0, 0, 0, 0))]
    for g in range(G):
        for i in range(nsel):
            in_specs.append(pl.BlockSpec(
                (None, None, SEL_BLOCK, 2, G, dh),
                lambda b, s, j=g * SEL_TOPK + i: (l, s[b, j], s[b, nslot + j], 1, 0, 0)))
    gs = pltpu.PrefetchScalarGridSpec(
        num_scalar_prefetch=1, grid=(nb,), in_specs=in_specs,
        out_specs=pl.BlockSpec((SROWS, qw), lambda b, s: (b, 0)))
    kern = functools.partial(_nsa_gather_kernel, DH=dh, NSEL=nsel, qpos=qpos, g_lane=g_lane)
    return pl.pallas_call(
        kern, grid_spec=gs,
        out_shape=jax.ShapeDtypeStruct((nb * SROWS, qw), F32),
        compiler_params=_cparams(("arbitrary",)), name="nsa_sel_win",
    )(sel, z_s, z_s, zsm_s, ocmp, win, *([cache] * (G * nsel)))


def _split_cols_kernel(*refs, n_in, G, DH):
    z_refs = refs[:n_in]
    o_ref = refs[-1]
    per = z_refs[0].shape[1] // DH
    for idx in range(n_in * per):
        o_ref[:, idx // G, idx % G, :] = z_refs[idx // per][:, (idx % per) * DH:(idx % per + 1) * DH]


def _split_cols(z, col0, ncomp, prev, l, depth, nblk, rows, row_block, G, dh):
    width = ncomp * G * dh
    n_in = max(1, width // Z_TILE)
    wblk = width // n_in
    assert col0 % wblk == 0
    kern = functools.partial(_split_cols_kernel, n_in=n_in, G=G, DH=dh)
    in_specs = [pl.BlockSpec((rows, wblk), lambda j, k=k: (row_block(j), col0 // wblk + k)) for k in range(n_in)]
    args = [z] * n_in
    aliases = {}
    if prev is not None:
        in_specs.append(pl.BlockSpec(memory_space=pl.ANY))
        args.append(prev)
        aliases = {n_in: 0}
    return pl.pallas_call(
        kern, grid=(nblk,), in_specs=in_specs,
        out_specs=pl.BlockSpec((None, None, rows, ncomp, G, dh), lambda j: (l, j, 0, 0, 0, 0)),
        out_shape=jax.ShapeDtypeStruct((depth, nblk, rows, ncomp, G, dh), z.dtype),
        input_output_aliases=aliases,
        compiler_params=_cparams(("arbitrary",)), name="nsa_rows_out",
    )(*args)


def _lane_vec(depth, lane, values):
    n = values.shape[1]
    return jnp.zeros((depth, 1, LANE), F32).at[:, 0, lane:lane + n].set(values.astype(F32))


def kernel(x_prompt, x_sample, cache_nsa_kv, cache_nsa_win, state_gla, state_ssd, state_ssd_conv, state_mlstm_c, state_mlstm_n, state_mlstm_m, page_table, c_prompt, c_sample, ada_w, ada_b, norm_mix_g, norm_ffn_g, w_in, gla_w_a, gla_b_a, gla_norm_g, ssd_conv_w, ssd_conv_b, ssd_dt_bias, ssd_a_log, ssd_d, ssd_norm_g, ml_b_i, ml_b_f, ml_norm_g, w_branch, w_out, ffn_w_gate, ffn_w_up, ffn_w_down, final_norm_g):
    nbp, seq, d = x_prompt.shape
    nbs, dec_seq, _ = x_sample.shape
    assert dec_seq == 1 and seq % CHUNK == 0
    depth = w_in.shape[0]
    past_len = page_table.shape[1] * PAGE_SIZE
    bw = d // 4
    dh = bw // NSA_HEADS
    gw = NSA_KV * dh
    src, dst, n_packed, _ = _packed_offsets(d)
    kv_off = dst['nsa_kv']
    q_idx = dst['nsa_q'] // bw
    kv_idx = kv_off // (6 * gw)
    xbc_idx = dst['ssd_xbc'] // src['ssd_xbc'][1]
    zg_idx = dst['ssd_z'] // bw
    mlqk_idx = dst['ml_q'] // (2 * bw)
    mlvo_idx = dst['ml_v'] // (2 * bw)
    for name, width in (('gla_q', 3 * bw), ('nsa_q', bw), ('nsa_kv', 6 * gw), ('ssd_xbc', src['ssd_xbc'][1]),
                        ('ssd_z', bw), ('ml_q', 2 * bw), ('ml_v', 2 * bw), ('merge', Z_TILE)):
        assert dst[name] % width == 0
    assert dst['gla_q'] == 0
    g_lane, dt_lane, i_lane, f_lane = dst['nsa_g'], dst['ssd_dt'], dst['ml_i'], dst['ml_f']

    crow = 2 * SUBLANE
    c_all = jnp.concatenate([c_prompt, c_sample, jnp.zeros((crow - nbp - nbs, d), F32)], axis=0)
    mod = _ada(c_all, ada_w, ada_b)
    modp = mod[:, :nbp].reshape(depth, nbp, 1, 6 * d)
    mods = jnp.repeat(mod[:, nbp:nbp + nbs], SROWS, axis=1)

    w_mix, w_gate, w_small = _pack_w_in(w_in)
    n_mix = dst['merge']
    assert n_mix % (4 * LANE) == 0
    r3 = lambda a: a.reshape(depth, 1, a.shape[-1])
    wa_pad = jnp.zeros((depth, LANE, gla_w_a.shape[2]), F32).at[:, dst['gla_a']:dst['gla_a'] + GLA_RANK].set(gla_w_a)
    dtb = _lane_vec(depth, dt_lane, ssd_dt_bias)
    alog = _lane_vec(depth, dt_lane, ssd_a_log)
    dvec = jnp.repeat(ssd_d, bw // SSD_HEADS, axis=1).reshape(depth, 1, bw)
    bif = _lane_vec(depth, i_lane, ml_b_i) + _lane_vec(depth, f_lane, ml_b_f)

    xp = x_prompt.reshape(nbp * seq, d)
    xs = jnp.zeros((nbs, SROWS, d), F32).at[:, 0].set(x_sample[:, 0]).reshape(nbs * SROWS, d)

    zeros_like_p = lambda a: jnp.zeros((nbp,) + a.shape[2:], F32)
    gla0, ssd0, conv0, mlc0 = (zeros_like_p(a) for a in (state_gla, state_ssd, state_ssd_conv, state_mlstm_c))
    nm0 = jnp.zeros((nbp, SUBLANE, LANE), F32)
    nm_s = (jnp.zeros((depth, nbs, SUBLANE, LANE), F32).at[:, :, :ML_HEADS, :].set(state_mlstm_n)
            .at[:, :, ML_HEADS, :ML_HEADS].set(state_mlstm_m))

    out = {k: [] for k in ('rows_s', 'win_s', 'gla_p', 'gla_s', 'ssd_p', 'ssd_s', 'conv_p',
                           'conv_s', 'mlc_p', 'mlc_s', 'mln_p', 'mln_s', 'mlm_p', 'mlm_s')}
    rows_p = win_p = None
    rows_tile = 1024
    keep = min(WINDOW, seq)
    assert seq % keep == 0 and (nbp * seq) % rows_tile == 0
    for l in range(depth):
        hp = _norm_mod_prompt(xp, r3(norm_mix_g), modp, l, 0, 1, seq)
        hs = _norm_mod_sample(xs, r3(norm_mix_g), mods, l, 0, 1)
        zp, zs = _proj_nt(hp, hs, w_mix, l, 1024, n_mix // 4, "proj_in")
        zgp, zgs = _proj_nt(hp, hs, w_small, l, 2048, LANE, "proj_small")
        gtp, gts = _proj_nt(hp, hs, w_gate, l, 1024, d, "proj_gate", BF16)

        o_gla_p, gla_p = _gla_call(zp, zgp, wa_pad, r3(gla_b_a), r3(gla_norm_g), gla0, l, nbp, seq, 2 * CHUNK,
                                   2 * CHUNK, BF16)
        o_gla_s, gla_s = _gla_call(zs, zgs, wa_pad, r3(gla_b_a), r3(gla_norm_g), state_gla[l], l, nbs, SROWS,
                                   SROWS, 1, F32)
        ssd_args = (ssd_conv_w, r3(ssd_conv_b), dtb, alog, dvec, r3(ssd_norm_g))
        o_ssd_p, ssd_p, conv_p = _ssd_call(zp, zgp, xbc_idx, zg_idx, *ssd_args, ssd0, conv0, l, nbp, seq,
                                           CHUNK, CHUNK, dt_lane, BF16)
        o_ssd_s, ssd_s, conv_s = _ssd_call(zs, zgs, xbc_idx, zg_idx, *ssd_args, state_ssd[l],
                                           state_ssd_conv[l], l, nbs, SROWS, SROWS, 1, dt_lane, F32)
        o_ml_p, mlc_p, mlnm_p = _mlstm_call(zp, zgp, mlqk_idx, mlvo_idx, bif, r3(ml_norm_g), mlc0, nm0, l,
                                            nbp, seq, CHUNK, CHUNK, i_lane, f_lane, BF16)
        o_ml_s, mlc_s, mlnm_s = _mlstm_call(zs, zgs, mlqk_idx, mlvo_idx, bif, r3(ml_norm_g), state_mlstm_c[l],
                                            nm_s[l], l, nbs, SROWS, SROWS, 1, i_lane, f_lane, F32)
        o_nsa_p = _nsa_prompt_call(zp, zgp, q_idx, kv_idx, nbp, seq, g_lane)
        kvc = _nsa_means_call(cache_nsa_kv, page_table, l)
        ocmp, ids = _nsa_score_call(zs, q_idx, kvc, past_len)
        ids2 = ids[:, :NSA_KV, :SEL_TOPK].reshape(nbs, NSA_KV * SEL_TOPK)
        o_nsa_s = _nsa_gather_call(zs, zgs, q_idx, kv_idx, ocmp, cache_nsa_win, cache_nsa_kv, page_table, ids2, l,
                                   past_len, g_lane)

        mp, ms = _merge((o_gla_p, o_nsa_p, o_ssd_p, o_ml_p), (o_gla_s, o_nsa_s, o_ssd_s, o_ml_s), gtp, gts, w_branch,
                        l, 1024, 512)
        xp, xs = _mm_resid(mp, ms, w_out, xp, xs, modp, mods, l, 2, seq, 1024, 512, "proj_out")
        hp = _norm_mod_prompt(xp, r3(norm_ffn_g), modp, l, 3, 4, seq)
        hs = _norm_mod_sample(xs, r3(norm_ffn_g), mods, l, 3, 4)
        ap, a_s = _mm_swiglu(hp, hs, ffn_w_gate, ffn_w_up, l, 1024, 512)
        xp, xs = _mm_resid(ap, a_s, ffn_w_down, xp, xs, modp, mods, l, 5, seq, 512, 512, "ffn_down")

        zs3 = zs.reshape(nbs, SROWS, n_mix)[:, :1]
        rows_p = _split_cols(zp, kv_off, 4, rows_p, l, depth, nbp * seq // rows_tile, rows_tile, lambda j: j,
                             NSA_KV, dh)
        win_p = _split_cols(zp, kv_off + 4 * gw, 2, win_p, l, depth, nbp, keep,
                            lambda j: (j + 1) * (seq // keep) - 1, NSA_KV, dh)
        out['rows_s'].append(zs3[:, :, kv_off:kv_off + 4 * gw].reshape(nbs, 1, 4, NSA_KV, dh))
        win_all = jnp.concatenate([cache_nsa_win[l], zs3[:, :, kv_off + 4 * gw:kv_off + 6 * gw]
                                   .reshape(nbs, 1, 2, NSA_KV, dh)], axis=1)
        out['win_s'].append(win_all[:, win_all.shape[1] - min(WINDOW, win_all.shape[1]):])
        for key, val in (('gla_p', gla_p), ('gla_s', gla_s), ('ssd_p', ssd_p), ('ssd_s', ssd_s), ('conv_p', conv_p),
                         ('conv_s', conv_s), ('mlc_p', mlc_p), ('mlc_s', mlc_s)):
            out[key].append(val)
        for tag, nm in (('p', mlnm_p), ('s', mlnm_s)):
            out['mln_' + tag].append(nm[:, :ML_HEADS, :])
            out['mlm_' + tag].append(nm[:, ML_HEADS, :ML_HEADS])

    y_prompt = _final_norm(xp, final_norm_g).reshape(nbp, seq, d)
    y_sample = _final_norm(xs, final_norm_g).reshape(nbs, SROWS, d)[:, :1]
    st = {k: jnp.stack(v) for k, v in out.items()}
    st['rows_p'] = rows_p.reshape(depth, nbp, seq, 4, NSA_KV, dh)
    st['win_p'] = win_p
    return (y_prompt, y_sample, st['rows_p'], st['rows_s'], st['win_p'], st['win_s'], st['gla_p'], st['gla_s'],
            st['ssd_p'], st['ssd_s'], st['conv_p'], st['conv_s'], st['mlc_p'], st['mlc_s'], st['mln_p'], st['mln_s'],
            st['mlm_p'], st['mlm_s'])
```
